```python
import jax, jax.numpy as jnp
from jax import lax
import numpy as np

D_MODEL = 2048
BATCH = 2
SEQ = 8192
DEPTH = 4

GRID_W = 64
CTX_LEN = 256
MLSTM_WIDTH = D_MODEL // 2
MLSTM_HEADS = 8
MLSTM_HEAD_DIM = MLSTM_WIDTH // MLSTM_HEADS
MLSTM_CHUNK = 128
FORGET_BIAS_LO = 3.0
FORGET_BIAS_HI = 6.0
POOL_WIDTH = D_MODEL // 2
POOL_WINDOWS = (2, 4, 8, 16)
POOL_GROUPS = len(POOL_WINDOWS)
POOL_GROUP_DIM = POOL_WIDTH // POOL_GROUPS
EVEN_IN = 4 * MLSTM_WIDTH + 4 * MLSTM_HEADS + POOL_WIDTH
EVEN_MIX = MLSTM_WIDTH + POOL_WIDTH
NA_HEADS = 16
NA_HEAD_DIM = D_MODEL // NA_HEADS
NA_WIDTH = NA_HEADS * NA_HEAD_DIM
NA_ROWS = 8
NA_COLS = 16
RPB_R = 2 * NA_ROWS - 1
RPB_C = 2 * NA_COLS - 1
N_EXPERTS = 16
EXPERT_FF = D_MODEL // 2
CAPACITY_FACTOR = 2
ROPE_BASE = 10000.0
EPS = 1e-6

kernel_name = "hybrid_mlstm_pool_natten_ec_moe_dit"


def rmsnorm(x, g):
    x32 = x.astype(jnp.float32)
    y = x32 * lax.rsqrt(jnp.mean(x32 * x32, axis=-1, keepdims=True) + EPS)
    return (y * g.astype(jnp.float32)).astype(x.dtype)


def modulate(h, shift, scale):
    return h * (1 + scale) + shift


def rope_1d(x, pos):
    half = x.shape[-1] // 2
    inv = ROPE_BASE ** (-jnp.arange(half, dtype=jnp.float32) / half)
    ang = pos.astype(jnp.float32)[:, None] * inv[None, :]
    cos = jnp.cos(ang).astype(x.dtype)
    sin = jnp.sin(ang).astype(x.dtype)
    x1, x2 = x[..., :half], x[..., half:]
    return jnp.concatenate([x1 * cos - x2 * sin, x1 * sin + x2 * cos], axis=-1)


def axial_rope(x, rows, cols):
    half = x.shape[-1] // 2
    return jnp.concatenate([rope_1d(x[..., :half], rows), rope_1d(x[..., half:], cols)], axis=-1)


def zero_state(batch):
    H, d = MLSTM_HEADS, MLSTM_HEAD_DIM
    return (jnp.zeros((batch, H, d, d), jnp.float32), jnp.zeros((batch, H, d), jnp.float32),
            jnp.zeros((batch, H), jnp.float32))


def mlstm_chunked(q, k, v, li, lf, state0):
    f32 = jnp.float32
    B, H, N, dk = q.shape
    dv = v.shape[-1]
    L = min(MLSTM_CHUNK, N)
    NC = N // L
    q = q.astype(f32).reshape(B, H, NC, L, dk)
    k = k.astype(f32).reshape(B, H, NC, L, dk)
    v = v.astype(f32).reshape(B, H, NC, L, dv)
    li = li.astype(f32).reshape(B, H, NC, L)
    lf = lf.astype(f32).reshape(B, H, NC, L)
    b = jnp.cumsum(lf, axis=-1)
    g = b[..., -1]
    a = g[..., None] - b + li
    m_loc = jnp.max(a, axis=-1)
    wa = jnp.exp(a - m_loc[..., None])
    C_loc = jnp.einsum('bhcl,bhclv,bhclk->bhcvk', wa, v, k)
    n_loc = jnp.einsum('bhcl,bhclk->bhck', wa, k)

    def step(carry, inp):
        C, n, m = carry
        Cl, nl, ml, gc = inp
        m_new = jnp.maximum(gc + m, ml)
        sp = jnp.exp(gc + m - m_new)
        sl = jnp.exp(ml - m_new)
        C_new = sp[..., None, None] * C + sl[..., None, None] * Cl
        n_new = sp[..., None] * n + sl[..., None] * nl
        return (C_new, n_new, m_new), (C, n, m)

    xs = tuple(jnp.moveaxis(t, 2, 0) for t in (C_loc, n_loc, m_loc, g))
    final, starts = lax.scan(step, state0, xs)
    Cs, ns, ms = (jnp.moveaxis(t, 0, 2) for t in starts)
    lower = np.tril(np.ones((L, L), dtype=bool))
    Dm = jnp.where(lower, b[..., :, None] - b[..., None, :] + li[..., None, :], -jnp.inf)
    inter = b + ms[..., None]
    m_j = jnp.maximum(inter, jnp.max(Dm, axis=-1))
    S = jnp.einsum('bhcjd,bhcsd->bhcjs', q, k) * jnp.exp(Dm - m_j[..., None])
    w_int = jnp.exp(inter - m_j)
    num = jnp.einsum('bhcjs,bhcsv->bhcjv', S, v) + w_int[..., None] * jnp.einsum('bhcjk,bhcvk->bhcjv', q, Cs)
    den = jnp.sum(S, axis=-1) + w_int * jnp.einsum('bhcjk,bhck->bhcj', q, ns)
    h = num / jnp.maximum(jnp.abs(den), jnp.exp(-m_j))[..., None]
    return h.reshape(B, H, N, dv), final


def mlstm_bidir(q, k, v, gates, state_fwd, state_bwd):
    flip = lambda t: jnp.flip(t, axis=2)
    h_f, s_f = mlstm_chunked(q, k, v, gates[0], jax.nn.log_sigmoid(gates[1]), state_fwd)
    h_b, s_b = mlstm_chunked(flip(q), flip(k), flip(v), flip(gates[2]), flip(jax.nn.log_sigmoid(gates[3])), state_bwd)
    return h_f + flip(h_b), s_f, s_b


def multiscale_pool(u, pool_w, pool_scale):
    B, N, _ = u.shape
    u32 = u.astype(jnp.float32)
    cs = jnp.concatenate([jnp.zeros((B, 1, POOL_WIDTH), jnp.float32), jnp.cumsum(u32, axis=1)], axis=1)
    t = np.arange(N)
    diffs = []
    for gi, w in enumerate(POOL_WINDOWS):
        lo = np.clip(t - w // 2, 0, N)
        hi = np.clip(t + w // 2, 0, N)
        sl = slice(gi * POOL_GROUP_DIM, (gi + 1) * POOL_GROUP_DIM)
        csg = cs[..., sl]
        cnt = jnp.asarray((hi - lo)[None, :, None], jnp.float32)
        diffs.append((csg[:, hi] - csg[:, lo]) / cnt - u32[..., sl])
    d = jnp.stack(diffs, axis=2).astype(u.dtype)
    y = jnp.einsum('bngc,gce->bnge', d, pool_w).reshape(B, N, POOL_WIDTH)
    return y * pool_scale


def mlstm_pool_mixer(hc, hl, w_in, gate_b, head_g, pool_w, pool_scale, w_out, rows, cols, ctx_out):
    H, dh = MLSTM_HEADS, MLSTM_HEAD_DIM
    cuts = [MLSTM_WIDTH, 2 * MLSTM_WIDTH, 3 * MLSTM_WIDTH, 4 * MLSTM_WIDTH, 4 * MLSTM_WIDTH + 4 * H]

    def project(h):
        B, N, _ = h.shape
        q, k, v, o, gts, u = jnp.split(h @ w_in, cuts, axis=-1)
        heads = lambda t: t.reshape(B, N, H, dh).transpose(0, 2, 1, 3)
        gts = (gts.astype(jnp.float32).reshape(B, N, 4, H) + gate_b.astype(jnp.float32)).transpose(2, 0, 3, 1)
        return heads(q), heads(k) * (dh ** -0.5), heads(v), o, gts, u

    def finish(h, o, u):
        B, _, N, _ = h.shape
        gn = head_g.astype(jnp.float32).reshape(H, dh)[None, :, None, :]
        hn = h * lax.rsqrt(jnp.mean(h * h, axis=-1, keepdims=True) + EPS) * gn
        hn = hn.transpose(0, 2, 1, 3).reshape(B, N, MLSTM_WIDTH).astype(o.dtype) * jax.nn.sigmoid(o)
        return jnp.concatenate([hn, multiscale_pool(u, pool_w, pool_scale)], axis=-1) @ w_out

    qc, kc, vc, oc, gc, uc = project(hc)
    ql, kl, vl, ol, gl, ul = project(hl)
    ql = axial_rope(ql, rows, cols)
    kl = axial_rope(kl, rows, cols)
    z = zero_state(hc.shape[0])
    h_c, s_f, s_b = mlstm_bidir(qc, kc, vc, gc, z, z)
    h_l, _, _ = mlstm_bidir(ql, kl, vl, gl, s_f, s_b)
    out_c = finish(h_c, oc, uc) if ctx_out else None
    return out_c, finish(h_l, ol, ul)


def na_indices(n):
    R = n // GRID_W
    WR = min(NA_ROWS, R)
    WC = NA_COLS
    r = np.arange(R)
    c = np.arange(GRID_W)
    rs = np.clip(r - WR // 2, 0, R - WR)
    cs = np.clip(c - WC // 2, 0, GRID_W - WC)
    kr = rs[:, None] + np.arange(WR)[None, :]
    kc = cs[:, None] + np.arange(WC)[None, :]
    idx = kr[:, None, :, None] * GRID_W + kc[None, :, None, :]
    dr = kr - r[:, None]
    dc = kc - c[:, None]
    bidx = (dr[:, None, :, None] + NA_ROWS - 1) * RPB_C + (dc[None, :, None, :] + NA_COLS - 1)
    return (jnp.asarray(idx.reshape(R, GRID_W, WR * WC), jnp.int32),
            jnp.asarray(bidx.reshape(R, GRID_W, WR * WC), jnp.int32))


def na_mixer(hc, hl, w_in, rpb, w_out, ctx_out):
    H, dh = NA_HEADS, NA_HEAD_DIM
    scale = dh ** -0.5

    def qkv(h):
        B, N, _ = h.shape
        return [t.reshape(B, N, H, dh).transpose(0, 2, 1, 3) for t in jnp.split(h @ w_in, 3, axis=-1)]

    def merge(o):
        B, _, N, _ = o.shape
        return o.transpose(0, 2, 1, 3).reshape(B, N, NA_WIDTH) @ w_out

    qc, kc, vc = qkv(hc)
    ql, kl, vl = qkv(hl)
    out_c = None
    if ctx_out:
        pc = jax.nn.softmax(jnp.einsum('bhqd,bhkd->bhqk', qc * scale, kc).astype(jnp.float32), axis=-1)
        out_c = merge(jnp.einsum('bhqk,bhkd->bhqd', pc.astype(vc.dtype), vc))
    B, _, N, _ = ql.shape
    R = N // GRID_W
    idx, bidx = na_indices(N)
    rpb_flat = rpb.reshape(H, RPB_R * RPB_C)
    qrows = ql.reshape(B, H, R, GRID_W, dh).transpose(2, 0, 1, 3, 4)

    def row_block(args):
        qb, ib, bb = args
        kn = kl[:, :, ib]
        vn = vl[:, :, ib]
        s_nb = jnp.einsum('bhqd,bhqkd->bhqk', qb * scale, kn).astype(jnp.float32) + rpb_flat[:, bb].astype(jnp.float32)
        s_cx = jnp.einsum('bhqd,bhkd->bhqk', qb * scale, kc).astype(jnp.float32)
        p = jax.nn.softmax(jnp.concatenate([s_nb, s_cx], axis=-1), axis=-1).astype(vl.dtype)
        K = ib.shape[-1]
        return (jnp.einsum('bhqk,bhqkd->bhqd', p[..., :K], vn)
                + jnp.einsum('bhqk,bhkd->bhqd', p[..., K:], vc))

    o = lax.map(row_block, (qrows, idx, bidx))
    ol = o.transpose(1, 2, 0, 3, 4).reshape(B, H, N, dh)
    return out_c, merge(ol)


def expert_choice_ffn(h, w_router, w_gate, w_up, w_down):
    B, N, D = h.shape
    cap = CAPACITY_FACTOR * N // N_EXPERTS
    aff = jax.nn.softmax((h @ w_router).astype(jnp.float32), axis=-1)
    gate, idx = lax.top_k(jnp.swapaxes(aff, 1, 2), cap)
    xin = jax.vmap(lambda hb, ib: hb[ib])(h, idx)
    a = jnp.einsum('becd,edf->becf', xin, w_gate)
    u = jnp.einsum('becd,edf->becf', xin, w_up)
    y = jnp.einsum('becf,efd->becd', jax.nn.silu(a) * u, w_down) * gate[..., None].astype(h.dtype)
    bi = jnp.arange(B)[:, None, None]
    return jnp.zeros_like(h).at[bi, idx].add(y)


def setup_inputs(seed: int = 0) -> dict:
    key = jax.random.key(seed)
    ks = jax.random.split(key, 24)
    D = D_MODEL
    n_even = (DEPTH + 1) // 2
    n_odd = DEPTH // 2
    nrm = lambda k, shape, std: jax.random.normal(k, shape, jnp.float32) * std
    fb = jnp.linspace(FORGET_BIAS_LO, FORGET_BIAS_HI, MLSTM_HEADS, dtype=jnp.float32)
    zb = jnp.zeros((MLSTM_HEADS,), jnp.float32)
    gate_base = jnp.stack([zb, fb, zb, fb])
    return {
        "x": nrm(ks[0], (BATCH, SEQ, D), 1.0),
        "c": nrm(ks[1], (BATCH, D), 1.0),
        "ctx": nrm(ks[2], (BATCH, CTX_LEN, D), 1.0),
        "c_ctx": nrm(ks[3], (D,), 1.0),
        "ada_w": nrm(ks[4], (DEPTH, D, 6 * D), 0.5 * D ** -0.5),
        "ada_b": nrm(ks[5], (DEPTH, 6 * D), 0.02),
        "norm_g": 1.0 + nrm(ks[6], (DEPTH, 2, D), 0.02),
        "final_g": 1.0 + nrm(ks[7], (D,), 0.02),
        "ev_w_in": nrm(ks[8], (n_even, D, EVEN_IN), D ** -0.5),
        "ev_gate_b": gate_base[None] + nrm(ks[9], (n_even, 4, MLSTM_HEADS), 0.1),
        "ev_head_g": 1.0 + nrm(ks[10], (n_even, MLSTM_WIDTH), 0.02),
        "ev_pool_w": nrm(ks[11], (n_even, POOL_GROUPS, POOL_GROUP_DIM, POOL_GROUP_DIM), POOL_GROUP_DIM ** -0.5),
        "ev_pool_scale": 1.0 + nrm(ks[12], (n_even, POOL_WIDTH), 0.1),
        "ev_w_out": nrm(ks[13], (n_even, EVEN_MIX, D), EVEN_MIX ** -0.5),
        "na_w_in": nrm(ks[14], (n_odd, D, 3 * NA_WIDTH), D ** -0.5),
        "na_rpb": nrm(ks[15], (n_odd, NA_HEADS, RPB_R, RPB_C), 0.1),
        "na_w_out": nrm(ks[16], (n_odd, NA_WIDTH, D), NA_WIDTH ** -0.5),
        "moe_w_router": nrm(ks[17], (DEPTH, D, N_EXPERTS), D ** -0.5),
        "moe_w_gate": nrm(ks[18], (DEPTH, N_EXPERTS, D, EXPERT_FF), D ** -0.5),
        "moe_w_up": nrm(ks[19], (DEPTH, N_EXPERTS, D, EXPERT_FF), D ** -0.5),
        "moe_w_down": nrm(ks[20], (DEPTH, N_EXPERTS, EXPERT_FF, D), EXPERT_FF ** -0.5),
    }


def reference(x, c, ctx, c_ctx, ada_w, ada_b, norm_g, final_g, ev_w_in, ev_gate_b, ev_head_g, ev_pool_w,
              ev_pool_scale, ev_w_out, na_w_in, na_rpb, na_w_out, moe_w_router, moe_w_gate, moe_w_up, moe_w_down):
    B, N, D = x.shape
    t = jnp.arange(N, dtype=jnp.int32)
    rows = t // GRID_W
    cols = t % GRID_W
    xl, xc = x, ctx
    for i in range(DEPTH):
        last = i == DEPTH - 1
        j = i // 2
        ml = (jax.nn.silu(c) @ ada_w[i] + ada_b[i])[:, None, :]
        mc = jax.nn.silu(c_ctx) @ ada_w[i] + ada_b[i]
        sh_l, sc_l, ga_l, shf_l, scf_l, gaf_l = jnp.split(ml, 6, axis=-1)
        sh_c, sc_c, ga_c, shf_c, scf_c, gaf_c = jnp.split(mc, 6, axis=-1)
        hl = modulate(rmsnorm(xl, norm_g[i, 0]), sh_l, sc_l)
        hc = modulate(rmsnorm(xc, norm_g[i, 0]), sh_c, sc_c)
        if i % 2 == 0:
            oc, ol = mlstm_pool_mixer(hc, hl, ev_w_in[j], ev_gate_b[j], ev_head_g[j], ev_pool_w[j],
                                      ev_pool_scale[j], ev_w_out[j], rows, cols, not last)
        else:
            oc, ol = na_mixer(hc, hl, na_w_in[j], na_rpb[j], na_w_out[j], not last)
        xl = xl + ga_l * ol
        hl = modulate(rmsnorm(xl, norm_g[i, 1]), shf_l, scf_l)
        xl = xl + gaf_l * expert_choice_ffn(hl, moe_w_router[i], moe_w_gate[i], moe_w_up[i], moe_w_down[i])
        if not last:
            xc = xc + ga_c * oc
            hc = modulate(rmsnorm(xc, norm_g[i, 1]), shf_c, scf_c)
            xc = xc + gaf_c * expert_choice_ffn(hc, moe_w_router[i], moe_w_gate[i], moe_w_up[i], moe_w_down[i])
    return rmsnorm(xl, final_g)
```

```python
import functools

import jax
import jax.numpy as jnp
import numpy as np
from jax import lax
from jax.experimental import pallas as pl
from jax.experimental.pallas import tpu as pltpu

D_MODEL = 2048
DEPTH = 4
GRID_W = 64
MLSTM_WIDTH = D_MODEL // 2
MLSTM_HEADS = 8
MLSTM_HEAD_DIM = MLSTM_WIDTH // MLSTM_HEADS
MLSTM_CHUNK = 128
POOL_WIDTH = D_MODEL // 2
POOL_WINDOWS = (2, 4, 8, 16)
POOL_GROUPS = len(POOL_WINDOWS)
POOL_GROUP_DIM = POOL_WIDTH // POOL_GROUPS
NA_HEADS = 16
NA_HEAD_DIM = D_MODEL // NA_HEADS
NA_WIDTH = NA_HEADS * NA_HEAD_DIM
NA_ROWS = 8
NA_COLS = 16
RPB_R = 2 * NA_ROWS - 1
RPB_C = 2 * NA_COLS - 1
N_EXPERTS = 16
CAPACITY_FACTOR = 2
ROPE_BASE = 10000.0
EPS = 1e-6


def _mm_kernel(a_ref, w_ref, o_ref):
    a = a_ref[...].astype(jnp.bfloat16)
    w = w_ref[...].astype(jnp.bfloat16)
    o_ref[...] = jnp.dot(a, w, preferred_element_type=jnp.float32)


def matmul(a, w, tm=512, tn=512):
    M, K = a.shape
    N = w.shape[1]
    tm = min(tm, M)
    assert M % tm == 0 and N % tn == 0
    return pl.pallas_call(
        _mm_kernel,
        grid=(N // tn, M // tm),
        in_specs=[pl.BlockSpec((tm, K), lambda j, i: (i, 0)),
                  pl.BlockSpec((K, tn), lambda j, i: (0, j))],
        out_specs=pl.BlockSpec((tm, tn), lambda j, i: (i, j)),
        out_shape=jax.ShapeDtypeStruct((M, N), jnp.float32),
        compiler_params=pltpu.CompilerParams(vmem_limit_bytes=48 * 1024 * 1024),
        name="matmul",
    )(a, w)


def proj(h, w):
    lead = h.shape[:-1]
    return matmul(h.reshape(-1, h.shape[-1]), w).reshape(*lead, w.shape[1])


def rmsnorm(x, g):
    x32 = x.astype(jnp.float32)
    y = x32 * lax.rsqrt(jnp.mean(x32 * x32, axis=-1, keepdims=True) + EPS)
    return (y * g.astype(jnp.float32)).astype(x.dtype)


def modulate(h, shift, scale):
    return h * (1 + scale) + shift


def rope_1d(x, pos):
    half = x.shape[-1] // 2
    inv = ROPE_BASE ** (-jnp.arange(half, dtype=jnp.float32) / half)
    ang = pos.astype(jnp.float32)[:, None] * inv[None, :]
    cos = jnp.cos(ang).astype(x.dtype)
    sin = jnp.sin(ang).astype(x.dtype)
    x1, x2 = x[..., :half], x[..., half:]
    return jnp.concatenate([x1 * cos - x2 * sin, x1 * sin + x2 * cos], axis=-1)


def axial_rope(x, rows, cols):
    half = x.shape[-1] // 2
    return jnp.concatenate([rope_1d(x[..., :half], rows), rope_1d(x[..., half:], cols)], axis=-1)


def zero_state(batch):
    H, d = MLSTM_HEADS, MLSTM_HEAD_DIM
    return (jnp.zeros((batch, H, d, d), jnp.float32), jnp.zeros((batch, H, d), jnp.float32),
            jnp.zeros((batch, H), jnp.float32))


def mlstm_chunked(q, k, v, li, lf, state0):
    f32 = jnp.float32
    B, H, N, dk = q.shape
    dv = v.shape[-1]
    L = min(MLSTM_CHUNK, N)
    NC = N // L
    q = q.astype(f32).reshape(B, H, NC, L, dk)
    k = k.astype(f32).reshape(B, H, NC, L, dk)
    v = v.astype(f32).reshape(B, H, NC, L, dv)
    li = li.astype(f32).reshape(B, H, NC, L)
    lf = lf.astype(f32).reshape(B, H, NC, L)
    b = jnp.cumsum(lf, axis=-1)
    g = b[..., -1]
    a = g[..., None] - b + li
    m_loc = jnp.max(a, axis=-1)
    wa = jnp.exp(a - m_loc[..., None])
    C_loc = jnp.einsum('bhcl,bhclv,bhclk->bhcvk', wa, v, k)
    n_loc = jnp.einsum('bhcl,bhclk->bhck', wa, k)

    def step(carry, inp):
        C, n, m = carry
        Cl, nl, ml, gc = inp
        m_new = jnp.maximum(gc + m, ml)
        sp = jnp.exp(gc + m - m_new)
        sl = jnp.exp(ml - m_new)
        C_new = sp[..., None, None] * C + sl[..., None, None] * Cl
        n_new = sp[..., None] * n + sl[..., None] * nl
        return (C_new, n_new, m_new), (C, n, m)

    xs = tuple(jnp.moveaxis(t, 2, 0) for t in (C_loc, n_loc, m_loc, g))
    final, starts = lax.scan(step, state0, xs)
    Cs, ns, ms = (jnp.moveaxis(t, 0, 2) for t in starts)
    lower = np.tril(np.ones((L, L), dtype=bool))
    Dm = jnp.where(lower, b[..., :, None] - b[..., None, :] + li[..., None, :], -jnp.inf)
    inter = b + ms[..., None]
    m_j = jnp.maximum(inter, jnp.max(Dm, axis=-1))
    S = jnp.einsum('bhcjd,bhcsd->bhcjs', q, k) * jnp.exp(Dm - m_j[..., None])
    w_int = jnp.exp(inter - m_j)
    num = jnp.einsum('bhcjs,bhcsv->bhcjv', S, v) + w_int[..., None] * jnp.einsum('bhcjk,bhcvk->bhcjv', q, Cs)
    den = jnp.sum(S, axis=-1) + w_int * jnp.einsum('bhcjk,bhck->bhcj', q, ns)
    h = num / jnp.maximum(jnp.abs(den), jnp.exp(-m_j))[..., None]
    return h.reshape(B, H, N, dv), final


def mlstm_bidir(q, k, v, gates, state_fwd, state_bwd):
    flip = lambda t: jnp.flip(t, axis=2)
    h_f, s_f = mlstm_chunked(q, k, v, gates[0], jax.nn.log_sigmoid(gates[1]), state_fwd)
    h_b, s_b = mlstm_chunked(flip(q), flip(k), flip(v), flip(gates[2]), flip(jax.nn.log_sigmoid(gates[3])), state_bwd)
    return h_f + flip(h_b), s_f, s_b


def multiscale_pool(u, pool_w, pool_scale):
    B, N, _ = u.shape
    u32 = u.astype(jnp.float32)
    cs = jnp.concatenate([jnp.zeros((B, 1, POOL_WIDTH), jnp.float32), jnp.cumsum(u32, axis=1)], axis=1)
    t = np.arange(N)
    diffs = []
    for gi, w in enumerate(POOL_WINDOWS):
        lo = np.clip(t - w // 2, 0, N)
        hi = np.clip(t + w // 2, 0, N)
        sl = slice(gi * POOL_GROUP_DIM, (gi + 1) * POOL_GROUP_DIM)
        csg = cs[..., sl]
        cnt = jnp.asarray((hi - lo)[None, :, None], jnp.float32)
        diffs.append((csg[:, hi] - csg[:, lo]) / cnt - u32[..., sl])
    d = jnp.stack(diffs, axis=2).astype(u.dtype)
    y = jnp.einsum('bngc,gce->bnge', d, pool_w).reshape(B, N, POOL_WIDTH)
    return y * pool_scale


def mlstm_pool_mixer(hc, hl, w_in, gate_b, head_g, pool_w, pool_scale, w_out, rows, cols, ctx_out):
    H, dh = MLSTM_HEADS, MLSTM_HEAD_DIM
    cuts = [MLSTM_WIDTH, 2 * MLSTM_WIDTH, 3 * MLSTM_WIDTH, 4 * MLSTM_WIDTH, 4 * MLSTM_WIDTH + 4 * H]

    def project(h):
        B, N, _ = h.shape
        q, k, v, o, gts, u = jnp.split(h @ w_in, cuts, axis=-1)
        heads = lambda t: t.reshape(B, N, H, dh).transpose(0, 2, 1, 3)
        gts = (gts.astype(jnp.float32).reshape(B, N, 4, H) + gate_b.astype(jnp.float32)).transpose(2, 0, 3, 1)
        return heads(q), heads(k) * (dh ** -0.5), heads(v), o, gts, u

    def finish(h, o, u):
        B, _, N, _ = h.shape
        gn = head_g.astype(jnp.float32).reshape(H, dh)[None, :, None, :]
        hn = h * lax.rsqrt(jnp.mean(h * h, axis=-1, keepdims=True) + EPS) * gn
        hn = hn.transpose(0, 2, 1, 3).reshape(B, N, MLSTM_WIDTH).astype(o.dtype) * jax.nn.sigmoid(o)
        return proj(jnp.concatenate([hn, multiscale_pool(u, pool_w, pool_scale)], axis=-1), w_out)

    qc, kc, vc, oc, gc, uc = project(hc)
    ql, kl, vl, ol, gl, ul = project(hl)
    ql = axial_rope(ql, rows, cols)
    kl = axial_rope(kl, rows, cols)
    z = zero_state(hc.shape[0])
    h_c, s_f, s_b = mlstm_bidir(qc, kc, vc, gc, z, z)
    h_l, _, _ = mlstm_bidir(ql, kl, vl, gl, s_f, s_b)
    out_c = finish(h_c, oc, uc) if ctx_out else None
    return out_c, finish(h_l, ol, ul)


def na_indices(n):
    R = n // GRID_W
    WR = min(NA_ROWS, R)
    WC = NA_COLS
    r = np.arange(R)
    c = np.arange(GRID_W)
    rs = np.clip(r - WR // 2, 0, R - WR)
    cs = np.clip(c - WC // 2, 0, GRID_W - WC)
    kr = rs[:, None] + np.arange(WR)[None, :]
    kc = cs[:, None] + np.arange(WC)[None, :]
    idx = kr[:, None, :, None] * GRID_W + kc[None, :, None, :]
    dr = kr - r[:, None]
    dc = kc - c[:, None]
    bidx = (dr[:, None, :, None] + NA_ROWS - 1) * RPB_C + (dc[None, :, None, :] + NA_COLS - 1)
    return (jnp.asarray(idx.reshape(R, GRID_W, WR * WC), jnp.int32),
            jnp.asarray(bidx.reshape(R, GRID_W, WR * WC), jnp.int32))


def na_mixer(hc, hl, w_in, rpb, w_out, ctx_out):
    H, dh = NA_HEADS, NA_HEAD_DIM
    scale = dh ** -0.5

    def qkv(h):
        B, N, _ = h.shape
        return [t.reshape(B, N, H, dh).transpose(0, 2, 1, 3) for t in jnp.split(proj(h, w_in), 3, axis=-1)]

    def merge(o):
        B, _, N, _ = o.shape
        return proj(o.transpose(0, 2, 1, 3).reshape(B, N, NA_WIDTH), w_out)

    qc, kc, vc = qkv(hc)
    ql, kl, vl = qkv(hl)
    out_c = None
    if ctx_out:
        pc = jax.nn.softmax(jnp.einsum('bhqd,bhkd->bhqk', qc * scale, kc).astype(jnp.float32), axis=-1)
        out_c = merge(jnp.einsum('bhqk,bhkd->bhqd', pc.astype(vc.dtype), vc))
    B, _, N, _ = ql.shape
    R = N // GRID_W
    idx, bidx = na_indices(N)
    rpb_flat = rpb.reshape(H, RPB_R * RPB_C)
    qrows = ql.reshape(B, H, R, GRID_W, dh).transpose(2, 0, 1, 3, 4)

    def row_block(args):
        qb, ib, bb = args
        kn = kl[:, :, ib]
        vn = vl[:, :, ib]
        s_nb = jnp.einsum('bhqd,bhqkd->bhqk', qb * scale, kn).astype(jnp.float32) + rpb_flat[:, bb].astype(jnp.float32)
        s_cx = jnp.einsum('bhqd,bhkd->bhqk', qb * scale, kc).astype(jnp.float32)
        p = jax.nn.softmax(jnp.concatenate([s_nb, s_cx], axis=-1), axis=-1).astype(vl.dtype)
        K = ib.shape[-1]
        return (jnp.einsum('bhqk,bhqkd->bhqd', p[..., :K], vn)
                + jnp.einsum('bhqk,bhkd->bhqd', p[..., K:], vc))

    o = lax.map(row_block, (qrows, idx, bidx))
    ol = o.transpose(1, 2, 0, 3, 4).reshape(B, H, N, dh)
    return out_c, merge(ol)


def expert_choice_ffn(h, w_router, w_gate, w_up, w_down):
    B, N, D = h.shape
    cap = CAPACITY_FACTOR * N // N_EXPERTS
    aff = jax.nn.softmax((h @ w_router).astype(jnp.float32), axis=-1)
    gate, idx = lax.top_k(jnp.swapaxes(aff, 1, 2), cap)
    xin = jax.vmap(lambda hb, ib: hb[ib])(h, idx)
    a = jnp.einsum('becd,edf->becf', xin, w_gate)
    u = jnp.einsum('becd,edf->becf', xin, w_up)
    y = jnp.einsum('becf,efd->becd', jax.nn.silu(a) * u, w_down) * gate[..., None].astype(h.dtype)
    bi = jnp.arange(B)[:, None, None]
    return jnp.zeros_like(h).at[bi, idx].add(y)


def kernel(x, c, ctx, c_ctx, ada_w, ada_b, norm_g, final_g, ev_w_in, ev_gate_b, ev_head_g, ev_pool_w,
           ev_pool_scale, ev_w_out, na_w_in, na_rpb, na_w_out, moe_w_router, moe_w_gate, moe_w_up, moe_w_down):
    B, N, D = x.shape
    t = jnp.arange(N, dtype=jnp.int32)
    rows = t // GRID_W
    cols = t % GRID_W
    xl, xc = x, ctx
    for i in range(DEPTH):
        last = i == DEPTH - 1
        j = i // 2
        ml = (jax.nn.silu(c) @ ada_w[i] + ada_b[i])[:, None, :]
        mc = jax.nn.silu(c_ctx) @ ada_w[i] + ada_b[i]
        sh_l, sc_l, ga_l, shf_l, scf_l, gaf_l = jnp.split(ml, 6, axis=-1)
        sh_c, sc_c, ga_c, shf_c, scf_c, gaf_c = jnp.split(mc, 6, axis=-1)
        hl = modulate(rmsnorm(xl, norm_g[i, 0]), sh_l, sc_l)
        hc = modulate(rmsnorm(xc, norm_g[i, 0]), sh_c, sc_c)
        if i % 2 == 0:
            oc, ol = mlstm_pool_mixer(hc, hl, ev_w_in[j], ev_gate_b[j], ev_head_g[j], ev_pool_w[j],
                                      ev_pool_scale[j], ev_w_out[j], rows, cols, not last)
        else:
            oc, ol = na_mixer(hc, hl, na_w_in[j], na_rpb[j], na_w_out[j], not last)
        xl = xl + ga_l * ol
        hl = modulate(rmsnorm(xl, norm_g[i, 1]), shf_l, scf_l)
        xl = xl + gaf_l * expert_choice_ffn(hl, moe_w_router[i], moe_w_gate[i], moe_w_up[i], moe_w_down[i])
        if not last:
            xc = xc + ga_c * oc
            hc = modulate(rmsnorm(xc, norm_g[i, 1]), shf_c, scf_c)
            xc = xc + gaf_c * expert_choice_ffn(hc, moe_w_router[i], moe_w_gate[i], moe_w_up[i], moe_w_down[i])
    return rmsnorm(xl, final_g)
```

```python
import functools

import jax
import jax.numpy as jnp
import numpy as np
from jax import lax
from jax.experimental import pallas as pl
from jax.experimental.pallas import tpu as pltpu

D_MODEL = 2048
DEPTH = 4
GRID_W = 64
MLSTM_WIDTH = D_MODEL // 2
MLSTM_HEADS = 8
MLSTM_HEAD_DIM = MLSTM_WIDTH // MLSTM_HEADS
MLSTM_CHUNK = 128
POOL_WIDTH = D_MODEL // 2
POOL_WINDOWS = (2, 4, 8, 16)
POOL_GROUPS = len(POOL_WINDOWS)
POOL_GROUP_DIM = POOL_WIDTH // POOL_GROUPS
NA_HEADS = 16
NA_HEAD_DIM = D_MODEL // NA_HEADS
NA_WIDTH = NA_HEADS * NA_HEAD_DIM
NA_ROWS = 8
NA_COLS = 16
RPB_R = 2 * NA_ROWS - 1
RPB_C = 2 * NA_COLS - 1
N_EXPERTS = 16
CAPACITY_FACTOR = 2
ROPE_BASE = 10000.0
EPS = 1e-6


def _mm_kernel(a_ref, w_ref, o_ref):
    a = a_ref[...].astype(jnp.bfloat16)
    w = w_ref[...].astype(jnp.bfloat16)
    o_ref[...] = jnp.dot(a, w, preferred_element_type=jnp.float32).astype(o_ref.dtype)


def matmul(a, w, out_dtype=jnp.float32, tm=512, tn=512):
    M, K = a.shape
    N = w.shape[1]
    tm = min(tm, M)
    assert M % tm == 0 and N % tn == 0
    return pl.pallas_call(
        _mm_kernel,
        grid=(N // tn, M // tm),
        in_specs=[pl.BlockSpec((tm, K), lambda j, i: (i, 0)),
                  pl.BlockSpec((K, tn), lambda j, i: (0, j))],
        out_specs=pl.BlockSpec((tm, tn), lambda j, i: (i, j)),
        out_shape=jax.ShapeDtypeStruct((M, N), out_dtype),
        compiler_params=pltpu.CompilerParams(vmem_limit_bytes=48 * 1024 * 1024),
        name="matmul",
    )(a, w)


def proj(h, w, out_dtype=jnp.float32):
    lead = h.shape[:-1]
    return matmul(h.reshape(-1, h.shape[-1]), w, out_dtype).reshape(*lead, w.shape[1])


NA_QROWS = 4
NA_KROWS = 12
NA_MASKED = -1e30


def na_bias_tables(rpb, R):
    H = rpb.shape[0]
    nblk = R // NA_QROWS
    r0 = np.arange(nblk) * NA_QROWS
    ws = np.clip(r0 - NA_ROWS // 2, 0, R - NA_KROWS)
    types, tmap = np.unique(ws - r0, return_inverse=True)
    qi = np.arange(NA_QROWS)[:, None, None, None]
    c = np.arange(GRID_W)[None, :, None, None]
    kj = np.arange(NA_KROWS)[None, None, :, None]
    kc = np.arange(GRID_W)[None, None, None, :]
    tabs = []
    for ti in range(len(types)):
        blk = int(np.nonzero(tmap == ti)[0][0])
        r = r0[blk] + qi
        kr = ws[blk] + kj
        rs = np.clip(r - NA_ROWS // 2, 0, R - NA_ROWS)
        cs = np.clip(c - NA_COLS // 2, 0, GRID_W - NA_COLS)
        valid = (kr >= rs) & (kr < rs + NA_ROWS) & (kc >= cs) & (kc < cs + NA_COLS)
        dr = np.clip(kr - r + NA_ROWS - 1, 0, RPB_R - 1)
        dc = np.clip(kc - c + NA_COLS - 1, 0, RPB_C - 1)
        shape2d = (NA_QROWS * GRID_W, NA_KROWS * GRID_W)
        bidx = np.broadcast_to(dr * RPB_C + dc, valid.shape).reshape(shape2d)
        b = rpb.reshape(H, RPB_R * RPB_C)[:, bidx]
        tabs.append(jnp.where(valid.reshape(shape2d)[None], b, NA_MASKED))
    return jnp.stack(tabs, axis=1).astype(jnp.float32), tmap.astype(np.int32), ws.astype(np.int32)


def _na_kernel(tmap_ref, ws_ref, q_ref, k_ref, v_ref, kc_ref, vc_ref, bias_ref, o_ref, *, scale):
    rb = pl.program_id(2)
    start = pl.multiple_of(ws_ref[rb] * GRID_W, GRID_W)
    q = q_ref[0]
    kw = k_ref[0, pl.ds(start, NA_KROWS * GRID_W), :]
    vw = v_ref[0, pl.ds(start, NA_KROWS * GRID_W), :]
    dn = (((1,), (1,)), ((), ()))
    s_nb = lax.dot_general(q, kw, dn, preferred_element_type=jnp.float32) * scale + bias_ref[0, 0]
    s_cx = lax.dot_general(q, kc_ref[0], dn, preferred_element_type=jnp.float32) * scale
    m = jnp.maximum(jnp.max(s_nb, axis=-1, keepdims=True), jnp.max(s_cx, axis=-1, keepdims=True))
    p_nb = jnp.exp(s_nb - m)
    p_cx = jnp.exp(s_cx - m)
    l = jnp.sum(p_nb, axis=-1, keepdims=True) + jnp.sum(p_cx, axis=-1, keepdims=True)
    o = (jnp.dot(p_nb.astype(jnp.bfloat16), vw, preferred_element_type=jnp.float32)
         + jnp.dot(p_cx.astype(jnp.bfloat16), vc_ref[0], preferred_element_type=jnp.float32))
    o_ref[0] = (o / l).astype(o_ref.dtype)


def na_attention(qkv_l, qkv_c, rpb):
    B, N, _ = qkv_l.shape
    Lc = qkv_c.shape[1]
    H, dh = NA_HEADS, NA_HEAD_DIM
    R = N // GRID_W
    bias, tmap, ws = na_bias_tables(rpb, R)
    tq = NA_QROWS * GRID_W
    tk = NA_KROWS * GRID_W
    grid_spec = pltpu.PrefetchScalarGridSpec(
        num_scalar_prefetch=2,
        grid=(B, H, R // NA_QROWS),
        in_specs=[
            pl.BlockSpec((1, tq, dh), lambda b, h, r, tm, ws: (b, r, h)),
            pl.BlockSpec((1, N, dh), lambda b, h, r, tm, ws: (b, 0, H + h)),
            pl.BlockSpec((1, N, dh), lambda b, h, r, tm, ws: (b, 0, 2 * H + h)),
            pl.BlockSpec((1, Lc, dh), lambda b, h, r, tm, ws: (b, 0, H + h)),
            pl.BlockSpec((1, Lc, dh), lambda b, h, r, tm, ws: (b, 0, 2 * H + h)),
            pl.BlockSpec((1, 1, tq, tk), lambda b, h, r, tm, ws: (h, tm[r], 0, 0)),
        ],
        out_specs=pl.BlockSpec((1, tq, dh), lambda b, h, r, tm, ws: (b, r, h)),
    )
    return pl.pallas_call(
        functools.partial(_na_kernel, scale=dh ** -0.5),
        grid_spec=grid_spec,
        out_shape=jax.ShapeDtypeStruct((B, N, H * dh), jnp.bfloat16),
        compiler_params=pltpu.CompilerParams(vmem_limit_bytes=48 * 1024 * 1024),
        name="na_attention",
    )(jnp.asarray(tmap), jnp.asarray(ws), qkv_l, qkv_l, qkv_l, qkv_c, qkv_c, bias)


def rmsnorm(x, g):
    x32 = x.astype(jnp.float32)
    y = x32 * lax.rsqrt(jnp.mean(x32 * x32, axis=-1, keepdims=True) + EPS)
    return (y * g.astype(jnp.float32)).astype(x.dtype)


def modulate(h, shift, scale):
    return h * (1 + scale) + shift


def rope_1d(x, pos):
    half = x.shape[-1] // 2
    inv = ROPE_BASE ** (-jnp.arange(half, dtype=jnp.float32) / half)
    ang = pos.astype(jnp.float32)[:, None] * inv[None, :]
    cos = jnp.cos(ang).astype(x.dtype)
    sin = jnp.sin(ang).astype(x.dtype)
    x1, x2 = x[..., :half], x[..., half:]
    return jnp.concatenate([x1 * cos - x2 * sin, x1 * sin + x2 * cos], axis=-1)


def axial_rope(x, rows, cols):
    half = x.shape[-1] // 2
    return jnp.concatenate([rope_1d(x[..., :half], rows), rope_1d(x[..., half:], cols)], axis=-1)


def zero_state(batch):
    H, d = MLSTM_HEADS, MLSTM_HEAD_DIM
    return (jnp.zeros((batch, H, d, d), jnp.float32), jnp.zeros((batch, H, d), jnp.float32),
            jnp.zeros((batch, H), jnp.float32))


def mlstm_chunked(q, k, v, li, lf, state0):
    f32 = jnp.float32
    B, H, N, dk = q.shape
    dv = v.shape[-1]
    L = min(MLSTM_CHUNK, N)
    NC = N // L
    q = q.astype(f32).reshape(B, H, NC, L, dk)
    k = k.astype(f32).reshape(B, H, NC, L, dk)
    v = v.astype(f32).reshape(B, H, NC, L, dv)
    li = li.astype(f32).reshape(B, H, NC, L)
    lf = lf.astype(f32).reshape(B, H, NC, L)
    b = jnp.cumsum(lf, axis=-1)
    g = b[..., -1]
    a = g[..., None] - b + li
    m_loc = jnp.max(a, axis=-1)
    wa = jnp.exp(a - m_loc[..., None])
    C_loc = jnp.einsum('bhcl,bhclv,bhclk->bhcvk', wa, v, k)
    n_loc = jnp.einsum('bhcl,bhclk->bhck', wa, k)

    def step(carry, inp):
        C, n, m = carry
        Cl, nl, ml, gc = inp
        m_new = jnp.maximum(gc + m, ml)
        sp = jnp.exp(gc + m - m_new)
        sl = jnp.exp(ml - m_new)
        C_new = sp[..., None, None] * C + sl[..., None, None] * Cl
        n_new = sp[..., None] * n + sl[..., None] * nl
        return (C_new, n_new, m_new), (C, n, m)

    xs = tuple(jnp.moveaxis(t, 2, 0) for t in (C_loc, n_loc, m_loc, g))
    final, starts = lax.scan(step, state0, xs)
    Cs, ns, ms = (jnp.moveaxis(t, 0, 2) for t in starts)
    lower = np.tril(np.ones((L, L), dtype=bool))
    Dm = jnp.where(lower, b[..., :, None] - b[..., None, :] + li[..., None, :], -jnp.inf)
    inter = b + ms[..., None]
    m_j = jnp.maximum(inter, jnp.max(Dm, axis=-1))
    S = jnp.einsum('bhcjd,bhcsd->bhcjs', q, k) * jnp.exp(Dm - m_j[..., None])
    w_int = jnp.exp(inter - m_j)
    num = jnp.einsum('bhcjs,bhcsv->bhcjv', S, v) + w_int[..., None] * jnp.einsum('bhcjk,bhcvk->bhcjv', q, Cs)
    den = jnp.sum(S, axis=-1) + w_int * jnp.einsum('bhcjk,bhck->bhcj', q, ns)
    h = num / jnp.maximum(jnp.abs(den), jnp.exp(-m_j))[..., None]
    return h.reshape(B, H, N, dv), final


def mlstm_bidir(q, k, v, gates, state_fwd, state_bwd):
    flip = lambda t: jnp.flip(t, axis=2)
    h_f, s_f = mlstm_chunked(q, k, v, gates[0], jax.nn.log_sigmoid(gates[1]), state_fwd)
    h_b, s_b = mlstm_chunked(flip(q), flip(k), flip(v), flip(gates[2]), flip(jax.nn.log_sigmoid(gates[3])), state_bwd)
    return h_f + flip(h_b), s_f, s_b


def multiscale_pool(u, pool_w, pool_scale):
    B, N, _ = u.shape
    u32 = u.astype(jnp.float32)
    cs = jnp.concatenate([jnp.zeros((B, 1, POOL_WIDTH), jnp.float32), jnp.cumsum(u32, axis=1)], axis=1)
    t = np.arange(N)
    diffs = []
    for gi, w in enumerate(POOL_WINDOWS):
        lo = np.clip(t - w // 2, 0, N)
        hi = np.clip(t + w // 2, 0, N)
        sl = slice(gi * POOL_GROUP_DIM, (gi + 1) * POOL_GROUP_DIM)
        csg = cs[..., sl]
        cnt = jnp.asarray((hi - lo)[None, :, None], jnp.float32)
        diffs.append((csg[:, hi] - csg[:, lo]) / cnt - u32[..., sl])
    d = jnp.stack(diffs, axis=2).astype(u.dtype)
    y = jnp.einsum('bngc,gce->bnge', d, pool_w).reshape(B, N, POOL_WIDTH)
    return y * pool_scale


def mlstm_pool_mixer(hc, hl, w_in, gate_b, head_g, pool_w, pool_scale, w_out, rows, cols, ctx_out):
    H, dh = MLSTM_HEADS, MLSTM_HEAD_DIM
    cuts = [MLSTM_WIDTH, 2 * MLSTM_WIDTH, 3 * MLSTM_WIDTH, 4 * MLSTM_WIDTH, 4 * MLSTM_WIDTH + 4 * H]

    def project(h):
        B, N, _ = h.shape
        q, k, v, o, gts, u = jnp.split(h @ w_in, cuts, axis=-1)
        heads = lambda t: t.reshape(B, N, H, dh).transpose(0, 2, 1, 3)
        gts = (gts.astype(jnp.float32).reshape(B, N, 4, H) + gate_b.astype(jnp.float32)).transpose(2, 0, 3, 1)
        return heads(q), heads(k) * (dh ** -0.5), heads(v), o, gts, u

    def finish(h, o, u):
        B, _, N, _ = h.shape
        gn = head_g.astype(jnp.float32).reshape(H, dh)[None, :, None, :]
        hn = h * lax.rsqrt(jnp.mean(h * h, axis=-1, keepdims=True) + EPS) * gn
        hn = hn.transpose(0, 2, 1, 3).reshape(B, N, MLSTM_WIDTH).astype(o.dtype) * jax.nn.sigmoid(o)
        return proj(jnp.concatenate([hn, multiscale_pool(u, pool_w, pool_scale)], axis=-1), w_out)

    qc, kc, vc, oc, gc, uc = project(hc)
    ql, kl, vl, ol, gl, ul = project(hl)
    ql = axial_rope(ql, rows, cols)
    kl = axial_rope(kl, rows, cols)
    z = zero_state(hc.shape[0])
    h_c, s_f, s_b = mlstm_bidir(qc, kc, vc, gc, z, z)
    h_l, _, _ = mlstm_bidir(ql, kl, vl, gl, s_f, s_b)
    out_c = finish(h_c, oc, uc) if ctx_out else None
    return out_c, finish(h_l, ol, ul)


def na_mixer(hc, hl, w_in, rpb, w_out, ctx_out):
    H, dh = NA_HEADS, NA_HEAD_DIM
    scale = dh ** -0.5
    qkv_c = proj(hc, w_in, jnp.bfloat16)
    qkv_l = proj(hl, w_in, jnp.bfloat16)
    out_c = None
    if ctx_out:
        B, Lc, _ = qkv_c.shape
        qc, kc, vc = (t.reshape(B, Lc, H, dh).transpose(0, 2, 1, 3) for t in jnp.split(qkv_c, 3, axis=-1))
        s = jnp.einsum('bhqd,bhkd->bhqk', qc, kc, preferred_element_type=jnp.float32) * scale
        pc = jax.nn.softmax(s, axis=-1)
        oc = jnp.einsum('bhqk,bhkd->bhqd', pc.astype(vc.dtype), vc, preferred_element_type=jnp.float32)
        out_c = proj(oc.transpose(0, 2, 1, 3).reshape(B, Lc, NA_WIDTH), w_out)
    return out_c, proj(na_attention(qkv_l, qkv_c, rpb), w_out)


def expert_choice_ffn(h, w_router, w_gate, w_up, w_down):
    B, N, D = h.shape
    cap = CAPACITY_FACTOR * N // N_EXPERTS
    aff = jax.nn.softmax((h @ w_router).astype(jnp.float32), axis=-1)
    gate, idx = lax.top_k(jnp.swapaxes(aff, 1, 2), cap)
    xin = jax.vmap(lambda hb, ib: hb[ib])(h, idx)
    a = jnp.einsum('becd,edf->becf', xin, w_gate)
    u = jnp.einsum('becd,edf->becf', xin, w_up)
    y = jnp.einsum('becf,efd->becd', jax.nn.silu(a) * u, w_down) * gate[..., None].astype(h.dtype)
    bi = jnp.arange(B)[:, None, None]
    return jnp.zeros_like(h).at[bi, idx].add(y)


def kernel(x, c, ctx, c_ctx, ada_w, ada_b, norm_g, final_g, ev_w_in, ev_gate_b, ev_head_g, ev_pool_w,
           ev_pool_scale, ev_w_out, na_w_in, na_rpb, na_w_out, moe_w_router, moe_w_gate, moe_w_up, moe_w_down):
    B, N, D = x.shape
    t = jnp.arange(N, dtype=jnp.int32)
    rows = t // GRID_W
    cols = t % GRID_W
    xl, xc = x, ctx
    for i in range(DEPTH):
        last = i == DEPTH - 1
        j = i // 2
        ml = (jax.nn.silu(c) @ ada_w[i] + ada_b[i])[:, None, :]
        mc = jax.nn.silu(c_ctx) @ ada_w[i] + ada_b[i]
        sh_l, sc_l, ga_l, shf_l, scf_l, gaf_l = jnp.split(ml, 6, axis=-1)
        sh_c, sc_c, ga_c, shf_c, scf_c, gaf_c = jnp.split(mc, 6, axis=-1)
        hl = modulate(rmsnorm(xl, norm_g[i, 0]), sh_l, sc_l)
        hc = modulate(rmsnorm(xc, norm_g[i, 0]), sh_c, sc_c)
        if i % 2 == 0:
            oc, ol = mlstm_pool_mixer(hc, hl, ev_w_in[j], ev_gate_b[j], ev_head_g[j], ev_pool_w[j],
                                      ev_pool_scale[j], ev_w_out[j], rows, cols, not last)
        else:
            oc, ol = na_mixer(hc, hl, na_w_in[j], na_rpb[j], na_w_out[j], not last)
        xl = xl + ga_l * ol
        hl = modulate(rmsnorm(xl, norm_g[i, 1]), shf_l, scf_l)
        xl = xl + gaf_l * expert_choice_ffn(hl, moe_w_router[i], moe_w_gate[i], moe_w_up[i], moe_w_down[i])
        if not last:
            xc = xc + ga_c * oc
            hc = modulate(rmsnorm(xc, norm_g[i, 1]), shf_c, scf_c)
            xc = xc + gaf_c * expert_choice_ffn(hc, moe_w_router[i], moe_w_gate[i], moe_w_up[i], moe_w_down[i])
    return rmsnorm(xl, final_g)
```

```python
import functools

import jax
import jax.numpy as jnp
import numpy as np
from jax import lax
from jax.experimental import pallas as pl
from jax.experimental.pallas import tpu as pltpu

D_MODEL = 2048
DEPTH = 4
GRID_W = 64
MLSTM_WIDTH = D_MODEL // 2
MLSTM_HEADS = 8
MLSTM_HEAD_DIM = MLSTM_WIDTH // MLSTM_HEADS
MLSTM_CHUNK = 128
POOL_WIDTH = D_MODEL // 2
POOL_WINDOWS = (2, 4, 8, 16)
POOL_GROUPS = len(POOL_WINDOWS)
POOL_GROUP_DIM = POOL_WIDTH // POOL_GROUPS
NA_HEADS = 16
NA_HEAD_DIM = D_MODEL // NA_HEADS
NA_WIDTH = NA_HEADS * NA_HEAD_DIM
NA_ROWS = 8
NA_COLS = 16
RPB_R = 2 * NA_ROWS - 1
RPB_C = 2 * NA_COLS - 1
N_EXPERTS = 16
EXPERT_FF = D_MODEL // 2
CAPACITY_FACTOR = 2
ROPE_BASE = 10000.0
EPS = 1e-6
MASKED = -1e30
VMEM_LIMIT = 48 * 1024 * 1024


def _mm_kernel(a_ref, w_ref, o_ref):
    a = a_ref[...].astype(jnp.bfloat16)
    w = w_ref[...].astype(jnp.bfloat16)
    o_ref[...] = jnp.dot(a, w, preferred_element_type=jnp.float32).astype(o_ref.dtype)


def matmul(a, w, out_dtype=jnp.float32, tm=512, tn=512):
    M, K = a.shape
    N = w.shape[1]
    tn = min(tn, N)
    assert M % tm == 0 and N % tn == 0
    return pl.pallas_call(
        _mm_kernel,
        grid=(N // tn, M // tm),
        in_specs=[pl.BlockSpec((tm, K), lambda j, i: (i, 0)),
                  pl.BlockSpec((K, tn), lambda j, i: (0, j))],
        out_specs=pl.BlockSpec((tm, tn), lambda j, i: (i, j)),
        out_shape=jax.ShapeDtypeStruct((M, N), out_dtype),
        compiler_params=pltpu.CompilerParams(vmem_limit_bytes=VMEM_LIMIT),
        name="matmul",
    )(a, w)


NA_QROWS = 4
NA_KROWS = 12


def _toeplitz(v):
    W = GRID_W
    lead = v.shape[:-1]
    y = jnp.concatenate([v, jnp.zeros(lead + (1,), v.dtype)], axis=-1)
    t = jnp.tile(y, (1,) * len(lead) + (W,))[..., : W * (2 * W - 1)].reshape(lead + (W, 2 * W - 1))
    return t[..., W - 1:]


def na_bias_tables(rpb, R):
    H = rpb.shape[0]
    W = GRID_W
    nblk = R // NA_QROWS
    r0 = np.arange(nblk) * NA_QROWS
    ws = np.clip(r0 - NA_ROWS // 2, 0, R - NA_KROWS)
    types, tmap = np.unique(ws - r0, return_inverse=True)
    dc = np.arange(-(W - 1), W)
    in_rpb = np.abs(dc + 0) <= NA_COLS - 1
    cols_idx = np.clip(dc + NA_COLS - 1, 0, RPB_C - 1)
    v = jnp.where(jnp.asarray(in_rpb), rpb[:, :, cols_idx], MASKED)
    toe = _toeplitz(v)
    c = np.arange(W)[:, None]
    kc = np.arange(W)[None, :]
    cs = np.clip(c - NA_COLS // 2, 0, W - NA_COLS)
    col_ok = (kc >= cs) & (kc < cs + NA_COLS)
    toe = jnp.where(jnp.asarray(col_ok), toe, MASKED)
    toe = jnp.concatenate([toe, jnp.full((H, 1, W, W), MASKED, toe.dtype)], axis=1)
    qi = np.arange(NA_QROWS)[:, None]
    kj = np.arange(NA_KROWS)[None, :]
    tabs = []
    for ti in range(len(types)):
        blk = int(np.nonzero(tmap == ti)[0][0])
        r = r0[blk] + qi
        kr = ws[blk] + kj
        rs = np.clip(r - NA_ROWS // 2, 0, R - NA_ROWS)
        row_ok = (kr >= rs) & (kr < rs + NA_ROWS)
        dr = np.where(row_ok, kr - r + NA_ROWS - 1, RPB_R)
        tab = toe[:, dr]
        tabs.append(tab.transpose(0, 1, 3, 2, 4).reshape(H, NA_QROWS * W, NA_KROWS * W))
    tabs.append(jnp.full((H, NA_QROWS * W, NA_KROWS * W), MASKED, jnp.float32))
    return jnp.stack(tabs, axis=1).astype(jnp.float32), tmap.astype(np.int32), ws.astype(np.int32)


def _na_kernel(tmap_ref, ws_ref, q_ref, k_ref, v_ref, kc_ref, vc_ref, bias_ref, o_ref, *, scale):
    rb = pl.program_id(2)
    start = pl.multiple_of(ws_ref[rb] * GRID_W, GRID_W)
    q = q_ref[...]
    kw = k_ref[pl.ds(start, NA_KROWS * GRID_W), :]
    vw = v_ref[pl.ds(start, NA_KROWS * GRID_W), :]
    dn = (((1,), (1,)), ((), ()))
    s_nb = lax.dot_general(q, kw, dn, preferred_element_type=jnp.float32) * scale + bias_ref[0, 0]
    s_cx = lax.dot_general(q, kc_ref[...], dn, preferred_element_type=jnp.float32) * scale
    m = jnp.maximum(jnp.max(s_nb, axis=-1, keepdims=True), jnp.max(s_cx, axis=-1, keepdims=True))
    p_nb = jnp.exp(s_nb - m)
    p_cx = jnp.exp(s_cx - m)
    l = jnp.sum(p_nb, axis=-1, keepdims=True) + jnp.sum(p_cx, axis=-1, keepdims=True)
    o = (jnp.dot(p_nb.astype(jnp.bfloat16), vw, preferred_element_type=jnp.float32)
         + jnp.dot(p_cx.astype(jnp.bfloat16), vc_ref[...], preferred_element_type=jnp.float32))
    o_ref[...] = (o / l).astype(o_ref.dtype)


def na_attention(qkv, rpb, B, n_lat, n_ctx):
    rows = qkv.shape[0]
    H, dh = NA_HEADS, NA_HEAD_DIM
    R = n_lat // GRID_W
    nblk = R // NA_QROWS
    tq = NA_QROWS * GRID_W
    tk = NA_KROWS * GRID_W
    assert n_ctx == tq and n_lat % tq == 0
    bias, tmap, ws = na_bias_tables(rpb, R)
    tmap = np.concatenate([tmap, [bias.shape[1] - 1]]).astype(np.int32)
    ws = np.concatenate([ws, [0]]).astype(np.int32)

    def qblk(b, r):
        return jnp.where(r < nblk, b * nblk + r, B * nblk + b)

    grid_spec = pltpu.PrefetchScalarGridSpec(
        num_scalar_prefetch=2,
        grid=(B, H, nblk + 1),
        in_specs=[
            pl.BlockSpec((tq, dh), lambda b, h, r, tm, ws: (qblk(b, r), h)),
            pl.BlockSpec((n_lat, dh), lambda b, h, r, tm, ws: (b, H + h)),
            pl.BlockSpec((n_lat, dh), lambda b, h, r, tm, ws: (b, 2 * H + h)),
            pl.BlockSpec((n_ctx, dh), lambda b, h, r, tm, ws: (B * nblk + b, H + h)),
            pl.BlockSpec((n_ctx, dh), lambda b, h, r, tm, ws: (B * nblk + b, 2 * H + h)),
            pl.BlockSpec((1, 1, tq, tk), lambda b, h, r, tm, ws: (h, tm[r], 0, 0)),
        ],
        out_specs=pl.BlockSpec((tq, dh), lambda b, h, r, tm, ws: (qblk(b, r), h)),
    )
    return pl.pallas_call(
        functools.partial(_na_kernel, scale=dh ** -0.5),
        grid_spec=grid_spec,
        out_shape=jax.ShapeDtypeStruct((rows, H * dh), jnp.bfloat16),
        compiler_params=pltpu.CompilerParams(vmem_limit_bytes=VMEM_LIMIT),
        name="na_attention",
    )(jnp.asarray(tmap), jnp.asarray(ws), qkv, qkv, qkv, qkv, qkv, bias)


def rope_tables(n):
    dh = MLSTM_HEAD_DIM
    quarter = dh // 4
    t = np.arange(n)
    pos = np.stack([t // GRID_W, t % GRID_W], axis=1).astype(np.float32)
    lane = np.arange(dh)
    inv = jnp.asarray(ROPE_BASE, jnp.float32) ** (-jnp.arange(quarter, dtype=jnp.float32) / quarter)
    ang = jnp.asarray(pos)[:, lane // (dh // 2)] * inv[lane % quarter][None, :]
    cos, sin = jnp.cos(ang), jnp.sin(ang)
    first = jnp.asarray((lane % (dh // 2)) < quarter)[None, :]
    sin_up = jnp.where(first, -sin, 0.0)
    sin_dn = jnp.where(first, 0.0, sin)
    pad = lambda a, v: jnp.concatenate([a, jnp.full((MLSTM_CHUNK, dh), v, jnp.float32)], axis=0)
    return pad(cos, 1.0), pad(sin_up, 0.0), pad(sin_dn, 0.0)


def _log_sigmoid(x):
    return jnp.minimum(x, 0.0) - jnp.log1p(jnp.exp(-jnp.abs(x)))


def _mlstm_chunk(q_ref, k_ref, v_ref, g_ref, cos_ref, su_ref, sd_ref, b_i, b_f, C_ref, n_ref, m_ref, o_ref, rev):
    L = MLSTM_CHUNK
    dh = MLSTM_HEAD_DIM
    cos, su, sd = cos_ref[...], su_ref[...], sd_ref[...]

    def rope(x):
        return x * cos + pltpu.roll(x, dh - dh // 4, 1) * su + pltpu.roll(x, dh // 4, 1) * sd

    q = rope(q_ref[...])
    k = rope(k_ref[...] * (dh ** -0.5))
    v = v_ref[...]
    qb, kb = q.astype(jnp.bfloat16), k.astype(jnp.bfloat16)

    gt = g_ref[0, 0]
    row = lax.broadcasted_iota(jnp.int32, gt.shape, 0)
    gl = jnp.where(row == 0, gt + b_i, jnp.where(row == 1, _log_sigmoid(gt + b_f), 0.0))
    glT = gl.T
    li_row, lf_row = gl[0:1], gl[1:2]
    li_col, lf_col = glT[:, 0:1], glT[:, 1:2]

    jj = lax.broadcasted_iota(jnp.int32, (L, L), 0)
    ss = lax.broadcasted_iota(jnp.int32, (L, L), 1)
    tri = (ss >= jj) if rev else (ss <= jj)
    tri_t = (ss <= jj) if rev else (ss >= jj)
    b_col = jnp.sum(jnp.where(tri, lf_row, 0.0), axis=1, keepdims=True)
    b_row = jnp.sum(jnp.where(tri_t, lf_col, 0.0), axis=0, keepdims=True)
    g = jnp.sum(lf_row, axis=1, keepdims=True)
    C, n, m = C_ref[...], n_ref[...], m_ref[...]

    Dm = jnp.where(tri, b_col - b_row + li_row, MASKED)
    inter = b_col + m
    m_j = jnp.maximum(inter, jnp.max(Dm, axis=1, keepdims=True))
    nt = (((1,), (1,)), ((), ()))
    S = lax.dot_general(qb, kb, nt, preferred_element_type=jnp.float32) * jnp.exp(Dm - m_j)
    w_int = jnp.exp(inter - m_j)
    qC = lax.dot_general(qb, C.astype(jnp.bfloat16), nt, preferred_element_type=jnp.float32)
    num = jnp.dot(S.astype(jnp.bfloat16), v.astype(jnp.bfloat16), preferred_element_type=jnp.float32) + w_int * qC
    den = jnp.sum(S, axis=1, keepdims=True) + w_int * jnp.sum(q * n, axis=1, keepdims=True)
    o_ref[...] = num / jnp.maximum(jnp.abs(den), jnp.exp(-m_j))

    a_row = g - b_row + li_row
    a_col = g - b_col + li_col
    m_loc = jnp.max(a_row, axis=1, keepdims=True)
    wa = jnp.exp(a_col - m_loc)
    C_loc = lax.dot_general((v * wa).astype(jnp.bfloat16), kb, (((0,), (0,)), ((), ())),
                            preferred_element_type=jnp.float32)
    n_loc = jnp.sum(k * wa, axis=0, keepdims=True)
    m_new = jnp.maximum(g + m, m_loc)
    sp = jnp.exp(g + m - m_new)
    sl = jnp.exp(m_loc - m_new)
    C_ref[...] = sp * C + sl * C_loc
    n_ref[...] = sp * n + sl * n_loc
    m_ref[...] = m_new


def _mlstm_kernel(gb_ref, qf, kf, vf, gf, cf, suf, sdf, qb, kb, vb, gbk, cb, sub, sdb, of, ob,
                  Cf, nf, mf, Cb, nb, mb):
    h = pl.program_id(1)

    @pl.when(pl.program_id(2) == 0)
    def _():
        for r in (Cf, nf, mf, Cb, nb, mb):
            r[...] = jnp.zeros_like(r)

    _mlstm_chunk(qf, kf, vf, gf, cf, suf, sdf, gb_ref[0, h], gb_ref[1, h], Cf, nf, mf, of, False)
    _mlstm_chunk(qb, kb, vb, gbk, cb, sub, sdb, gb_ref[2, h], gb_ref[3, h], Cb, nb, mb, ob, True)


def mlstm_bidir(P, GT, gate_b, rope, B, n_lat, n_ctx):
    H, dh, L = MLSTM_HEADS, MLSTM_HEAD_DIM, MLSTM_CHUNK
    rows = P.shape[0]
    NB, CB = n_lat // L, n_ctx // L

    def blk(b, s, rev):
        sc = (CB - 1 - s) if rev else s
        sl = (NB - 1 - (s - CB)) if rev else (s - CB)
        return jnp.where(s < CB, B * NB + b * CB + sc, b * NB + sl)

    def rblk(s, rev):
        sl = (NB - 1 - (s - CB)) if rev else (s - CB)
        return jnp.where(s < CB, NB, sl)

    def specs(rev):
        d = 1 if rev else 0
        return [
            pl.BlockSpec((L, dh), lambda b, h, s, gb: (blk(b, s, rev), h)),
            pl.BlockSpec((L, dh), lambda b, h, s, gb: (blk(b, s, rev), H + h)),
            pl.BlockSpec((L, dh), lambda b, h, s, gb: (blk(b, s, rev), 2 * H + h)),
            pl.BlockSpec((1, 1, 8, L), lambda b, h, s, gb: (d, h, 0, blk(b, s, rev))),
            pl.BlockSpec((L, dh), lambda b, h, s, gb: (rblk(s, rev), 0)),
            pl.BlockSpec((L, dh), lambda b, h, s, gb: (rblk(s, rev), 0)),
            pl.BlockSpec((L, dh), lambda b, h, s, gb: (rblk(s, rev), 0)),
        ]

    state = [pltpu.VMEM((dh, dh), jnp.float32), pltpu.VMEM((1, dh), jnp.float32), pltpu.VMEM((1, 1), jnp.float32)]
    grid_spec = pltpu.PrefetchScalarGridSpec(
        num_scalar_prefetch=1,
        grid=(B, H, NB + CB),
        in_specs=specs(False) + specs(True),
        out_specs=[pl.BlockSpec((L, dh), lambda b, h, s, gb: (blk(b, s, False), h)),
                   pl.BlockSpec((L, dh), lambda b, h, s, gb: (blk(b, s, True), h))],
        scratch_shapes=state + state,
    )
    return pl.pallas_call(
        _mlstm_kernel,
        grid_spec=grid_spec,
        out_shape=[jax.ShapeDtypeStruct((rows, H * dh), jnp.float32)] * 2,
        name="mlstm_bidir",
    )(gate_b, P, P, P, GT, *rope, P, P, P, GT, *rope)


def gates_layout(G):
    rows = G.shape[0]
    g = G.T.reshape(2, 2, MLSTM_HEADS, rows).transpose(0, 2, 1, 3)
    return jnp.pad(g, ((0, 0), (0, 0), (0, 6), (0, 0)))


FFN_TF = 256


def _ffn_kernel(x_ref, wg_ref, wu_ref, wd_ref, gate_ref, o_ref):
    f = pl.program_id(2)
    x = x_ref[0, 0]
    a = jnp.dot(x, wg_ref[0].astype(jnp.bfloat16), preferred_element_type=jnp.float32)
    u = jnp.dot(x, wu_ref[0].astype(jnp.bfloat16), preferred_element_type=jnp.float32)
    act = (a * jax.nn.sigmoid(a) * u).astype(jnp.bfloat16)
    y = jnp.dot(act, wd_ref[0].astype(jnp.bfloat16), preferred_element_type=jnp.float32)

    @pl.when(f == 0)
    def _():
        o_ref[0, 0] = y

    @pl.when(f > 0)
    def _():
        o_ref[0, 0] += y

    @pl.when(f == pl.num_programs(2) - 1)
    def _():
        o_ref[0, 0] *= gate_ref[0, 0]


def expert_ffn(xin, w_gate, w_up, w_down, gate):
    B, E, C, D = xin.shape
    F = w_gate.shape[-1]
    return pl.pallas_call(
        _ffn_kernel,
        grid=(E, B, F // FFN_TF),
        in_specs=[pl.BlockSpec((1, 1, C, D), lambda e, b, f: (b, e, 0, 0)),
                  pl.BlockSpec((1, D, FFN_TF), lambda e, b, f: (e, 0, f)),
                  pl.BlockSpec((1, D, FFN_TF), lambda e, b, f: (e, 0, f)),
                  pl.BlockSpec((1, FFN_TF, D), lambda e, b, f: (e, f, 0)),
                  pl.BlockSpec((1, 1, C, 1), lambda e, b, f: (b, e, 0, 0))],
        out_specs=pl.BlockSpec((1, 1, C, D), lambda e, b, f: (b, e, 0, 0)),
        out_shape=jax.ShapeDtypeStruct((B, E, C, D), jnp.float32),
        compiler_params=pltpu.CompilerParams(vmem_limit_bytes=VMEM_LIMIT),
        name="expert_ffn",
    )(xin, w_gate, w_up, w_down, gate)


def rmsnorm(x, g):
    y = x * lax.rsqrt(jnp.mean(x * x, axis=-1, keepdims=True) + EPS)
    return y * g


def rowwise(fn, X, per_sample, per_ctx, B, n_lat):
    D = X.shape[-1]
    xl = fn(X[:B * n_lat].reshape(B, n_lat, D), per_sample[:, None, :]).reshape(B * n_lat, D)
    xc = fn(X[B * n_lat:], per_ctx)
    return jnp.concatenate([xl, xc], axis=0)


def multiscale_pool_diff(u):
    S, n, _ = u.shape
    t = np.arange(n)
    outs = []
    for gi, w in enumerate(POOL_WINDOWS):
        ug = u[..., gi * POOL_GROUP_DIM:(gi + 1) * POOL_GROUP_DIM]
        up = jnp.pad(ug, ((0, 0), (w // 2, w // 2), (0, 0)))
        acc = sum(up[:, d:d + n] for d in range(w))
        cnt = np.clip(t + w // 2, 0, n) - np.clip(t - w // 2, 0, n)
        outs.append(acc / jnp.asarray(cnt, jnp.float32)[None, :, None] - ug)
    return jnp.concatenate(outs, axis=-1)


def mlstm_pool_mixer(h, w_in, gate_b, head_g, pool_w, pool_scale, w_out, rope, B, n_lat, n_ctx):
    H, dh = MLSTM_HEADS, MLSTM_HEAD_DIM
    W4 = 4 * MLSTM_WIDTH
    w_main = jnp.concatenate([w_in[:, :W4], w_in[:, W4 + 4 * H:]], axis=1)
    P = matmul(h, w_main)
    G = matmul(h, w_in[:, W4:W4 + 4 * H])
    hf, hb = mlstm_bidir(P, gates_layout(G), gate_b, rope, B, n_lat, n_ctx)
    rows = h.shape[0]
    hh = (hf + hb).reshape(rows, H, dh)
    hn = hh * lax.rsqrt(jnp.mean(hh * hh, axis=-1, keepdims=True) + EPS) * head_g.reshape(H, dh)
    hn = hn.reshape(rows, MLSTM_WIDTH) * jax.nn.sigmoid(P[:, 3 * MLSTM_WIDTH:W4])
    u = P[:, W4:]
    d = jnp.concatenate([multiscale_pool_diff(u[:B * n_lat].reshape(B, n_lat, POOL_WIDTH)).reshape(B * n_lat, -1),
                         multiscale_pool_diff(u[B * n_lat:].reshape(B, n_ctx, POOL_WIDTH)).reshape(B * n_ctx, -1)], 0)
    y = jnp.einsum('ngc,gce->nge', d.reshape(rows, POOL_GROUPS, POOL_GROUP_DIM), pool_w).reshape(rows, POOL_WIDTH)
    mix = jnp.concatenate([hn, y * pool_scale], axis=-1)
    return matmul(mix, w_out)


def na_mixer(h, w_in, rpb, w_out, B, n_lat, n_ctx):
    qkv = matmul(h, w_in, jnp.bfloat16)
    return matmul(na_attention(qkv, rpb, B, n_lat, n_ctx), w_out)


def expert_choice_ffn(h, w_router, w_gate, w_up, w_down, B, n_lat, n_ctx):
    E = N_EXPERTS
    rows, D = h.shape
    aff = jax.nn.softmax(matmul(h, w_router), axis=-1)
    picks = []
    for lo, n in ((0, n_lat), (B * n_lat, n_ctx)):
        a = aff[lo:lo + B * n].reshape(B, n, E)
        gate, idx = lax.top_k(jnp.swapaxes(a, 1, 2), CAPACITY_FACTOR * n // E)
        picks.append((gate, idx + lo + (jnp.arange(B) * n)[:, None, None]))
    gate = jnp.concatenate([p[0] for p in picks], axis=-1)
    idx = jnp.concatenate([p[1] for p in picks], axis=-1)
    xin = h.astype(jnp.bfloat16)[idx]
    y = expert_ffn(xin, w_gate, w_up, w_down, gate[..., None])
    return jnp.zeros_like(h).at[idx.reshape(-1)].add(y.reshape(-1, D))


def kernel(x, c, ctx, c_ctx, ada_w, ada_b, norm_g, final_g, ev_w_in, ev_gate_b, ev_head_g, ev_pool_w,
           ev_pool_scale, ev_w_out, na_w_in, na_rpb, na_w_out, moe_w_router, moe_w_gate, moe_w_up, moe_w_down):
    B, n_lat, D = x.shape
    n_ctx = ctx.shape[1]
    X = jnp.concatenate([x.reshape(B * n_lat, D), ctx.reshape(B * n_ctx, D)], axis=0)
    rope = rope_tables(n_lat)
    mod = lambda v, sh, sc: v * (1 + sc) + sh
    for i in range(DEPTH):
        j = i // 2
        ml = jnp.split(jax.nn.silu(c) @ ada_w[i] + ada_b[i], 6, axis=-1)
        mc = jnp.split(jax.nn.silu(c_ctx) @ ada_w[i] + ada_b[i], 6, axis=-1)
        per_row = functools.partial(rowwise, B=B, n_lat=n_lat)
        h = per_row(lambda v, s: v * s, rmsnorm(X, norm_g[i, 0]), 1 + ml[1], 1 + mc[1])
        h = per_row(lambda v, s: v + s, h, ml[0], mc[0])
        if i % 2 == 0:
            o = mlstm_pool_mixer(h, ev_w_in[j], ev_gate_b[j], ev_head_g[j], ev_pool_w[j], ev_pool_scale[j],
                                 ev_w_out[j], rope, B, n_lat, n_ctx)
        else:
            o = na_mixer(h, na_w_in[j], na_rpb[j], na_w_out[j], B, n_lat, n_ctx)
        X = X + per_row(lambda v, s: v * s, o, ml[2], mc[2])
        h = per_row(lambda v, s: v * s, rmsnorm(X, norm_g[i, 1]), 1 + ml[4], 1 + mc[4])
        h = per_row(lambda v, s: v + s, h, ml[3], mc[3])
        f = expert_choice_ffn(h, moe_w_router[i], moe_w_gate[i], moe_w_up[i], moe_w_down[i], B, n_lat, n_ctx)
        X = X + per_row(lambda v, s: v * s, f, ml[5], mc[5])
    out = rmsnorm(X[:B * n_lat], final_g)
    return out.reshape(B, n_lat, D)
```

```python
import functools

import jax
import jax.numpy as jnp
import numpy as np
from jax import lax
from jax.experimental import pallas as pl
from jax.experimental.pallas import tpu as pltpu

D_MODEL = 2048
DEPTH = 4
GRID_W = 64
MLSTM_WIDTH = D_MODEL // 2
MLSTM_HEADS = 8
MLSTM_HEAD_DIM = MLSTM_WIDTH // MLSTM_HEADS
MLSTM_CHUNK = 128
POOL_WIDTH = D_MODEL // 2
POOL_WINDOWS = (2, 4, 8, 16)
POOL_GROUPS = len(POOL_WINDOWS)
POOL_GROUP_DIM = POOL_WIDTH // POOL_GROUPS
NA_HEADS = 16
NA_HEAD_DIM = D_MODEL // NA_HEADS
NA_WIDTH = NA_HEADS * NA_HEAD_DIM
NA_ROWS = 8
NA_COLS = 16
RPB_R = 2 * NA_ROWS - 1
RPB_C = 2 * NA_COLS - 1
N_EXPERTS = 16
EXPERT_FF = D_MODEL // 2
CAPACITY_FACTOR = 2
ROPE_BASE = 10000.0
EPS = 1e-6
MASKED = -1e30
VMEM_LIMIT = 48 * 1024 * 1024


def _mm_kernel(a_ref, w_ref, o_ref):
    o_ref[...] = jnp.dot(a_ref[...], w_ref[...], preferred_element_type=jnp.float32).astype(o_ref.dtype)


def matmul(a, w, out_dtype=jnp.float32, tm=512, tn=1024):
    M, K = a.shape
    N = w.shape[1]
    tm, tn = min(tm, M), min(tn, N)
    assert a.dtype == w.dtype == jnp.bfloat16 and M % tm == 0 and N % tn == 0
    return pl.pallas_call(
        _mm_kernel,
        grid=(N // tn, M // tm),
        in_specs=[pl.BlockSpec((tm, K), lambda j, i: (i, 0)),
                  pl.BlockSpec((K, tn), lambda j, i: (0, j))],
        out_specs=pl.BlockSpec((tm, tn), lambda j, i: (i, j)),
        out_shape=jax.ShapeDtypeStruct((M, N), out_dtype),
        compiler_params=pltpu.CompilerParams(vmem_limit_bytes=VMEM_LIMIT),
        name="matmul",
    )(a, w)


NA_QROWS = 4
NA_KROWS = 12


def _toeplitz(v):
    W = GRID_W
    lead = v.shape[:-1]
    y = jnp.concatenate([v, jnp.zeros(lead + (1,), v.dtype)], axis=-1)
    t = jnp.tile(y, (1,) * len(lead) + (W,))[..., : W * (2 * W - 1)].reshape(lead + (W, 2 * W - 1))
    return t[..., W - 1:]


def na_bias_tables(rpb, R):
    H = rpb.shape[0]
    W = GRID_W
    nblk = R // NA_QROWS
    r0 = np.arange(nblk) * NA_QROWS
    ws = np.clip(r0 - NA_ROWS // 2, 0, R - NA_KROWS)
    types, tmap = np.unique(ws - r0, return_inverse=True)
    dc = np.arange(-(W - 1), W)
    in_rpb = np.abs(dc + 0) <= NA_COLS - 1
    cols_idx = np.clip(dc + NA_COLS - 1, 0, RPB_C - 1)
    v = jnp.where(jnp.asarray(in_rpb), rpb[:, :, cols_idx], MASKED)
    toe = _toeplitz(v)
    c = np.arange(W)[:, None]
    kc = np.arange(W)[None, :]
    cs = np.clip(c - NA_COLS // 2, 0, W - NA_COLS)
    col_ok = (kc >= cs) & (kc < cs + NA_COLS)
    toe = jnp.where(jnp.asarray(col_ok), toe, MASKED)
    toe = jnp.concatenate([toe, jnp.full((H, 1, W, W), MASKED, toe.dtype)], axis=1)
    qi = np.arange(NA_QROWS)[:, None]
    kj = np.arange(NA_KROWS)[None, :]
    tabs = []
    for ti in range(len(types)):
        blk = int(np.nonzero(tmap == ti)[0][0])
        r = r0[blk] + qi
        kr = ws[blk] + kj
        rs = np.clip(r - NA_ROWS // 2, 0, R - NA_ROWS)
        row_ok = (kr >= rs) & (kr < rs + NA_ROWS)
        dr = np.where(row_ok, kr - r + NA_ROWS - 1, RPB_R)
        tab = toe[:, dr]
        tabs.append(tab.transpose(0, 1, 3, 2, 4).reshape(H, NA_QROWS * W, NA_KROWS * W))
    tabs.append(jnp.full((H, NA_QROWS * W, NA_KROWS * W), MASKED, jnp.float32))
    return jnp.stack(tabs, axis=1).astype(jnp.float32), tmap.astype(np.int32), ws.astype(np.int32)


def _na_kernel(tmap_ref, ws_ref, q_ref, k_ref, v_ref, kc_ref, vc_ref, bias_ref, o_ref, *, scale):
    rb = pl.program_id(2)
    start = pl.multiple_of(ws_ref[rb] * GRID_W, GRID_W)
    q = q_ref[...]
    kw = k_ref[pl.ds(start, NA_KROWS * GRID_W), :]
    vw = v_ref[pl.ds(start, NA_KROWS * GRID_W), :]
    dn = (((1,), (1,)), ((), ()))
    s_nb = lax.dot_general(q, kw, dn, preferred_element_type=jnp.float32) * scale + bias_ref[0, 0]
    s_cx = lax.dot_general(q, kc_ref[...], dn, preferred_element_type=jnp.float32) * scale
    m = jnp.maximum(jnp.max(s_nb, axis=-1, keepdims=True), jnp.max(s_cx, axis=-1, keepdims=True))
    p_nb = jnp.exp(s_nb - m)
    p_cx = jnp.exp(s_cx - m)
    l = jnp.sum(p_nb, axis=-1, keepdims=True) + jnp.sum(p_cx, axis=-1, keepdims=True)
    o = (jnp.dot(p_nb.astype(jnp.bfloat16), vw, preferred_element_type=jnp.float32)
         + jnp.dot(p_cx.astype(jnp.bfloat16), vc_ref[...], preferred_element_type=jnp.float32))
    o_ref[...] = (o / l).astype(o_ref.dtype)


def na_attention(qkv, rpb, B, n_lat, n_ctx):
    rows = qkv.shape[0]
    H, dh = NA_HEADS, NA_HEAD_DIM
    R = n_lat // GRID_W
    nblk = R // NA_QROWS
    tq = NA_QROWS * GRID_W
    tk = NA_KROWS * GRID_W
    assert n_ctx == tq and n_lat % tq == 0
    bias, tmap, ws = na_bias_tables(rpb, R)
    tmap = np.concatenate([tmap, [bias.shape[1] - 1]]).astype(np.int32)
    ws = np.concatenate([ws, [0]]).astype(np.int32)

    def qblk(b, r):
        return jnp.where(r < nblk, b * nblk + r, B * nblk + b)

    grid_spec = pltpu.PrefetchScalarGridSpec(
        num_scalar_prefetch=2,
        grid=(B, H, nblk + 1),
        in_specs=[
            pl.BlockSpec((tq, dh), lambda b, h, r, tm, ws: (qblk(b, r), h)),
            pl.BlockSpec((n_lat, dh), lambda b, h, r, tm, ws: (b, H + h)),
            pl.BlockSpec((n_lat, dh), lambda b, h, r, tm, ws: (b, 2 * H + h)),
            pl.BlockSpec((n_ctx, dh), lambda b, h, r, tm, ws: (B * nblk + b, H + h)),
            pl.BlockSpec((n_ctx, dh), lambda b, h, r, tm, ws: (B * nblk + b, 2 * H + h)),
            pl.BlockSpec((1, 1, tq, tk), lambda b, h, r, tm, ws: (h, tm[r], 0, 0)),
        ],
        out_specs=pl.BlockSpec((tq, dh), lambda b, h, r, tm, ws: (qblk(b, r), h)),
    )
    return pl.pallas_call(
        functools.partial(_na_kernel, scale=dh ** -0.5),
        grid_spec=grid_spec,
        out_shape=jax.ShapeDtypeStruct((rows, H * dh), jnp.bfloat16),
        compiler_params=pltpu.CompilerParams(vmem_limit_bytes=VMEM_LIMIT),
        name="na_attention",
    )(jnp.asarray(tmap), jnp.asarray(ws), qkv, qkv, qkv, qkv, qkv, bias)


def rope_tables(n):
    dh = MLSTM_HEAD_DIM
    quarter = dh // 4
    t = np.arange(n)
    pos = np.stack([t // GRID_W, t % GRID_W], axis=1).astype(np.float32)
    lane = np.arange(dh)
    inv = jnp.asarray(ROPE_BASE, jnp.float32) ** (-jnp.arange(quarter, dtype=jnp.float32) / quarter)
    ang = jnp.asarray(pos)[:, lane // (dh // 2)] * inv[lane % quarter][None, :]
    cos, sin = jnp.cos(ang), jnp.sin(ang)
    first = jnp.asarray((lane % (dh // 2)) < quarter)[None, :]
    sin_up = jnp.where(first, -sin, 0.0)
    sin_dn = jnp.where(first, 0.0, sin)
    pad = lambda a, v: jnp.concatenate([a, jnp.full((MLSTM_CHUNK, dh), v, jnp.float32)], axis=0)
    return pad(cos, 1.0), pad(sin_up, 0.0), pad(sin_dn, 0.0)


def _log_sigmoid(x):
    return jnp.minimum(x, 0.0) - jnp.log1p(jnp.exp(-jnp.abs(x)))


def _mlstm_chunk(q_ref, k_ref, v_ref, g_ref, cos_ref, su_ref, sd_ref, b_i, b_f, C_ref, n_ref, m_ref, o_ref, rev):
    L = MLSTM_CHUNK
    dh = MLSTM_HEAD_DIM
    cos, su, sd = cos_ref[...], su_ref[...], sd_ref[...]

    def rope(x):
        return x * cos + pltpu.roll(x, dh - dh // 4, 1) * su + pltpu.roll(x, dh // 4, 1) * sd

    q = rope(q_ref[...])
    k = rope(k_ref[...] * (dh ** -0.5))
    v = v_ref[...]
    qb, kb = q.astype(jnp.bfloat16), k.astype(jnp.bfloat16)

    gt = g_ref[0, 0]
    row = lax.broadcasted_iota(jnp.int32, gt.shape, 0)
    gl = jnp.where(row == 0, gt + b_i, jnp.where(row == 1, _log_sigmoid(gt + b_f), 0.0))
    glT = gl.T
    li_row, lf_row = gl[0:1], gl[1:2]
    li_col, lf_col = glT[:, 0:1], glT[:, 1:2]

    jj = lax.broadcasted_iota(jnp.int32, (L, L), 0)
    ss = lax.broadcasted_iota(jnp.int32, (L, L), 1)
    tri = (ss >= jj) if rev else (ss <= jj)
    tri_t = (ss <= jj) if rev else (ss >= jj)
    b_col = jnp.sum(jnp.where(tri, lf_row, 0.0), axis=1, keepdims=True)
    b_row = jnp.sum(jnp.where(tri_t, lf_col, 0.0), axis=0, keepdims=True)
    g = jnp.sum(lf_row, axis=1, keepdims=True)
    C, n, m = C_ref[...], n_ref[...], m_ref[...]

    Dm = jnp.where(tri, b_col - b_row + li_row, MASKED)
    inter = b_col + m
    m_j = jnp.maximum(inter, jnp.max(Dm, axis=1, keepdims=True))
    nt = (((1,), (1,)), ((), ()))
    S = lax.dot_general(qb, kb, nt, preferred_element_type=jnp.float32) * jnp.exp(Dm - m_j)
    w_int = jnp.exp(inter - m_j)
    qC = lax.dot_general(qb, C.astype(jnp.bfloat16), nt, preferred_element_type=jnp.float32)
    num = jnp.dot(S.astype(jnp.bfloat16), v.astype(jnp.bfloat16), preferred_element_type=jnp.float32) + w_int * qC
    den = jnp.sum(S, axis=1, keepdims=True) + w_int * jnp.sum(q * n, axis=1, keepdims=True)
    o_ref[...] = num / jnp.maximum(jnp.abs(den), jnp.exp(-m_j))

    a_row = g - b_row + li_row
    a_col = g - b_col + li_col
    m_loc = jnp.max(a_row, axis=1, keepdims=True)
    wa = jnp.exp(a_col - m_loc)
    C_loc = lax.dot_general((v * wa).astype(jnp.bfloat16), kb, (((0,), (0,)), ((), ())),
                            preferred_element_type=jnp.float32)
    n_loc = jnp.sum(k * wa, axis=0, keepdims=True)
    m_new = jnp.maximum(g + m, m_loc)
    sp = jnp.exp(g + m - m_new)
    sl = jnp.exp(m_loc - m_new)
    C_ref[...] = sp * C + sl * C_loc
    n_ref[...] = sp * n + sl * n_loc
    m_ref[...] = m_new


def _mlstm_kernel(gb_ref, qf, kf, vf, gf, cf, suf, sdf, qb, kb, vb, gbk, cb, sub, sdb, of, ob,
                  Cf, nf, mf, Cb, nb, mb):
    h = pl.program_id(1)

    @pl.when(pl.program_id(2) == 0)
    def _():
        for r in (Cf, nf, mf, Cb, nb, mb):
            r[...] = jnp.zeros_like(r)

    _mlstm_chunk(qf, kf, vf, gf, cf, suf, sdf, gb_ref[0, h], gb_ref[1, h], Cf, nf, mf, of, False)
    _mlstm_chunk(qb, kb, vb, gbk, cb, sub, sdb, gb_ref[2, h], gb_ref[3, h], Cb, nb, mb, ob, True)


def mlstm_bidir(P, GT, gate_b, rope, B, n_lat, n_ctx):
    H, dh, L = MLSTM_HEADS, MLSTM_HEAD_DIM, MLSTM_CHUNK
    rows = P.shape[0]
    NB, CB = n_lat // L, n_ctx // L

    def blk(b, s, rev):
        sc = (CB - 1 - s) if rev else s
        sl = (NB - 1 - (s - CB)) if rev else (s - CB)
        return jnp.where(s < CB, B * NB + b * CB + sc, b * NB + sl)

    def rblk(s, rev):
        sl = (NB - 1 - (s - CB)) if rev else (s - CB)
        return jnp.where(s < CB, NB, sl)

    def specs(rev):
        d = 1 if rev else 0
        return [
            pl.BlockSpec((L, dh), lambda b, h, s, gb: (blk(b, s, rev), h)),
            pl.BlockSpec((L, dh), lambda b, h, s, gb: (blk(b, s, rev), H + h)),
            pl.BlockSpec((L, dh), lambda b, h, s, gb: (blk(b, s, rev), 2 * H + h)),
            pl.BlockSpec((1, 1, 8, L), lambda b, h, s, gb: (d, h, 0, blk(b, s, rev))),
            pl.BlockSpec((L, dh), lambda b, h, s, gb: (rblk(s, rev), 0)),
            pl.BlockSpec((L, dh), lambda b, h, s, gb: (rblk(s, rev), 0)),
            pl.BlockSpec((L, dh), lambda b, h, s, gb: (rblk(s, rev), 0)),
        ]

    state = [pltpu.VMEM((dh, dh), jnp.float32), pltpu.VMEM((1, dh), jnp.float32), pltpu.VMEM((1, 1), jnp.float32)]
    grid_spec = pltpu.PrefetchScalarGridSpec(
        num_scalar_prefetch=1,
        grid=(B, H, NB + CB),
        in_specs=specs(False) + specs(True),
        out_specs=[pl.BlockSpec((L, dh), lambda b, h, s, gb: (blk(b, s, False), h)),
                   pl.BlockSpec((L, dh), lambda b, h, s, gb: (blk(b, s, True), h))],
        scratch_shapes=state + state,
    )
    return pl.pallas_call(
        _mlstm_kernel,
        grid_spec=grid_spec,
        out_shape=[jax.ShapeDtypeStruct((rows, H * dh), jnp.float32)] * 2,
        name="mlstm_bidir",
    )(gate_b, P, P, P, GT, *rope, P, P, P, GT, *rope)


def gates_layout(G):
    rows = G.shape[0]
    g = G.T.reshape(2, 2, MLSTM_HEADS, rows).transpose(0, 2, 1, 3)
    return jnp.pad(g, ((0, 0), (0, 0), (0, 6), (0, 0)))


FFN_TF = 256


def _ffn_kernel(x_ref, wg_ref, wu_ref, wd_ref, gate_ref, o_ref):
    f = pl.program_id(2)
    x = x_ref[0, 0]
    a = jnp.dot(x, wg_ref[0].astype(jnp.bfloat16), preferred_element_type=jnp.float32)
    u = jnp.dot(x, wu_ref[0].astype(jnp.bfloat16), preferred_element_type=jnp.float32)
    act = (a * jax.nn.sigmoid(a) * u).astype(jnp.bfloat16)
    y = jnp.dot(act, wd_ref[0].astype(jnp.bfloat16), preferred_element_type=jnp.float32)

    @pl.when(f == 0)
    def _():
        o_ref[0, 0] = y

    @pl.when(f > 0)
    def _():
        o_ref[0, 0] += y

    @pl.when(f == pl.num_programs(2) - 1)
    def _():
        o_ref[0, 0] *= gate_ref[0, 0]


def expert_ffn(xin, w_gate, w_up, w_down, gate):
    B, E, C, D = xin.shape
    F = w_gate.shape[-1]
    return pl.pallas_call(
        _ffn_kernel,
        grid=(E, B, F // FFN_TF),
        in_specs=[pl.BlockSpec((1, 1, C, D), lambda e, b, f: (b, e, 0, 0)),
                  pl.BlockSpec((1, D, FFN_TF), lambda e, b, f: (e, 0, f)),
                  pl.BlockSpec((1, D, FFN_TF), lambda e, b, f: (e, 0, f)),
                  pl.BlockSpec((1, FFN_TF, D), lambda e, b, f: (e, f, 0)),
                  pl.BlockSpec((1, 1, C, 1), lambda e, b, f: (b, e, 0, 0))],
        out_specs=pl.BlockSpec((1, 1, C, D), lambda e, b, f: (b, e, 0, 0)),
        out_shape=jax.ShapeDtypeStruct((B, E, C, D), jnp.float32),
        compiler_params=pltpu.CompilerParams(vmem_limit_bytes=VMEM_LIMIT),
        name="expert_ffn",
    )(xin, w_gate, w_up, w_down, gate)


ROW_TILE = 256


def _res_norm_kernel(*refs, has_res, has_router):
    it = iter(refs)
    x_ref = next(it)
    f_ref, gate_ref = (next(it), next(it)) if has_res else (None, None)
    g_ref, shift_ref, scale1_ref = next(it), next(it), next(it)
    wr_ref = next(it) if has_router else None
    xo_ref = next(it) if has_res else None
    h_ref = next(it)
    aff_ref = next(it) if has_router else None

    x = x_ref[...]
    if has_res:
        x = x + gate_ref[0] * f_ref[...]
        xo_ref[...] = x
    y = x * lax.rsqrt(jnp.mean(x * x, axis=-1, keepdims=True) + EPS) * g_ref[...]
    h = (y * scale1_ref[0] + shift_ref[0]).astype(h_ref.dtype)
    h_ref[...] = h
    if has_router:
        logits = jnp.dot(h, wr_ref[...], preferred_element_type=jnp.float32)
        e = jnp.exp(logits - jnp.max(logits, axis=-1, keepdims=True))
        aff_ref[...] = e / jnp.sum(e, axis=-1, keepdims=True)


def res_norm(X, g, shift, scale1, n_lat, n_groups, res=None, w_router=None, out_dtype=jnp.bfloat16):
    rows, D = X.shape
    tm = ROW_TILE
    per_sample = n_lat // tm
    grp = lambda t: (jnp.minimum(t // per_sample, n_groups - 1), 0, 0)
    row_spec = pl.BlockSpec((tm, D), lambda t: (t, 0))
    vec_spec = pl.BlockSpec((1, 1, D), grp)
    ins, in_specs = [X], [row_spec]
    if res is not None:
        F, gate = res
        ins += [F, gate]
        in_specs += [row_spec, vec_spec]
    ins += [g.reshape(1, D), shift, scale1]
    in_specs += [pl.BlockSpec((1, D), lambda t: (0, 0)), vec_spec, vec_spec]
    outs, out_specs = [], []
    if res is not None:
        outs.append(jax.ShapeDtypeStruct((rows, D), X.dtype))
        out_specs.append(row_spec)
    outs.append(jax.ShapeDtypeStruct((rows, D), out_dtype))
    out_specs.append(row_spec)
    if w_router is not None:
        E = w_router.shape[1]
        ins.append(w_router)
        in_specs.append(pl.BlockSpec((D, E), lambda t: (0, 0)))
        outs.append(jax.ShapeDtypeStruct((rows, E), jnp.float32))
        out_specs.append(pl.BlockSpec((tm, E), lambda t: (t, 0)))
    return pl.pallas_call(
        functools.partial(_res_norm_kernel, has_res=res is not None, has_router=w_router is not None),
        grid=(rows // tm,),
        in_specs=in_specs,
        out_specs=out_specs,
        out_shape=outs,
        input_output_aliases={0: 0} if res is not None else {},
        compiler_params=pltpu.CompilerParams(vmem_limit_bytes=VMEM_LIMIT),
        name="res_norm",
    )(*ins)


POOL_HALO = 8


def _even_mix_kernel(hf_ref, hb_ref, o_ref, u_ref, up_ref, un_ref, hg_ref, pw_ref, ps_ref, mix_ref, xe_ref,
                     *, n_lat, lat_tiles, n_ctx):
    tm = ROW_TILE
    H, dh = MLSTM_HEADS, MLSTM_HEAD_DIM
    t = pl.program_id(0)
    hh = hf_ref[...] + hb_ref[...]
    for h in range(H):
        sl = slice(h * dh, (h + 1) * dh)
        x = hh[:, sl]
        hn = x * lax.rsqrt(jnp.mean(x * x, axis=-1, keepdims=True) + EPS) * hg_ref[:, sl]
        mix_ref[:, sl] = (hn * jax.nn.sigmoid(o_ref[:, sl])).astype(mix_ref.dtype)

    is_lat = t < lat_tiles
    n_seq = jnp.where(is_lat, n_lat, n_ctx)
    p0 = jnp.where(is_lat, (t % (n_lat // tm)) * tm, ((t - lat_tiles) % (n_ctx // tm)) * tm)
    pos_prev = p0 - POOL_HALO + lax.broadcasted_iota(jnp.int32, (POOL_HALO, 1), 0)
    pos_next = p0 + tm + lax.broadcasted_iota(jnp.int32, (POOL_HALO, 1), 0)
    xe_ref[0:POOL_HALO, :] = jnp.where(pos_prev >= 0, up_ref[...], 0.0)
    xe_ref[POOL_HALO:POOL_HALO + tm, :] = u_ref[...]
    xe_ref[POOL_HALO + tm:, :] = jnp.where(pos_next < n_seq, un_ref[...], 0.0)
    pos = p0 + lax.broadcasted_iota(jnp.int32, (tm, 1), 0)
    for gi, w in enumerate(POOL_WINDOWS):
        cs = slice(gi * POOL_GROUP_DIM, (gi + 1) * POOL_GROUP_DIM)
        acc = xe_ref[POOL_HALO - w // 2:POOL_HALO - w // 2 + tm, cs]
        for d in range(1, w):
            acc = acc + xe_ref[POOL_HALO - w // 2 + d:POOL_HALO - w // 2 + d + tm, cs]
        cnt = (jnp.minimum(pos + w // 2, n_seq) - jnp.maximum(pos - w // 2, 0)).astype(jnp.float32)
        diff = (acc / cnt - u_ref[:, cs]).astype(jnp.bfloat16)
        y = jnp.dot(diff, pw_ref[gi], preferred_element_type=jnp.float32) * ps_ref[:, cs]
        mix_ref[:, MLSTM_WIDTH + gi * POOL_GROUP_DIM:MLSTM_WIDTH + (gi + 1) * POOL_GROUP_DIM] = y.astype(mix_ref.dtype)


def even_mix(hf, hb, P, head_g, pool_w, pool_scale, B, n_lat, n_ctx):
    rows = hf.shape[0]
    tm = ROW_TILE
    hpt = tm // POOL_HALO
    last_halo = rows // POOL_HALO - 1
    wide = pl.BlockSpec((tm, MLSTM_WIDTH), lambda t: (t, 0))
    return pl.pallas_call(
        functools.partial(_even_mix_kernel, n_lat=n_lat, lat_tiles=B * n_lat // tm, n_ctx=n_ctx),
        grid=(rows // tm,),
        in_specs=[wide, wide,
                  pl.BlockSpec((tm, MLSTM_WIDTH), lambda t: (t, 3)),
                  pl.BlockSpec((tm, POOL_WIDTH), lambda t: (t, 4)),
                  pl.BlockSpec((POOL_HALO, POOL_WIDTH), lambda t: (jnp.maximum(t * hpt - 1, 0), 4)),
                  pl.BlockSpec((POOL_HALO, POOL_WIDTH), lambda t: (jnp.minimum((t + 1) * hpt, last_halo), 4)),
                  pl.BlockSpec((1, MLSTM_WIDTH), lambda t: (0, 0)),
                  pl.BlockSpec((POOL_GROUPS, POOL_GROUP_DIM, POOL_GROUP_DIM), lambda t: (0, 0, 0)),
                  pl.BlockSpec((1, POOL_WIDTH), lambda t: (0, 0))],
        out_specs=pl.BlockSpec((tm, MLSTM_WIDTH + POOL_WIDTH), lambda t: (t, 0)),
        out_shape=jax.ShapeDtypeStruct((rows, MLSTM_WIDTH + POOL_WIDTH), jnp.bfloat16),
        scratch_shapes=[pltpu.VMEM((tm + 2 * POOL_HALO, POOL_WIDTH), jnp.float32)],
        compiler_params=pltpu.CompilerParams(vmem_limit_bytes=VMEM_LIMIT),
        name="even_mix",
    )(hf, hb, P, P, P, P, head_g.reshape(1, -1), pool_w.astype(jnp.bfloat16), pool_scale.reshape(1, -1))


def mlstm_pool_mixer(h, w_in, gate_b, head_g, pool_w, pool_scale, w_out, rope, B, n_lat, n_ctx):
    H = MLSTM_HEADS
    W4 = 4 * MLSTM_WIDTH
    w_in = w_in.astype(jnp.bfloat16)
    w_main = jnp.concatenate([w_in[:, :W4], w_in[:, W4 + 4 * H:]], axis=1)
    P = matmul(h, w_main)
    G = matmul(h, w_in[:, W4:W4 + 4 * H])
    hf, hb = mlstm_bidir(P, gates_layout(G), gate_b, rope, B, n_lat, n_ctx)
    mix = even_mix(hf, hb, P, head_g, pool_w, pool_scale, B, n_lat, n_ctx)
    return matmul(mix, w_out.astype(jnp.bfloat16))


def na_mixer(h, w_in, rpb, w_out, B, n_lat, n_ctx):
    qkv = matmul(h, w_in.astype(jnp.bfloat16), jnp.bfloat16)
    return matmul(na_attention(qkv, rpb, B, n_lat, n_ctx), w_out.astype(jnp.bfloat16))


def expert_choice_ffn(h, aff, w_gate, w_up, w_down, B, n_lat, n_ctx):
    E = N_EXPERTS
    rows, D = h.shape
    picks = []
    for lo, n in ((0, n_lat), (B * n_lat, n_ctx)):
        a = aff[lo:lo + B * n].reshape(B, n, E)
        gate, idx = lax.top_k(jnp.swapaxes(a, 1, 2), CAPACITY_FACTOR * n // E)
        picks.append((gate, idx + lo + (jnp.arange(B) * n)[:, None, None]))
    gate = jnp.concatenate([p[0] for p in picks], axis=-1)
    idx = jnp.concatenate([p[1] for p in picks], axis=-1)
    xin = h[idx]
    y = expert_ffn(xin, w_gate, w_up, w_down, gate[..., None])
    return jnp.zeros((rows, D), jnp.float32).at[idx.reshape(-1)].add(y.reshape(-1, D))


def kernel(x, c, ctx, c_ctx, ada_w, ada_b, norm_g, final_g, ev_w_in, ev_gate_b, ev_head_g, ev_pool_w,
           ev_pool_scale, ev_w_out, na_w_in, na_rpb, na_w_out, moe_w_router, moe_w_gate, moe_w_up, moe_w_down):
    B, n_lat, D = x.shape
    n_ctx = ctx.shape[1]
    G = B + 1
    X = jnp.concatenate([x.reshape(B * n_lat, D), ctx.reshape(B * n_ctx, D)], axis=0)
    rope = rope_tables(n_lat)
    cc = jax.nn.silu(jnp.concatenate([c, c_ctx[None]], axis=0)).astype(jnp.bfloat16)
    cc = jnp.pad(cc, ((0, 16 - G), (0, 0)))
    norm = functools.partial(res_norm, n_lat=n_lat, n_groups=G)
    res = None
    for i in range(DEPTH):
        j = i // 2
        ada = matmul(cc, ada_w[i].astype(jnp.bfloat16))[:G] + ada_b[i]
        sh, sc, ga, shf, scf, gaf = (v[:, None, :] for v in jnp.split(ada, 6, axis=-1))
        if res is None:
            h = norm(X, norm_g[i, 0], sh, 1 + sc)[0]
        else:
            X, h = norm(X, norm_g[i, 0], sh, 1 + sc, res=res)
        if i % 2 == 0:
            o = mlstm_pool_mixer(h, ev_w_in[j], ev_gate_b[j], ev_head_g[j], ev_pool_w[j], ev_pool_scale[j],
                                 ev_w_out[j], rope, B, n_lat, n_ctx)
        else:
            o = na_mixer(h, na_w_in[j], na_rpb[j], na_w_out[j], B, n_lat, n_ctx)
        X, h, aff = norm(X, norm_g[i, 1], shf, 1 + scf, res=(o, ga), w_router=moe_w_router[i].astype(jnp.bfloat16))
        f = expert_choice_ffn(h, aff, moe_w_gate[i], moe_w_up[i], moe_w_down[i], B, n_lat, n_ctx)
        res = (f, gaf)
    zeros = jnp.zeros((G, 1, D), jnp.float32)
    _, out = norm(X, final_g, zeros, zeros + 1, res=res, out_dtype=jnp.float32)
    return out[:B * n_lat].reshape(B, n_lat, D)
```

```python
import functools

import jax
import jax.numpy as jnp
import numpy as np
from jax import lax
from jax.experimental import pallas as pl
from jax.experimental.pallas import tpu as pltpu

D_MODEL = 2048
DEPTH = 4
GRID_W = 64
MLSTM_WIDTH = D_MODEL // 2
MLSTM_HEADS = 8
MLSTM_HEAD_DIM = MLSTM_WIDTH // MLSTM_HEADS
MLSTM_CHUNK = 128
POOL_WIDTH = D_MODEL // 2
POOL_WINDOWS = (2, 4, 8, 16)
POOL_GROUPS = len(POOL_WINDOWS)
POOL_GROUP_DIM = POOL_WIDTH // POOL_GROUPS
NA_HEADS = 16
NA_HEAD_DIM = D_MODEL // NA_HEADS
NA_WIDTH = NA_HEADS * NA_HEAD_DIM
NA_ROWS = 8
NA_COLS = 16
RPB_R = 2 * NA_ROWS - 1
RPB_C = 2 * NA_COLS - 1
N_EXPERTS = 16
EXPERT_FF = D_MODEL // 2
CAPACITY_FACTOR = 2
ROPE_BASE = 10000.0
EPS = 1e-6
MASKED = -1e30
VMEM_LIMIT = 48 * 1024 * 1024


def _mm_kernel(a_ref, w_ref, o_ref):
    o_ref[...] = jnp.dot(a_ref[...], w_ref[...], preferred_element_type=jnp.float32).astype(o_ref.dtype)


def matmul(a, w, out_dtype=jnp.float32, tm=512, tn=1024):
    M, K = a.shape
    N = w.shape[1]
    tm, tn = min(tm, M), min(tn, N)
    assert a.dtype == w.dtype == jnp.bfloat16 and M % tm == 0 and N % tn == 0
    return pl.pallas_call(
        _mm_kernel,
        grid=(N // tn, M // tm),
        in_specs=[pl.BlockSpec((tm, K), lambda j, i: (i, 0)),
                  pl.BlockSpec((K, tn), lambda j, i: (0, j))],
        out_specs=pl.BlockSpec((tm, tn), lambda j, i: (i, j)),
        out_shape=jax.ShapeDtypeStruct((M, N), out_dtype),
        compiler_params=pltpu.CompilerParams(vmem_limit_bytes=VMEM_LIMIT),
        name="matmul",
    )(a, w)


def _ada_kernel(c_ref, w_ref, b_ref, o_ref):
    o_ref[0] = jnp.dot(c_ref[...], w_ref[0].astype(jnp.bfloat16), preferred_element_type=jnp.float32) + b_ref[0]


def ada_modulation(cc, ada_w, ada_b, tn=1024):
    M, D = cc.shape
    depth, _, N = ada_w.shape
    return pl.pallas_call(
        _ada_kernel,
        grid=(depth, N // tn),
        in_specs=[pl.BlockSpec((M, D), lambda l, j: (0, 0)),
                  pl.BlockSpec((1, D, tn), lambda l, j: (l, 0, j)),
                  pl.BlockSpec((1, 1, tn), lambda l, j: (l, 0, j))],
        out_specs=pl.BlockSpec((1, M, tn), lambda l, j: (l, 0, j)),
        out_shape=jax.ShapeDtypeStruct((depth, M, N), jnp.float32),
        compiler_params=pltpu.CompilerParams(vmem_limit_bytes=VMEM_LIMIT),
        name="ada_modulation",
    )(cc, ada_w, ada_b.reshape(depth, 1, N))


NA_QROWS = 4
NA_KROWS = 12
NA_STEP_HEADS = 2
LOG2E = 1.4426950408889634


def _toeplitz(v):
    W = GRID_W
    lead = v.shape[:-1]
    y = jnp.concatenate([v, jnp.zeros(lead + (1,), v.dtype)], axis=-1)
    t = jnp.tile(y, (1,) * len(lead) + (W,))[..., : W * (2 * W - 1)].reshape(lead + (W, 2 * W - 1))
    return t[..., W - 1:]


def na_bias_tables(rpb, R):
    H = rpb.shape[0]
    W = GRID_W
    nblk = R // NA_QROWS
    r0 = np.arange(nblk) * NA_QROWS
    ws = np.clip(r0 - NA_ROWS // 2, 0, R - NA_KROWS)
    types, tmap = np.unique(ws - r0, return_inverse=True)
    dc = np.arange(-(W - 1), W)
    in_rpb = np.abs(dc + 0) <= NA_COLS - 1
    cols_idx = np.clip(dc + NA_COLS - 1, 0, RPB_C - 1)
    v = jnp.where(jnp.asarray(in_rpb), rpb[:, :, cols_idx], MASKED)
    toe = _toeplitz(v)
    c = np.arange(W)[:, None]
    kc = np.arange(W)[None, :]
    cs = np.clip(c - NA_COLS // 2, 0, W - NA_COLS)
    col_ok = (kc >= cs) & (kc < cs + NA_COLS)
    toe = jnp.where(jnp.asarray(col_ok), toe, MASKED)
    toe = jnp.concatenate([toe, jnp.full((H, 1, W, W), MASKED, toe.dtype)], axis=1)
    qi = np.arange(NA_QROWS)[:, None]
    kj = np.arange(NA_KROWS)[None, :]
    tabs = []
    for ti in range(len(types)):
        blk = int(np.nonzero(tmap == ti)[0][0])
        r = r0[blk] + qi
        kr = ws[blk] + kj
        rs = np.clip(r - NA_ROWS // 2, 0, R - NA_ROWS)
        row_ok = (kr >= rs) & (kr < rs + NA_ROWS)
        dr = np.where(row_ok, kr - r + NA_ROWS - 1, RPB_R)
        tab = toe[:, dr]
        tabs.append(tab.transpose(0, 1, 3, 2, 4).reshape(H, NA_QROWS * W, NA_KROWS * W))
    tabs.append(jnp.full((H, NA_QROWS * W, NA_KROWS * W), MASKED, jnp.float32))
    return jnp.stack(tabs, axis=1).astype(jnp.float32), tmap.astype(np.int32), ws.astype(np.int32)


def _na_kernel(tmap_ref, ws_ref, q_ref, k_ref, v_ref, kc_ref, vc_ref, bias_ref, o_ref, *, qscale):
    rb = pl.program_id(2)
    start = pl.multiple_of(ws_ref[rb] * GRID_W, GRID_W)
    dh = NA_HEAD_DIM
    tk = NA_KROWS * GRID_W
    dn = (((1,), (1,)), ((), ()))
    ones_w = jnp.ones((tk, dh), jnp.bfloat16)
    ones_c = jnp.ones((kc_ref.shape[0], dh), jnp.bfloat16)
    for hh in range(NA_STEP_HEADS):
        ls = slice(hh * dh, (hh + 1) * dh)
        q = (q_ref[:, ls].astype(jnp.float32) * qscale).astype(jnp.bfloat16)
        kw = k_ref[pl.ds(start, tk), ls]
        vw = jnp.concatenate([v_ref[pl.ds(start, tk), ls], ones_w], axis=1)
        vc = jnp.concatenate([vc_ref[:, ls], ones_c], axis=1)
        s_nb = lax.dot_general(q, kw, dn, preferred_element_type=jnp.float32) + bias_ref[hh, 0]
        s_cx = lax.dot_general(q, kc_ref[:, ls], dn, preferred_element_type=jnp.float32)
        m = jnp.maximum(jnp.max(s_nb, axis=-1, keepdims=True), jnp.max(s_cx, axis=-1, keepdims=True))
        p_nb = jnp.exp2(s_nb - m).astype(jnp.bfloat16)
        p_cx = jnp.exp2(s_cx - m).astype(jnp.bfloat16)
        ol = (jnp.dot(p_nb, vw, preferred_element_type=jnp.float32)
              + jnp.dot(p_cx, vc, preferred_element_type=jnp.float32))
        o_ref[:, ls] = (ol[:, :dh] / ol[:, dh:dh + 1]).astype(o_ref.dtype)


def na_attention(qkv, rpb, B, n_lat, n_ctx):
    rows = qkv.shape[0]
    H, dh = NA_HEADS, NA_HEAD_DIM
    HS = NA_STEP_HEADS
    HG = H // HS
    R = n_lat // GRID_W
    nblk = R // NA_QROWS
    tq = NA_QROWS * GRID_W
    tk = NA_KROWS * GRID_W
    assert n_ctx == tq and n_lat % tq == 0
    bias, tmap, ws = na_bias_tables(rpb, R)
    bias = jnp.maximum(bias * LOG2E, MASKED)
    tmap = np.concatenate([tmap, [bias.shape[1] - 1]]).astype(np.int32)
    ws = np.concatenate([ws, [0]]).astype(np.int32)

    def qblk(b, r):
        return jnp.where(r < nblk, b * nblk + r, B * nblk + b)

    grid_spec = pltpu.PrefetchScalarGridSpec(
        num_scalar_prefetch=2,
        grid=(B, HG, nblk + 1),
        in_specs=[
            pl.BlockSpec((tq, HS * dh), lambda b, h, r, tm, ws: (qblk(b, r), h)),
            pl.BlockSpec((n_lat, HS * dh), lambda b, h, r, tm, ws: (b, HG + h)),
            pl.BlockSpec((n_lat, HS * dh), lambda b, h, r, tm, ws: (b, 2 * HG + h)),
            pl.BlockSpec((n_ctx, HS * dh), lambda b, h, r, tm, ws: (B * nblk + b, HG + h)),
            pl.BlockSpec((n_ctx, HS * dh), lambda b, h, r, tm, ws: (B * nblk + b, 2 * HG + h)),
            pl.BlockSpec((HS, 1, tq, tk), lambda b, h, r, tm, ws: (h, tm[r], 0, 0)),
        ],
        out_specs=pl.BlockSpec((tq, HS * dh), lambda b, h, r, tm, ws: (qblk(b, r), h)),
    )
    return pl.pallas_call(
        functools.partial(_na_kernel, qscale=dh ** -0.5 * LOG2E),
        grid_spec=grid_spec,
        out_shape=jax.ShapeDtypeStruct((rows, H * dh), jnp.bfloat16),
        compiler_params=pltpu.CompilerParams(vmem_limit_bytes=VMEM_LIMIT),
        name="na_attention",
    )(jnp.asarray(tmap), jnp.asarray(ws), qkv, qkv, qkv, qkv, qkv, bias)


MLSTM_STEP_CHUNKS = 2
MLSTM_STEP_ROWS = MLSTM_CHUNK * MLSTM_STEP_CHUNKS


def rope_tables(n):
    dh = MLSTM_HEAD_DIM
    quarter = dh // 4
    t = np.arange(n)
    pos = np.stack([t // GRID_W, t % GRID_W], axis=1).astype(np.float32)
    lane = np.arange(dh)
    inv = jnp.asarray(ROPE_BASE, jnp.float32) ** (-jnp.arange(quarter, dtype=jnp.float32) / quarter)
    ang = jnp.asarray(pos)[:, lane // (dh // 2)] * inv[lane % quarter][None, :]
    cos, sin = jnp.cos(ang), jnp.sin(ang)
    first = jnp.asarray((lane % (dh // 2)) < quarter)[None, :]
    sin_up = jnp.where(first, -sin, 0.0)
    sin_dn = jnp.where(first, 0.0, sin)
    pad = lambda a, v: jnp.concatenate([a, jnp.full((MLSTM_STEP_ROWS, dh), v, jnp.float32)], axis=0)
    return pad(cos, 1.0), pad(sin_up, 0.0), pad(sin_dn, 0.0)


def _log_sigmoid(x):
    return jnp.minimum(x, 0.0) - jnp.log1p(jnp.exp(-jnp.abs(x)))


def _mlstm_chunk(qb, kb, v, gt, b_i, b_f, state, rev):
    L = MLSTM_CHUNK
    dh = MLSTM_HEAD_DIM
    f32, bf16 = jnp.float32, jnp.bfloat16
    Cn, m = state

    row = lax.broadcasted_iota(jnp.int32, gt.shape, 0)
    gl = jnp.where(row == 0, gt + b_i, jnp.where(row == 1, _log_sigmoid(gt + b_f), 0.0))
    glT = gl.T
    li_row, lf_row = gl[0:1], gl[1:2]
    li_col, lf_col = glT[:, 0:1], glT[:, 1:2]
    jj = lax.broadcasted_iota(jnp.int32, (L, L), 0)
    ss = lax.broadcasted_iota(jnp.int32, (L, L), 1)
    tri = (ss >= jj) if rev else (ss <= jj)
    tri_t = (ss <= jj) if rev else (ss >= jj)
    b_col = jnp.sum(jnp.where(tri, lf_row, 0.0), axis=1, keepdims=True)
    b_row = jnp.sum(jnp.where(tri_t, lf_col, 0.0), axis=0, keepdims=True)
    g = jnp.sum(lf_row, axis=1, keepdims=True)

    Dm = jnp.where(tri, b_col - b_row + li_row, MASKED)
    inter = b_col + m
    m_j = jnp.maximum(inter, jnp.max(Dm, axis=1, keepdims=True))
    nt = (((1,), (1,)), ((), ()))
    S = lax.dot_general(qb, kb, nt, preferred_element_type=f32) * jnp.exp(Dm - m_j)
    w_int = jnp.exp(inter - m_j)
    qCn = lax.dot_general(qb, Cn.astype(bf16), nt, preferred_element_type=f32)
    num = jnp.dot(S.astype(bf16), v.astype(bf16), preferred_element_type=f32) + w_int * qCn[:, :dh]
    den = jnp.sum(S, axis=1, keepdims=True) + w_int * qCn[:, dh:dh + 1]
    out = num / jnp.maximum(jnp.abs(den), jnp.exp(-m_j))

    a_row = g - b_row + li_row
    a_col = g - b_col + li_col
    m_loc = jnp.max(a_row, axis=1, keepdims=True)
    wa = jnp.exp(a_col - m_loc)
    C_loc = lax.dot_general((v * wa).astype(bf16), kb, (((0,), (0,)), ((), ())), preferred_element_type=f32)
    n_loc = jnp.sum(kb.astype(f32) * wa, axis=0, keepdims=True)
    upd = jnp.concatenate([C_loc, jnp.broadcast_to(n_loc, (8, dh))], axis=0)
    m_new = jnp.maximum(g + m, m_loc)
    sp = jnp.exp(g + m - m_new)
    sl = jnp.exp(m_loc - m_new)
    return out, (sp * Cn + sl * upd, m_new)


def _mlstm_direction(q_ref, k_ref, v_ref, g_ref, b_i, b_f, state_refs, o_ref, rev):
    L = MLSTM_CHUNK
    state = tuple(r[...] for r in state_refs)
    order = range(MLSTM_STEP_CHUNKS - 1, -1, -1) if rev else range(MLSTM_STEP_CHUNKS)
    for t in order:
        rs = slice(t * L, (t + 1) * L)
        out, state = _mlstm_chunk(q_ref[rs, :], k_ref[rs, :], v_ref[rs, :], g_ref[0, 0, :, rs], b_i, b_f, state, rev)
        o_ref[rs, :] = out
    for r, val in zip(state_refs, state):
        r[...] = val


def _mlstm_kernel(gb_ref, qf, kf, vf, gf, qb, kb, vb, gbk, of, ob, Cf, mf, Cb, mb):
    h = pl.program_id(1)

    @pl.when(pl.program_id(2) == 0)
    def _():
        for r in (Cf, mf, Cb, mb):
            r[...] = jnp.zeros_like(r)

    _mlstm_direction(qf, kf, vf, gf, gb_ref[0, h], gb_ref[1, h], (Cf, mf), of, False)
    _mlstm_direction(qb, kb, vb, gbk, gb_ref[2, h], gb_ref[3, h], (Cb, mb), ob, True)


def _rope_kernel(p_ref, cos_ref, su_ref, sd_ref, o_ref):
    dh = MLSTM_HEAD_DIM
    cos, su, sd = cos_ref[...], su_ref[...], sd_ref[...]
    for hs in range(2 * MLSTM_HEADS):
        ls = slice(hs * dh, (hs + 1) * dh)
        x = p_ref[:, ls] if hs < MLSTM_HEADS else p_ref[:, ls] * (dh ** -0.5)
        y = x * cos + pltpu.roll(x, dh - dh // 4, 1) * su + pltpu.roll(x, dh // 4, 1) * sd
        o_ref[:, ls] = y.astype(o_ref.dtype)


def rope_qk(P, rope, B, n_lat):
    rows = P.shape[0]
    tm = ROW_TILE
    dh = MLSTM_HEAD_DIM
    per_sample, lat_tiles = n_lat // tm, B * n_lat // tm
    tab = pl.BlockSpec((tm, dh), lambda t: (jnp.where(t < lat_tiles, t % per_sample, per_sample), 0))
    return pl.pallas_call(
        _rope_kernel,
        grid=(rows // tm,),
        in_specs=[pl.BlockSpec((tm, 2 * MLSTM_WIDTH), lambda t: (t, 0)), tab, tab, tab],
        out_specs=pl.BlockSpec((tm, 2 * MLSTM_WIDTH), lambda t: (t, 0)),
        out_shape=jax.ShapeDtypeStruct((rows, 2 * MLSTM_WIDTH), jnp.bfloat16),
        compiler_params=pltpu.CompilerParams(vmem_limit_bytes=VMEM_LIMIT),
        name="rope_qk",
    )(P, *rope)


def mlstm_bidir(QK, P, GT, gate_b, B, n_lat, n_ctx):
    H, dh = MLSTM_HEADS, MLSTM_HEAD_DIM
    L = MLSTM_STEP_ROWS
    rows = P.shape[0]
    NB, CB = n_lat // L, n_ctx // L

    def blk(b, s, rev):
        sc = (CB - 1 - s) if rev else s
        sl = (NB - 1 - (s - CB)) if rev else (s - CB)
        return jnp.where(s < CB, B * NB + b * CB + sc, b * NB + sl)

    def specs(rev):
        d = 1 if rev else 0
        return [
            pl.BlockSpec((L, dh), lambda b, h, s, gb: (blk(b, s, rev), h)),
            pl.BlockSpec((L, dh), lambda b, h, s, gb: (blk(b, s, rev), H + h)),
            pl.BlockSpec((L, dh), lambda b, h, s, gb: (blk(b, s, rev), 2 * H + h)),
            pl.BlockSpec((1, 1, 8, L), lambda b, h, s, gb: (d, h, 0, blk(b, s, rev))),
        ]

    state = [pltpu.VMEM((dh + 8, dh), jnp.float32), pltpu.VMEM((1, 1), jnp.float32)]
    grid_spec = pltpu.PrefetchScalarGridSpec(
        num_scalar_prefetch=1,
        grid=(B, H, NB + CB),
        in_specs=specs(False) + specs(True),
        out_specs=[pl.BlockSpec((L, dh), lambda b, h, s, gb: (blk(b, s, False), h)),
                   pl.BlockSpec((L, dh), lambda b, h, s, gb: (blk(b, s, True), h))],
        scratch_shapes=state + state,
    )
    return pl.pallas_call(
        _mlstm_kernel,
        grid_spec=grid_spec,
        out_shape=[jax.ShapeDtypeStruct((rows, H * dh), jnp.float32)] * 2,
        name="mlstm_bidir",
    )(gate_b, QK, QK, P, GT, QK, QK, P, GT)


def gates_layout(G):
    rows = G.shape[0]
    g = G.T.reshape(2, 2, MLSTM_HEADS, rows).transpose(0, 2, 1, 3)
    return jnp.pad(g, ((0, 0), (0, 0), (0, 6), (0, 0)))


FFN_TF = 256


def _ffn_kernel(x_ref, wg_ref, wu_ref, wd_ref, gate_ref, o_ref):
    f = pl.program_id(2)
    x = x_ref[0, 0]
    a = jnp.dot(x, wg_ref[0, 0].astype(jnp.bfloat16), preferred_element_type=jnp.float32)
    u = jnp.dot(x, wu_ref[0, 0].astype(jnp.bfloat16), preferred_element_type=jnp.float32)
    act = (a * jax.nn.sigmoid(a) * u).astype(jnp.bfloat16)
    y = jnp.dot(act, wd_ref[0, 0].astype(jnp.bfloat16), preferred_element_type=jnp.float32)

    @pl.when(f == 0)
    def _():
        o_ref[0, 0] = y

    @pl.when(f > 0)
    def _():
        o_ref[0, 0] += y

    @pl.when(f == pl.num_programs(2) - 1)
    def _():
        o_ref[0, 0] *= gate_ref[0, 0]


def expert_ffn(xin, w_gate, w_up, w_down, layer, gate):
    B, E, C, D = xin.shape
    F = w_gate.shape[-1]
    return pl.pallas_call(
        _ffn_kernel,
        grid=(E, B, F // FFN_TF),
        in_specs=[pl.BlockSpec((1, 1, C, D), lambda e, b, f: (b, e, 0, 0)),
                  pl.BlockSpec((1, 1, D, FFN_TF), lambda e, b, f: (layer, e, 0, f)),
                  pl.BlockSpec((1, 1, D, FFN_TF), lambda e, b, f: (layer, e, 0, f)),
                  pl.BlockSpec((1, 1, FFN_TF, D), lambda e, b, f: (layer, e, f, 0)),
                  pl.BlockSpec((1, 1, C, 1), lambda e, b, f: (b, e, 0, 0))],
        out_specs=pl.BlockSpec((1, 1, C, D), lambda e, b, f: (b, e, 0, 0)),
        out_shape=jax.ShapeDtypeStruct((B, E, C, D), jnp.float32),
        compiler_params=pltpu.CompilerParams(vmem_limit_bytes=VMEM_LIMIT),
        name="expert_ffn",
    )(xin, w_gate, w_up, w_down, gate)


ROW_TILE = 256


def _res_norm_kernel(*refs, has_res, has_router):
    it = iter(refs)
    x_ref = next(it)
    f_ref, gate_ref = (next(it), next(it)) if has_res else (None, None)
    g_ref, shift_ref, scale1_ref = next(it), next(it), next(it)
    wr_ref = next(it) if has_router else None
    xo_ref = next(it) if has_res else None
    h_ref = next(it)
    aff_ref = next(it) if has_router else None

    x = x_ref[...]
    if has_res:
        x = x + gate_ref[0] * f_ref[...]
        xo_ref[...] = x
    y = x * lax.rsqrt(jnp.mean(x * x, axis=-1, keepdims=True) + EPS) * g_ref[...]
    h = (y * scale1_ref[0] + shift_ref[0]).astype(h_ref.dtype)
    h_ref[...] = h
    if has_router:
        logits = jnp.dot(h, wr_ref[...], preferred_element_type=jnp.float32)
        e = jnp.exp(logits - jnp.max(logits, axis=-1, keepdims=True))
        aff_ref[...] = e / jnp.sum(e, axis=-1, keepdims=True)


def res_norm(X, g, shift, scale1, n_lat, n_groups, res=None, w_router=None, out_dtype=jnp.bfloat16):
    rows, D = X.shape
    tm = ROW_TILE
    per_sample = n_lat // tm
    grp = lambda t: (jnp.minimum(t // per_sample, n_groups - 1), 0, 0)
    row_spec = pl.BlockSpec((tm, D), lambda t: (t, 0))
    vec_spec = pl.BlockSpec((1, 1, D), grp)
    ins, in_specs = [X], [row_spec]
    if res is not None:
        F, gate = res
        ins += [F, gate]
        in_specs += [row_spec, vec_spec]
    ins += [g.reshape(1, D), shift, scale1]
    in_specs += [pl.BlockSpec((1, D), lambda t: (0, 0)), vec_spec, vec_spec]
    outs, out_specs = [], []
    if res is not None:
        outs.append(jax.ShapeDtypeStruct((rows, D), X.dtype))
        out_specs.append(row_spec)
    outs.append(jax.ShapeDtypeStruct((rows, D), out_dtype))
    out_specs.append(row_spec)
    if w_router is not None:
        E = w_router.shape[1]
        ins.append(w_router)
        in_specs.append(pl.BlockSpec((D, E), lambda t: (0, 0)))
        outs.append(jax.ShapeDtypeStruct((rows, E), jnp.float32))
        out_specs.append(pl.BlockSpec((tm, E), lambda t: (t, 0)))
    return pl.pallas_call(
        functools.partial(_res_norm_kernel, has_res=res is not None, has_router=w_router is not None),
        grid=(rows // tm,),
        in_specs=in_specs,
        out_specs=out_specs,
        out_shape=outs,
        input_output_aliases={0: 0} if res is not None else {},
        compiler_params=pltpu.CompilerParams(vmem_limit_bytes=VMEM_LIMIT),
        name="res_norm",
    )(*ins)


POOL_HALO = 8


def _even_mix_kernel(hf_ref, hb_ref, o_ref, u_ref, up_ref, un_ref, hg_ref, pw_ref, ps_ref, mix_ref, xe_ref,
                     *, n_lat, lat_tiles, n_ctx):
    tm = ROW_TILE
    H, dh = MLSTM_HEADS, MLSTM_HEAD_DIM
    t = pl.program_id(0)
    hh = hf_ref[...] + hb_ref[...]
    for h in range(H):
        sl = slice(h * dh, (h + 1) * dh)
        x = hh[:, sl]
        hn = x * lax.rsqrt(jnp.mean(x * x, axis=-1, keepdims=True) + EPS) * hg_ref[:, sl]
        mix_ref[:, sl] = (hn * jax.nn.sigmoid(o_ref[:, sl])).astype(mix_ref.dtype)

    is_lat = t < lat_tiles
    n_seq = jnp.where(is_lat, n_lat, n_ctx)
    p0 = jnp.where(is_lat, (t % (n_lat // tm)) * tm, ((t - lat_tiles) % (n_ctx // tm)) * tm)
    pos_prev = p0 - POOL_HALO + lax.broadcasted_iota(jnp.int32, (POOL_HALO, 1), 0)
    pos_next = p0 + tm + lax.broadcasted_iota(jnp.int32, (POOL_HALO, 1), 0)
    xe_ref[0:POOL_HALO, :] = jnp.where(pos_prev >= 0, up_ref[...], 0.0)
    xe_ref[POOL_HALO:POOL_HALO + tm, :] = u_ref[...]
    xe_ref[POOL_HALO + tm:, :] = jnp.where(pos_next < n_seq, un_ref[...], 0.0)
    pos = p0 + lax.broadcasted_iota(jnp.int32, (tm, 1), 0)
    for gi, w in enumerate(POOL_WINDOWS):
        cs = slice(gi * POOL_GROUP_DIM, (gi + 1) * POOL_GROUP_DIM)
        acc = xe_ref[POOL_HALO - w // 2:POOL_HALO - w // 2 + tm, cs]
        for d in range(1, w):
            acc = acc + xe_ref[POOL_HALO - w // 2 + d:POOL_HALO - w // 2 + d + tm, cs]
        cnt = (jnp.minimum(pos + w // 2, n_seq) - jnp.maximum(pos - w // 2, 0)).astype(jnp.float32)
        diff = (acc / cnt - u_ref[:, cs]).astype(jnp.bfloat16)
        y = jnp.dot(diff, pw_ref[gi], preferred_element_type=jnp.float32) * ps_ref[:, cs]
        mix_ref[:, MLSTM_WIDTH + gi * POOL_GROUP_DIM:MLSTM_WIDTH + (gi + 1) * POOL_GROUP_DIM] = y.astype(mix_ref.dtype)


def even_mix(hf, hb, P, head_g, pool_w, pool_scale, B, n_lat, n_ctx):
    rows = hf.shape[0]
    tm = ROW_TILE
    hpt = tm // POOL_HALO
    last_halo = rows // POOL_HALO - 1
    wide = pl.BlockSpec((tm, MLSTM_WIDTH), lambda t: (t, 0))
    return pl.pallas_call(
        functools.partial(_even_mix_kernel, n_lat=n_lat, lat_tiles=B * n_lat // tm, n_ctx=n_ctx),
        grid=(rows // tm,),
        in_specs=[wide, wide,
                  pl.BlockSpec((tm, MLSTM_WIDTH), lambda t: (t, 3)),
                  pl.BlockSpec((tm, POOL_WIDTH), lambda t: (t, 4)),
                  pl.BlockSpec((POOL_HALO, POOL_WIDTH), lambda t: (jnp.maximum(t * hpt - 1, 0), 4)),
                  pl.BlockSpec((POOL_HALO, POOL_WIDTH), lambda t: (jnp.minimum((t + 1) * hpt, last_halo), 4)),
                  pl.BlockSpec((1, MLSTM_WIDTH), lambda t: (0, 0)),
                  pl.BlockSpec((POOL_GROUPS, POOL_GROUP_DIM, POOL_GROUP_DIM), lambda t: (0, 0, 0)),
                  pl.BlockSpec((1, POOL_WIDTH), lambda t: (0, 0))],
        out_specs=pl.BlockSpec((tm, MLSTM_WIDTH + POOL_WIDTH), lambda t: (t, 0)),
        out_shape=jax.ShapeDtypeStruct((rows, MLSTM_WIDTH + POOL_WIDTH), jnp.bfloat16),
        scratch_shapes=[pltpu.VMEM((tm + 2 * POOL_HALO, POOL_WIDTH), jnp.float32)],
        compiler_params=pltpu.CompilerParams(vmem_limit_bytes=VMEM_LIMIT),
        name="even_mix",
    )(hf, hb, P, P, P, P, head_g.reshape(1, -1), pool_w.astype(jnp.bfloat16), pool_scale.reshape(1, -1))


def mlstm_pool_mixer(h, w_in, gate_b, head_g, pool_w, pool_scale, w_out, rope, B, n_lat, n_ctx):
    H = MLSTM_HEADS
    W4 = 4 * MLSTM_WIDTH
    w_in = w_in.astype(jnp.bfloat16)
    w_main = jnp.concatenate([w_in[:, :W4], w_in[:, W4 + 4 * H:]], axis=1)
    P = matmul(h, w_main)
    G = matmul(h, w_in[:, W4:W4 + 4 * H])
    hf, hb = mlstm_bidir(rope_qk(P, rope, B, n_lat), P, gates_layout(G), gate_b, B, n_lat, n_ctx)
    mix = even_mix(hf, hb, P, head_g, pool_w, pool_scale, B, n_lat, n_ctx)
    return matmul(mix, w_out.astype(jnp.bfloat16))


def na_mixer(h, w_in, rpb, w_out, B, n_lat, n_ctx):
    qkv = matmul(h, w_in.astype(jnp.bfloat16), jnp.bfloat16)
    return matmul(na_attention(qkv, rpb, B, n_lat, n_ctx), w_out.astype(jnp.bfloat16))


def expert_choice_ffn(h, aff, w_gate, w_up, w_down, layer, B, n_lat, n_ctx):
    E = N_EXPERTS
    rows, D = h.shape
    picks = []
    for lo, n in ((0, n_lat), (B * n_lat, n_ctx)):
        a = aff[lo:lo + B * n].reshape(B, n, E)
        gate, idx = lax.top_k(jnp.swapaxes(a, 1, 2), CAPACITY_FACTOR * n // E)
        picks.append((gate, idx + lo + (jnp.arange(B) * n)[:, None, None]))
    gate = jnp.concatenate([p[0] for p in picks], axis=-1)
    idx = jnp.concatenate([p[1] for p in picks], axis=-1)
    xin = h[idx]
    y = expert_ffn(xin, w_gate, w_up, w_down, layer, gate[..., None])
    return jnp.zeros((rows, D), jnp.float32).at[idx.reshape(-1)].add(y.reshape(-1, D))


def kernel(x, c, ctx, c_ctx, ada_w, ada_b, norm_g, final_g, ev_w_in, ev_gate_b, ev_head_g, ev_pool_w,
           ev_pool_scale, ev_w_out, na_w_in, na_rpb, na_w_out, moe_w_router, moe_w_gate, moe_w_up, moe_w_down):
    B, n_lat, D = x.shape
    n_ctx = ctx.shape[1]
    G = B + 1
    X = jnp.concatenate([x.reshape(B * n_lat, D), ctx.reshape(B * n_ctx, D)], axis=0)
    rope = rope_tables(n_lat)
    cc = jax.nn.silu(jnp.concatenate([c, c_ctx[None]], axis=0)).astype(jnp.bfloat16)
    cc = jnp.pad(cc, ((0, 16 - G), (0, 0)))
    norm = functools.partial(res_norm, n_lat=n_lat, n_groups=G)
    ada = ada_modulation(cc, ada_w, ada_b)[:, :G]
    res = None
    for i in range(DEPTH):
        j = i // 2
        sh, sc, ga, shf, scf, gaf = (v[:, None, :] for v in jnp.split(ada[i], 6, axis=-1))
        if res is None:
            h = norm(X, norm_g[i, 0], sh, 1 + sc)[0]
        else:
            X, h = norm(X, norm_g[i, 0], sh, 1 + sc, res=res)
        if i % 2 == 0:
            o = mlstm_pool_mixer(h, ev_w_in[j], ev_gate_b[j], ev_head_g[j], ev_pool_w[j], ev_pool_scale[j],
                                 ev_w_out[j], rope, B, n_lat, n_ctx)
        else:
            o = na_mixer(h, na_w_in[j], na_rpb[j], na_w_out[j], B, n_lat, n_ctx)
        X, h, aff = norm(X, norm_g[i, 1], shf, 1 + scf, res=(o, ga), w_router=moe_w_router[i].astype(jnp.bfloat16))
        f = expert_choice_ffn(h, aff, moe_w_gate, moe_w_up, moe_w_down, i, B, n_lat, n_ctx)
        res = (f, gaf)
    zeros = jnp.zeros((G, 1, D), jnp.float32)
    _, out = norm(X, final_g, zeros, zeros + 1, res=res, out_dtype=jnp.float32)
    return out[:B * n_lat].reshape(B, n_lat, D)
```

```python
import functools

import jax
import jax.numpy as jnp
import numpy as np
from jax import lax
from jax.experimental import pallas as pl
from jax.experimental.pallas import tpu as pltpu

D_MODEL = 2048
DEPTH = 4
GRID_W = 64
MLSTM_WIDTH = D_MODEL // 2
MLSTM_HEADS = 8
MLSTM_HEAD_DIM = MLSTM_WIDTH // MLSTM_HEADS
MLSTM_CHUNK = 128
POOL_WIDTH = D_MODEL // 2
POOL_WINDOWS = (2, 4, 8, 16)
POOL_GROUPS = len(POOL_WINDOWS)
POOL_GROUP_DIM = POOL_WIDTH // POOL_GROUPS
NA_HEADS = 16
NA_HEAD_DIM = D_MODEL // NA_HEADS
NA_WIDTH = NA_HEADS * NA_HEAD_DIM
NA_ROWS = 8
NA_COLS = 16
RPB_R = 2 * NA_ROWS - 1
RPB_C = 2 * NA_COLS - 1
N_EXPERTS = 16
EXPERT_FF = D_MODEL // 2
CAPACITY_FACTOR = 2
ROPE_BASE = 10000.0
EPS = 1e-6
MASKED = -1e30
VMEM_LIMIT = 48 * 1024 * 1024


def _mm_kernel(a_ref, w_ref, o_ref):
    o_ref[...] = jnp.dot(a_ref[...], w_ref[...], preferred_element_type=jnp.float32).astype(o_ref.dtype)


def matmul(a, w, out_dtype=jnp.float32, tm=512, tn=1024):
    M, K = a.shape
    N = w.shape[1]
    tm, tn = min(tm, M), min(tn, N)
    assert a.dtype == w.dtype == jnp.bfloat16 and M % tm == 0 and N % tn == 0
    return pl.pallas_call(
        _mm_kernel,
        grid=(N // tn, M // tm),
        in_specs=[pl.BlockSpec((tm, K), lambda j, i: (i, 0)),
                  pl.BlockSpec((K, tn), lambda j, i: (0, j))],
        out_specs=pl.BlockSpec((tm, tn), lambda j, i: (i, j)),
        out_shape=jax.ShapeDtypeStruct((M, N), out_dtype),
        compiler_params=pltpu.CompilerParams(vmem_limit_bytes=VMEM_LIMIT),
        name="matmul",
    )(a, w)


def _ada_kernel(c_ref, w_ref, b_ref, o_ref):
    o_ref[0] = jnp.dot(c_ref[...], w_ref[0].astype(jnp.bfloat16), preferred_element_type=jnp.float32) + b_ref[0]


def ada_modulation(cc, ada_w, ada_b, tn=1024):
    M, D = cc.shape
    depth, _, N = ada_w.shape
    return pl.pallas_call(
        _ada_kernel,
        grid=(depth, N // tn),
        in_specs=[pl.BlockSpec((M, D), lambda l, j: (0, 0)),
                  pl.BlockSpec((1, D, tn), lambda l, j: (l, 0, j)),
                  pl.BlockSpec((1, 1, tn), lambda l, j: (l, 0, j))],
        out_specs=pl.BlockSpec((1, M, tn), lambda l, j: (l, 0, j)),
        out_shape=jax.ShapeDtypeStruct((depth, M, N), jnp.float32),
        compiler_params=pltpu.CompilerParams(vmem_limit_bytes=VMEM_LIMIT),
        name="ada_modulation",
    )(cc, ada_w, ada_b.reshape(depth, 1, N))


NA_QROWS = 4
NA_KROWS = 12
NA_STEP_HEADS = 2
LOG2E = 1.4426950408889634


def _toeplitz(v):
    W = GRID_W
    lead = v.shape[:-1]
    y = jnp.concatenate([v, jnp.zeros(lead + (1,), v.dtype)], axis=-1)
    t = jnp.tile(y, (1,) * len(lead) + (W,))[..., : W * (2 * W - 1)].reshape(lead + (W, 2 * W - 1))
    return t[..., W - 1:]


def na_bias_tables(rpb, R):
    H = rpb.shape[0]
    W = GRID_W
    nblk = R // NA_QROWS
    r0 = np.arange(nblk) * NA_QROWS
    ws = np.clip(r0 - NA_ROWS // 2, 0, R - NA_KROWS)
    types, tmap = np.unique(ws - r0, return_inverse=True)
    dc = np.arange(-(W - 1), W)
    in_rpb = np.abs(dc + 0) <= NA_COLS - 1
    cols_idx = np.clip(dc + NA_COLS - 1, 0, RPB_C - 1)
    v = jnp.where(jnp.asarray(in_rpb), rpb[:, :, cols_idx], MASKED)
    toe = _toeplitz(v)
    c = np.arange(W)[:, None]
    kc = np.arange(W)[None, :]
    cs = np.clip(c - NA_COLS // 2, 0, W - NA_COLS)
    col_ok = (kc >= cs) & (kc < cs + NA_COLS)
    toe = jnp.where(jnp.asarray(col_ok), toe, MASKED)
    toe = jnp.concatenate([toe, jnp.full((H, 1, W, W), MASKED, toe.dtype)], axis=1)
    qi = np.arange(NA_QROWS)[:, None]
    kj = np.arange(NA_KROWS)[None, :]
    tabs = []
    for ti in range(len(types)):
        blk = int(np.nonzero(tmap == ti)[0][0])
        r = r0[blk] + qi
        kr = ws[blk] + kj
        rs = np.clip(r - NA_ROWS // 2, 0, R - NA_ROWS)
        row_ok = (kr >= rs) & (kr < rs + NA_ROWS)
        dr = np.where(row_ok, kr - r + NA_ROWS - 1, RPB_R)
        tab = toe[:, dr]
        tabs.append(tab.transpose(0, 1, 3, 2, 4).reshape(H, NA_QROWS * W, NA_KROWS * W))
    tabs.append(jnp.full((H, NA_QROWS * W, NA_KROWS * W), MASKED, jnp.float32))
    return jnp.stack(tabs, axis=1).astype(jnp.float32), tmap.astype(np.int32), ws.astype(np.int32)


def _na_kernel(tmap_ref, ws_ref, q_ref, k_ref, v_ref, kc_ref, vc_ref, bias_ref, o_ref, *, qscale):
    rb = pl.program_id(2)
    start = pl.multiple_of(ws_ref[rb] * GRID_W, GRID_W)
    dh = NA_HEAD_DIM
    tk = NA_KROWS * GRID_W
    dn = (((1,), (1,)), ((), ()))
    ones_w = jnp.ones((tk, dh), jnp.bfloat16)
    ones_c = jnp.ones((kc_ref.shape[0], dh), jnp.bfloat16)
    for hh in range(NA_STEP_HEADS):
        ls = slice(hh * dh, (hh + 1) * dh)
        q = (q_ref[:, ls].astype(jnp.float32) * qscale).astype(jnp.bfloat16)
        kw = k_ref[pl.ds(start, tk), ls]
        vw = jnp.concatenate([v_ref[pl.ds(start, tk), ls], ones_w], axis=1)
        vc = jnp.concatenate([vc_ref[:, ls], ones_c], axis=1)
        s_nb = lax.dot_general(q, kw, dn, preferred_element_type=jnp.float32) + bias_ref[hh, 0]
        s_cx = lax.dot_general(q, kc_ref[:, ls], dn, preferred_element_type=jnp.float32)
        m = jnp.maximum(jnp.max(s_nb, axis=-1, keepdims=True), jnp.max(s_cx, axis=-1, keepdims=True))
        p_nb = jnp.exp2(s_nb - m).astype(jnp.bfloat16)
        p_cx = jnp.exp2(s_cx - m).astype(jnp.bfloat16)
        ol = (jnp.dot(p_nb, vw, preferred_element_type=jnp.float32)
              + jnp.dot(p_cx, vc, preferred_element_type=jnp.float32))
        o_ref[:, ls] = (ol[:, :dh] / ol[:, dh:dh + 1]).astype(o_ref.dtype)


def na_attention(qkv, rpb, B, n_lat, n_ctx):
    rows = qkv.shape[0]
    H, dh = NA_HEADS, NA_HEAD_DIM
    HS = NA_STEP_HEADS
    HG = H // HS
    R = n_lat // GRID_W
    nblk = R // NA_QROWS
    tq = NA_QROWS * GRID_W
    tk = NA_KROWS * GRID_W
    assert n_ctx == tq and n_lat % tq == 0
    bias, tmap, ws = na_bias_tables(rpb, R)
    bias = jnp.maximum(bias * LOG2E, MASKED)
    tmap = np.concatenate([tmap, [bias.shape[1] - 1]]).astype(np.int32)
    ws = np.concatenate([ws, [0]]).astype(np.int32)

    def qblk(b, r):
        return jnp.where(r < nblk, b * nblk + r, B * nblk + b)

    grid_spec = pltpu.PrefetchScalarGridSpec(
        num_scalar_prefetch=2,
        grid=(B, HG, nblk + 1),
        in_specs=[
            pl.BlockSpec((tq, HS * dh), lambda b, h, r, tm, ws: (qblk(b, r), h)),
            pl.BlockSpec((n_lat, HS * dh), lambda b, h, r, tm, ws: (b, HG + h)),
            pl.BlockSpec((n_lat, HS * dh), lambda b, h, r, tm, ws: (b, 2 * HG + h)),
            pl.BlockSpec((n_ctx, HS * dh), lambda b, h, r, tm, ws: (B * nblk + b, HG + h)),
            pl.BlockSpec((n_ctx, HS * dh), lambda b, h, r, tm, ws: (B * nblk + b, 2 * HG + h)),
            pl.BlockSpec((HS, 1, tq, tk), lambda b, h, r, tm, ws: (h, tm[r], 0, 0)),
        ],
        out_specs=pl.BlockSpec((tq, HS * dh), lambda b, h, r, tm, ws: (qblk(b, r), h)),
    )
    return pl.pallas_call(
        functools.partial(_na_kernel, qscale=dh ** -0.5 * LOG2E),
        grid_spec=grid_spec,
        out_shape=jax.ShapeDtypeStruct((rows, H * dh), jnp.bfloat16),
        compiler_params=pltpu.CompilerParams(vmem_limit_bytes=VMEM_LIMIT),
        name="na_attention",
    )(jnp.asarray(tmap), jnp.asarray(ws), qkv, qkv, qkv, qkv, qkv, bias)


MLSTM_STEP_CHUNKS = 2
MLSTM_STEP_ROWS = MLSTM_CHUNK * MLSTM_STEP_CHUNKS


def rope_tables(n):
    dh = MLSTM_HEAD_DIM
    quarter = dh // 4
    t = np.arange(n)
    pos = np.stack([t // GRID_W, t % GRID_W], axis=1).astype(np.float32)
    lane = np.arange(dh)
    inv = jnp.asarray(ROPE_BASE, jnp.float32) ** (-jnp.arange(quarter, dtype=jnp.float32) / quarter)
    ang = jnp.asarray(pos)[:, lane // (dh // 2)] * inv[lane % quarter][None, :]
    cos, sin = jnp.cos(ang), jnp.sin(ang)
    first = jnp.asarray((lane % (dh // 2)) < quarter)[None, :]
    sin_up = jnp.where(first, -sin, 0.0)
    sin_dn = jnp.where(first, 0.0, sin)
    pad = lambda a, v: jnp.concatenate([a, jnp.full((MLSTM_STEP_ROWS, dh), v, jnp.float32)], axis=0)
    return pad(cos, 1.0), pad(sin_up, 0.0), pad(sin_dn, 0.0)


def _log_sigmoid(x):
    return jnp.minimum(x, 0.0) - jnp.log1p(jnp.exp(-jnp.abs(x)))


def _mlstm_chunk(qb, kb, v, gt, b_i, b_f, state, rev):
    L = MLSTM_CHUNK
    dh = MLSTM_HEAD_DIM
    f32, bf16 = jnp.float32, jnp.bfloat16
    Cn, m = state

    row = lax.broadcasted_iota(jnp.int32, gt.shape, 0)
    gl = jnp.where(row == 0, gt + b_i, jnp.where(row == 1, _log_sigmoid(gt + b_f), 0.0))
    glT = gl.T
    li_row, lf_row = gl[0:1], gl[1:2]
    li_col, lf_col = glT[:, 0:1], glT[:, 1:2]
    jj = lax.broadcasted_iota(jnp.int32, (L, L), 0)
    ss = lax.broadcasted_iota(jnp.int32, (L, L), 1)
    tri = (ss >= jj) if rev else (ss <= jj)
    tri_t = (ss <= jj) if rev else (ss >= jj)
    b_col = jnp.sum(jnp.where(tri, lf_row, 0.0), axis=1, keepdims=True)
    b_row = jnp.sum(jnp.where(tri_t, lf_col, 0.0), axis=0, keepdims=True)
    g = jnp.sum(lf_row, axis=1, keepdims=True)

    Dm = jnp.where(tri, b_col - b_row + li_row, MASKED)
    inter = b_col + m
    m_j = jnp.maximum(inter, jnp.max(Dm, axis=1, keepdims=True))
    nt = (((1,), (1,)), ((), ()))
    S = lax.dot_general(qb, kb, nt, preferred_element_type=f32) * jnp.exp(Dm - m_j)
    w_int = jnp.exp(inter - m_j)
    qCn = lax.dot_general(qb, Cn.astype(bf16), nt, preferred_element_type=f32)
    num = jnp.dot(S.astype(bf16), v.astype(bf16), preferred_element_type=f32) + w_int * qCn[:, :dh]
    den = jnp.sum(S, axis=1, keepdims=True) + w_int * qCn[:, dh:dh + 1]
    out = num / jnp.maximum(jnp.abs(den), jnp.exp(-m_j))

    a_row = g - b_row + li_row
    a_col = g - b_col + li_col
    m_loc = jnp.max(a_row, axis=1, keepdims=True)
    wa = jnp.exp(a_col - m_loc)
    C_loc = lax.dot_general((v * wa).astype(bf16), kb, (((0,), (0,)), ((), ())), preferred_element_type=f32)
    n_loc = jnp.sum(kb.astype(f32) * wa, axis=0, keepdims=True)
    upd = jnp.concatenate([C_loc, jnp.broadcast_to(n_loc, (8, dh))], axis=0)
    m_new = jnp.maximum(g + m, m_loc)
    sp = jnp.exp(g + m - m_new)
    sl = jnp.exp(m_loc - m_new)
    return out, (sp * Cn + sl * upd, m_new)


def _mlstm_direction(q_ref, k_ref, v_ref, g_ref, b_i, b_f, state_refs, o_ref, rev):
    L = MLSTM_CHUNK
    state = tuple(r[...] for r in state_refs)
    order = range(MLSTM_STEP_CHUNKS - 1, -1, -1) if rev else range(MLSTM_STEP_CHUNKS)
    for t in order:
        rs = slice(t * L, (t + 1) * L)
        out, state = _mlstm_chunk(q_ref[rs, :], k_ref[rs, :], v_ref[rs, :], g_ref[0, 0, :, rs], b_i, b_f, state, rev)
        o_ref[rs, :] = out
    for r, val in zip(state_refs, state):
        r[...] = val


def _mlstm_kernel(gb_ref, qf, kf, vf, gf, qb, kb, vb, gbk, of, ob, Cf, mf, Cb, mb):
    h = pl.program_id(1)

    @pl.when(pl.program_id(2) == 0)
    def _():
        for r in (Cf, mf, Cb, mb):
            r[...] = jnp.zeros_like(r)

    _mlstm_direction(qf, kf, vf, gf, gb_ref[0, h], gb_ref[1, h], (Cf, mf), of, False)
    _mlstm_direction(qb, kb, vb, gbk, gb_ref[2, h], gb_ref[3, h], (Cb, mb), ob, True)


def _rope_kernel(p_ref, cos_ref, su_ref, sd_ref, o_ref):
    dh = MLSTM_HEAD_DIM
    cos, su, sd = cos_ref[...], su_ref[...], sd_ref[...]
    for hs in range(2 * MLSTM_HEADS):
        ls = slice(hs * dh, (hs + 1) * dh)
        x = p_ref[:, ls] if hs < MLSTM_HEADS else p_ref[:, ls] * (dh ** -0.5)
        y = x * cos + pltpu.roll(x, dh - dh // 4, 1) * su + pltpu.roll(x, dh // 4, 1) * sd
        o_ref[:, ls] = y.astype(o_ref.dtype)


def rope_qk(P, rope, B, n_lat):
    rows = P.shape[0]
    tm = ROW_TILE
    dh = MLSTM_HEAD_DIM
    per_sample, lat_tiles = n_lat // tm, B * n_lat // tm
    tab = pl.BlockSpec((tm, dh), lambda t: (jnp.where(t < lat_tiles, t % per_sample, per_sample), 0))
    return pl.pallas_call(
        _rope_kernel,
        grid=(rows // tm,),
        in_specs=[pl.BlockSpec((tm, 2 * MLSTM_WIDTH), lambda t: (t, 0)), tab, tab, tab],
        out_specs=pl.BlockSpec((tm, 2 * MLSTM_WIDTH), lambda t: (t, 0)),
        out_shape=jax.ShapeDtypeStruct((rows, 2 * MLSTM_WIDTH), jnp.bfloat16),
        compiler_params=pltpu.CompilerParams(vmem_limit_bytes=VMEM_LIMIT),
        name="rope_qk",
    )(P, *rope)


def mlstm_bidir(QK, P, GT, gate_b, B, n_lat, n_ctx):
    H, dh = MLSTM_HEADS, MLSTM_HEAD_DIM
    L = MLSTM_STEP_ROWS
    rows = P.shape[0]
    NB, CB = n_lat // L, n_ctx // L

    def blk(b, s, rev):
        sc = (CB - 1 - s) if rev else s
        sl = (NB - 1 - (s - CB)) if rev else (s - CB)
        return jnp.where(s < CB, B * NB + b * CB + sc, b * NB + sl)

    def specs(rev):
        d = 1 if rev else 0
        return [
            pl.BlockSpec((L, dh), lambda b, h, s, gb: (blk(b, s, rev), h)),
            pl.BlockSpec((L, dh), lambda b, h, s, gb: (blk(b, s, rev), H + h)),
            pl.BlockSpec((L, dh), lambda b, h, s, gb: (blk(b, s, rev), 2 * H + h)),
            pl.BlockSpec((1, 1, 8, L), lambda b, h, s, gb: (d, h, 0, blk(b, s, rev))),
        ]

    state = [pltpu.VMEM((dh + 8, dh), jnp.float32), pltpu.VMEM((1, 1), jnp.float32)]
    grid_spec = pltpu.PrefetchScalarGridSpec(
        num_scalar_prefetch=1,
        grid=(B, H, NB + CB),
        in_specs=specs(False) + specs(True),
        out_specs=[pl.BlockSpec((L, dh), lambda b, h, s, gb: (blk(b, s, False), h)),
                   pl.BlockSpec((L, dh), lambda b, h, s, gb: (blk(b, s, True), h))],
        scratch_shapes=state + state,
    )
    return pl.pallas_call(
        _mlstm_kernel,
        grid_spec=grid_spec,
        out_shape=[jax.ShapeDtypeStruct((rows, H * dh), jnp.float32)] * 2,
        name="mlstm_bidir",
    )(gate_b, QK, QK, P, GT, QK, QK, P, GT)


def gates_layout(G):
    rows = G.shape[0]
    g = G.T.reshape(2, 2, MLSTM_HEADS, rows).transpose(0, 2, 1, 3)
    return jnp.pad(g, ((0, 0), (0, 0), (0, 6), (0, 0)))


FFN_TF = 256


def _ffn_kernel(x_ref, wg_ref, wu_ref, wd_ref, gate_ref, o_ref, acc_ref):
    f = pl.program_id(2)
    x = x_ref[0, 0]
    a = jnp.dot(x, wg_ref[0, 0].astype(jnp.bfloat16), preferred_element_type=jnp.float32)
    u = jnp.dot(x, wu_ref[0, 0].astype(jnp.bfloat16), preferred_element_type=jnp.float32)
    act = (a * jax.nn.sigmoid(a) * u).astype(jnp.bfloat16)
    y = jnp.dot(act, wd_ref[0, 0].astype(jnp.bfloat16), preferred_element_type=jnp.float32)

    @pl.when(f == 0)
    def _():
        acc_ref[...] = y

    @pl.when(f > 0)
    def _():
        acc_ref[...] += y

    @pl.when(f == pl.num_programs(2) - 1)
    def _():
        o_ref[0, 0] = (acc_ref[...] * gate_ref[0, 0]).astype(o_ref.dtype)


def expert_ffn(xin, w_gate, w_up, w_down, layer, gate):
    B, E, C, D = xin.shape
    F = w_gate.shape[-1]
    return pl.pallas_call(
        _ffn_kernel,
        grid=(E, B, F // FFN_TF),
        in_specs=[pl.BlockSpec((1, 1, C, D), lambda e, b, f: (b, e, 0, 0)),
                  pl.BlockSpec((1, 1, D, FFN_TF), lambda e, b, f: (layer, e, 0, f)),
                  pl.BlockSpec((1, 1, D, FFN_TF), lambda e, b, f: (layer, e, 0, f)),
                  pl.BlockSpec((1, 1, FFN_TF, D), lambda e, b, f: (layer, e, f, 0)),
                  pl.BlockSpec((1, 1, C, 1), lambda e, b, f: (b, e, 0, 0))],
        out_specs=pl.BlockSpec((1, 1, C, D), lambda e, b, f: (b, e, 0, 0)),
        out_shape=jax.ShapeDtypeStruct((B, E, C, D), jnp.bfloat16),
        scratch_shapes=[pltpu.VMEM((C, D), jnp.float32)],
        compiler_params=pltpu.CompilerParams(vmem_limit_bytes=VMEM_LIMIT),
        name="expert_ffn",
    )(xin, w_gate, w_up, w_down, gate)


COMBINE_TOKENS = 512
COMBINE_ROWS = 256


def _combine_kernel(tile_ref, blk_ref, first_ref, valid_ref, y_ref, tok_ref, o_ref):
    i = pl.program_id(0)

    @pl.when(first_ref[i] == 1)
    def _():
        o_ref[...] = jnp.zeros_like(o_ref)

    @pl.when(valid_ref[i] == 1)
    def _():
        base = tile_ref[i] * COMBINE_TOKENS
        t = base + lax.broadcasted_iota(jnp.int32, (COMBINE_TOKENS, COMBINE_ROWS), 0)
        onehot = jnp.where(tok_ref[0] == t, 1.0, 0.0).astype(jnp.bfloat16)
        o_ref[...] += jnp.dot(onehot, y_ref[...], preferred_element_type=jnp.float32)


def combine_routed(ys, tok_sorted, rows):
    R, D = ys.shape
    n_tiles, n_blk = rows // COMBINE_TOKENS, R // COMBINE_ROWS
    n_items = n_tiles + n_blk
    bounds = jnp.searchsorted(tok_sorted, jnp.arange(n_tiles + 1, dtype=jnp.int32) * COMBINE_TOKENS).astype(jnp.int32)
    first_blk = jnp.minimum(bounds[:-1] // COMBINE_ROWS, n_blk - 1)
    last_blk = jnp.maximum((bounds[1:] - 1) // COMBINE_ROWS, first_blk)
    start = jnp.cumsum(last_blk - first_blk + 1) - (last_blk - first_blk + 1)
    item = jnp.arange(n_items, dtype=jnp.int32)
    tile = jnp.clip(jnp.searchsorted(start, item, side='right') - 1, 0, n_tiles - 1).astype(jnp.int32)
    blk = first_blk[tile] + item - start[tile]
    valid = (blk <= last_blk[tile]).astype(jnp.int32)
    blk = jnp.minimum(blk, n_blk - 1).astype(jnp.int32)
    first = (item == start[tile]).astype(jnp.int32)
    grid_spec = pltpu.PrefetchScalarGridSpec(
        num_scalar_prefetch=4,
        grid=(n_items,),
        in_specs=[pl.BlockSpec((COMBINE_ROWS, D), lambda i, tile, blk, first, valid: (blk[i], 0)),
                  pl.BlockSpec((1, 1, COMBINE_ROWS), lambda i, tile, blk, first, valid: (blk[i], 0, 0))],
        out_specs=pl.BlockSpec((COMBINE_TOKENS, D), lambda i, tile, blk, first, valid: (tile[i], 0)),
    )
    return pl.pallas_call(
        _combine_kernel,
        grid_spec=grid_spec,
        out_shape=jax.ShapeDtypeStruct((rows, D), jnp.float32),
        compiler_params=pltpu.CompilerParams(vmem_limit_bytes=VMEM_LIMIT),
        name="combine_routed",
    )(tile, blk, first, valid, ys, tok_sorted.reshape(n_blk, 1, COMBINE_ROWS))


ROW_TILE = 256


def _res_norm_kernel(*refs, has_res, has_router):
    it = iter(refs)
    x_ref = next(it)
    f_ref, gate_ref = (next(it), next(it)) if has_res else (None, None)
    g_ref, shift_ref, scale1_ref = next(it), next(it), next(it)
    wr_ref = next(it) if has_router else None
    xo_ref = next(it) if has_res else None
    h_ref = next(it)
    aff_ref = next(it) if has_router else None

    x = x_ref[...]
    if has_res:
        x = x + gate_ref[0] * f_ref[...]
        xo_ref[...] = x
    y = x * lax.rsqrt(jnp.mean(x * x, axis=-1, keepdims=True) + EPS) * g_ref[...]
    h = (y * scale1_ref[0] + shift_ref[0]).astype(h_ref.dtype)
    h_ref[...] = h
    if has_router:
        logits = jnp.dot(h, wr_ref[...], preferred_element_type=jnp.float32)
        e = jnp.exp(logits - jnp.max(logits, axis=-1, keepdims=True))
        aff_ref[...] = e / jnp.sum(e, axis=-1, keepdims=True)


def res_norm(X, g, shift, scale1, n_lat, n_groups, res=None, w_router=None, out_dtype=jnp.bfloat16):
    rows, D = X.shape
    tm = ROW_TILE
    per_sample = n_lat // tm
    grp = lambda t: (jnp.minimum(t // per_sample, n_groups - 1), 0, 0)
    row_spec = pl.BlockSpec((tm, D), lambda t: (t, 0))
    vec_spec = pl.BlockSpec((1, 1, D), grp)
    ins, in_specs = [X], [row_spec]
    if res is not None:
        F, gate = res
        ins += [F, gate]
        in_specs += [row_spec, vec_spec]
    ins += [g.reshape(1, D), shift, scale1]
    in_specs += [pl.BlockSpec((1, D), lambda t: (0, 0)), vec_spec, vec_spec]
    outs, out_specs = [], []
    if res is not None:
        outs.append(jax.ShapeDtypeStruct((rows, D), X.dtype))
        out_specs.append(row_spec)
    outs.append(jax.ShapeDtypeStruct((rows, D), out_dtype))
    out_specs.append(row_spec)
    if w_router is not None:
        E = w_router.shape[1]
        ins.append(w_router)
        in_specs.append(pl.BlockSpec((D, E), lambda t: (0, 0)))
        outs.append(jax.ShapeDtypeStruct((rows, E), jnp.float32))
        out_specs.append(pl.BlockSpec((tm, E), lambda t: (t, 0)))
    return pl.pallas_call(
        functools.partial(_res_norm_kernel, has_res=res is not None, has_router=w_router is not None),
        grid=(rows // tm,),
        in_specs=in_specs,
        out_specs=out_specs,
        out_shape=outs,
        input_output_aliases={0: 0} if res is not None else {},
        compiler_params=pltpu.CompilerParams(vmem_limit_bytes=VMEM_LIMIT),
        name="res_norm",
    )(*ins)


POOL_HALO = 8


def _even_mix_kernel(hf_ref, hb_ref, o_ref, u_ref, up_ref, un_ref, hg_ref, pw_ref, ps_ref, mix_ref, xe_ref,
                     *, n_lat, lat_tiles, n_ctx):
    tm = ROW_TILE
    H, dh = MLSTM_HEADS, MLSTM_HEAD_DIM
    t = pl.program_id(0)
    hh = hf_ref[...] + hb_ref[...]
    for h in range(H):
        sl = slice(h * dh, (h + 1) * dh)
        x = hh[:, sl]
        hn = x * lax.rsqrt(jnp.mean(x * x, axis=-1, keepdims=True) + EPS) * hg_ref[:, sl]
        mix_ref[:, sl] = (hn * jax.nn.sigmoid(o_ref[:, sl])).astype(mix_ref.dtype)

    is_lat = t < lat_tiles
    n_seq = jnp.where(is_lat, n_lat, n_ctx)
    p0 = jnp.where(is_lat, (t % (n_lat // tm)) * tm, ((t - lat_tiles) % (n_ctx // tm)) * tm)
    pos_prev = p0 - POOL_HALO + lax.broadcasted_iota(jnp.int32, (POOL_HALO, 1), 0)
    pos_next = p0 + tm + lax.broadcasted_iota(jnp.int32, (POOL_HALO, 1), 0)
    xe_ref[0:POOL_HALO, :] = jnp.where(pos_prev >= 0, up_ref[...], 0.0)
    xe_ref[POOL_HALO:POOL_HALO + tm, :] = u_ref[...]
    xe_ref[POOL_HALO + tm:, :] = jnp.where(pos_next < n_seq, un_ref[...], 0.0)
    pos = p0 + lax.broadcasted_iota(jnp.int32, (tm, 1), 0)
    for gi, w in enumerate(POOL_WINDOWS):
        cs = slice(gi * POOL_GROUP_DIM, (gi + 1) * POOL_GROUP_DIM)
        acc = xe_ref[POOL_HALO - w // 2:POOL_HALO - w // 2 + tm, cs]
        for d in range(1, w):
            acc = acc + xe_ref[POOL_HALO - w // 2 + d:POOL_HALO - w // 2 + d + tm, cs]
        cnt = (jnp.minimum(pos + w // 2, n_seq) - jnp.maximum(pos - w // 2, 0)).astype(jnp.float32)
        diff = (acc / cnt - u_ref[:, cs]).astype(jnp.bfloat16)
        y = jnp.dot(diff, pw_ref[gi], preferred_element_type=jnp.float32) * ps_ref[:, cs]
        mix_ref[:, MLSTM_WIDTH + gi * POOL_GROUP_DIM:MLSTM_WIDTH + (gi + 1) * POOL_GROUP_DIM] = y.astype(mix_ref.dtype)


def even_mix(hf, hb, P, head_g, pool_w, pool_scale, B, n_lat, n_ctx):
    rows = hf.shape[0]
    tm = ROW_TILE
    hpt = tm // POOL_HALO
    last_halo = rows // POOL_HALO - 1
    wide = pl.BlockSpec((tm, MLSTM_WIDTH), lambda t: (t, 0))
    return pl.pallas_call(
        functools.partial(_even_mix_kernel, n_lat=n_lat, lat_tiles=B * n_lat // tm, n_ctx=n_ctx),
        grid=(rows // tm,),
        in_specs=[wide, wide,
                  pl.BlockSpec((tm, MLSTM_WIDTH), lambda t: (t, 3)),
                  pl.BlockSpec((tm, POOL_WIDTH), lambda t: (t, 4)),
                  pl.BlockSpec((POOL_HALO, POOL_WIDTH), lambda t: (jnp.maximum(t * hpt - 1, 0), 4)),
                  pl.BlockSpec((POOL_HALO, POOL_WIDTH), lambda t: (jnp.minimum((t + 1) * hpt, last_halo), 4)),
                  pl.BlockSpec((1, MLSTM_WIDTH), lambda t: (0, 0)),
                  pl.BlockSpec((POOL_GROUPS, POOL_GROUP_DIM, POOL_GROUP_DIM), lambda t: (0, 0, 0)),
                  pl.BlockSpec((1, POOL_WIDTH), lambda t: (0, 0))],
        out_specs=pl.BlockSpec((tm, MLSTM_WIDTH + POOL_WIDTH), lambda t: (t, 0)),
        out_shape=jax.ShapeDtypeStruct((rows, MLSTM_WIDTH + POOL_WIDTH), jnp.bfloat16),
        scratch_shapes=[pltpu.VMEM((tm + 2 * POOL_HALO, POOL_WIDTH), jnp.float32)],
        compiler_params=pltpu.CompilerParams(vmem_limit_bytes=VMEM_LIMIT),
        name="even_mix",
    )(hf, hb, P, P, P, P, head_g.reshape(1, -1), pool_w.astype(jnp.bfloat16), pool_scale.reshape(1, -1))


def mlstm_pool_mixer(h, w_in, gate_b, head_g, pool_w, pool_scale, w_out, rope, B, n_lat, n_ctx):
    H = MLSTM_HEADS
    W4 = 4 * MLSTM_WIDTH
    w_in = w_in.astype(jnp.bfloat16)
    w_main = jnp.concatenate([w_in[:, :W4], w_in[:, W4 + 4 * H:]], axis=1)
    P = matmul(h, w_main)
    G = matmul(h, w_in[:, W4:W4 + 4 * H])
    hf, hb = mlstm_bidir(rope_qk(P, rope, B, n_lat), P, gates_layout(G), gate_b, B, n_lat, n_ctx)
    mix = even_mix(hf, hb, P, head_g, pool_w, pool_scale, B, n_lat, n_ctx)
    return matmul(mix, w_out.astype(jnp.bfloat16))


def na_mixer(h, w_in, rpb, w_out, B, n_lat, n_ctx):
    qkv = matmul(h, w_in.astype(jnp.bfloat16), jnp.bfloat16)
    return matmul(na_attention(qkv, rpb, B, n_lat, n_ctx), w_out.astype(jnp.bfloat16))


def expert_choice_ffn(h, aff, w_gate, w_up, w_down, layer, B, n_lat, n_ctx):
    E = N_EXPERTS
    rows, D = h.shape
    picks = []
    for lo, n in ((0, n_lat), (B * n_lat, n_ctx)):
        a = aff[lo:lo + B * n].reshape(B, n, E)
        gate, idx = lax.top_k(jnp.swapaxes(a, 1, 2), CAPACITY_FACTOR * n // E)
        picks.append((gate, idx + lo + (jnp.arange(B) * n)[:, None, None]))
    gate = jnp.concatenate([p[0] for p in picks], axis=-1)
    idx = jnp.concatenate([p[1] for p in picks], axis=-1)
    xin = h[idx]
    y = expert_ffn(xin, w_gate, w_up, w_down, layer, gate[..., None])
    tok = idx.reshape(-1)
    order = jnp.argsort(tok)
    return combine_routed(y.reshape(-1, D)[order], tok[order], rows)


def kernel(x, c, ctx, c_ctx, ada_w, ada_b, norm_g, final_g, ev_w_in, ev_gate_b, ev_head_g, ev_pool_w,
           ev_pool_scale, ev_w_out, na_w_in, na_rpb, na_w_out, moe_w_router, moe_w_gate, moe_w_up, moe_w_down):
    B, n_lat, D = x.shape
    n_ctx = ctx.shape[1]
    G = B + 1
    X = jnp.concatenate([x.reshape(B * n_lat, D), ctx.reshape(B * n_ctx, D)], axis=0)
    rope = rope_tables(n_lat)
    cc = jax.nn.silu(jnp.concatenate([c, c_ctx[None]], axis=0)).astype(jnp.bfloat16)
    cc = jnp.pad(cc, ((0, 16 - G), (0, 0)))
    norm = functools.partial(res_norm, n_lat=n_lat, n_groups=G)
    ada = ada_modulation(cc, ada_w, ada_b)[:, :G]
    res = None
    for i in range(DEPTH):
        j = i // 2
        sh, sc, ga, shf, scf, gaf = (v[:, None, :] for v in jnp.split(ada[i], 6, axis=-1))
        if res is None:
            h = norm(X, norm_g[i, 0], sh, 1 + sc)[0]
        else:
            X, h = norm(X, norm_g[i, 0], sh, 1 + sc, res=res)
        if i % 2 == 0:
            o = mlstm_pool_mixer(h, ev_w_in[j], ev_gate_b[j], ev_head_g[j], ev_pool_w[j], ev_pool_scale[j],
                                 ev_w_out[j], rope, B, n_lat, n_ctx)
        else:
            o = na_mixer(h, na_w_in[j], na_rpb[j], na_w_out[j], B, n_lat, n_ctx)
        X, h, aff = norm(X, norm_g[i, 1], shf, 1 + scf, res=(o, ga), w_router=moe_w_router[i].astype(jnp.bfloat16))
        f = expert_choice_ffn(h, aff, moe_w_gate, moe_w_up, moe_w_down, i, B, n_lat, n_ctx)
        res = (f, gaf)
    zeros = jnp.zeros((G, 1, D), jnp.float32)
    _, out = norm(X, final_g, zeros, zeros + 1, res=res, out_dtype=jnp.float32)
    return out[:B * n_lat].reshape(B, n_lat, D)
```

```python
import functools

import jax
import jax.numpy as jnp
import numpy as np
from jax import lax
from jax.experimental import pallas as pl
from jax.experimental.pallas import tpu as pltpu

D_MODEL = 2048
DEPTH = 4
GRID_W = 64
MLSTM_WIDTH = D_MODEL // 2
MLSTM_HEADS = 8
MLSTM_HEAD_DIM = MLSTM_WIDTH // MLSTM_HEADS
MLSTM_CHUNK = 128
POOL_WIDTH = D_MODEL // 2
POOL_WINDOWS = (2, 4, 8, 16)
POOL_GROUPS = len(POOL_WINDOWS)
POOL_GROUP_DIM = POOL_WIDTH // POOL_GROUPS
NA_HEADS = 16
NA_HEAD_DIM = D_MODEL // NA_HEADS
NA_WIDTH = NA_HEADS * NA_HEAD_DIM
NA_ROWS = 8
NA_COLS = 16
RPB_R = 2 * NA_ROWS - 1
RPB_C = 2 * NA_COLS - 1
N_EXPERTS = 16
EXPERT_FF = D_MODEL // 2
CAPACITY_FACTOR = 2
ROPE_BASE = 10000.0
EPS = 1e-6
MASKED = -1e30
VMEM_LIMIT = 48 * 1024 * 1024


def _mm_kernel(a_ref, w_ref, o_ref):
    o_ref[...] = jnp.dot(a_ref[...], w_ref[...], preferred_element_type=jnp.float32).astype(o_ref.dtype)


def matmul(a, w, out_dtype=jnp.float32, tm=512, tn=1024):
    M, K = a.shape
    N = w.shape[1]
    tm, tn = min(tm, M), min(tn, N)
    assert a.dtype == w.dtype == jnp.bfloat16 and M % tm == 0 and N % tn == 0
    return pl.pallas_call(
        _mm_kernel,
        grid=(N // tn, M // tm),
        in_specs=[pl.BlockSpec((tm, K), lambda j, i: (i, 0)),
                  pl.BlockSpec((K, tn), lambda j, i: (0, j))],
        out_specs=pl.BlockSpec((tm, tn), lambda j, i: (i, j)),
        out_shape=jax.ShapeDtypeStruct((M, N), out_dtype),
        compiler_params=pltpu.CompilerParams(vmem_limit_bytes=VMEM_LIMIT),
        name="matmul",
    )(a, w)


def _ada_kernel(c_ref, w_ref, b_ref, o_ref):
    o_ref[0] = jnp.dot(c_ref[...], w_ref[0].astype(jnp.bfloat16), preferred_element_type=jnp.float32) + b_ref[0]


def ada_modulation(cc, ada_w, ada_b, tn=1024):
    M, D = cc.shape
    depth, _, N = ada_w.shape
    return pl.pallas_call(
        _ada_kernel,
        grid=(depth, N // tn),
        in_specs=[pl.BlockSpec((M, D), lambda l, j: (0, 0)),
                  pl.BlockSpec((1, D, tn), lambda l, j: (l, 0, j)),
                  pl.BlockSpec((1, 1, tn), lambda l, j: (l, 0, j))],
        out_specs=pl.BlockSpec((1, M, tn), lambda l, j: (l, 0, j)),
        out_shape=jax.ShapeDtypeStruct((depth, M, N), jnp.float32),
        compiler_params=pltpu.CompilerParams(vmem_limit_bytes=VMEM_LIMIT),
        name="ada_modulation",
    )(cc, ada_w, ada_b.reshape(depth, 1, N))


NA_QROWS = 4
NA_KROWS = 12
NA_STEP_HEADS = 2
LOG2E = 1.4426950408889634


def _toeplitz(v):
    W = GRID_W
    lead = v.shape[:-1]
    y = jnp.concatenate([v, jnp.zeros(lead + (1,), v.dtype)], axis=-1)
    t = jnp.tile(y, (1,) * len(lead) + (W,))[..., : W * (2 * W - 1)].reshape(lead + (W, 2 * W - 1))
    return t[..., W - 1:]


def na_bias_tables(rpb, R):
    H = rpb.shape[0]
    W = GRID_W
    nblk = R // NA_QROWS
    r0 = np.arange(nblk) * NA_QROWS
    ws = np.clip(r0 - NA_ROWS // 2, 0, R - NA_KROWS)
    types, tmap = np.unique(ws - r0, return_inverse=True)
    dc = np.arange(-(W - 1), W)
    in_rpb = np.abs(dc + 0) <= NA_COLS - 1
    cols_idx = np.clip(dc + NA_COLS - 1, 0, RPB_C - 1)
    v = jnp.where(jnp.asarray(in_rpb), rpb[:, :, cols_idx], MASKED)
    toe = _toeplitz(v)
    c = np.arange(W)[:, None]
    kc = np.arange(W)[None, :]
    cs = np.clip(c - NA_COLS // 2, 0, W - NA_COLS)
    col_ok = (kc >= cs) & (kc < cs + NA_COLS)
    toe = jnp.where(jnp.asarray(col_ok), toe, MASKED)
    toe = jnp.concatenate([toe, jnp.full((H, 1, W, W), MASKED, toe.dtype)], axis=1)
    qi = np.arange(NA_QROWS)[:, None]
    kj = np.arange(NA_KROWS)[None, :]
    tabs = []
    for ti in range(len(types)):
        blk = int(np.nonzero(tmap == ti)[0][0])
        r = r0[blk] + qi
        kr = ws[blk] + kj
        rs = np.clip(r - NA_ROWS // 2, 0, R - NA_ROWS)
        row_ok = (kr >= rs) & (kr < rs + NA_ROWS)
        dr = np.where(row_ok, kr - r + NA_ROWS - 1, RPB_R)
        tab = toe[:, dr]
        tabs.append(tab.transpose(0, 1, 3, 2, 4).reshape(H, NA_QROWS * W, NA_KROWS * W))
    tabs.append(jnp.full((H, NA_QROWS * W, NA_KROWS * W), MASKED, jnp.float32))
    return jnp.stack(tabs, axis=1).astype(jnp.float32), tmap.astype(np.int32), ws.astype(np.int32)


def _na_kernel(tmap_ref, ws_ref, q_ref, k_ref, v_ref, kc_ref, vc_ref, bias_ref, o_ref, *, qscale):
    rb = pl.program_id(2)
    start = pl.multiple_of(ws_ref[rb] * GRID_W, GRID_W)
    dh = NA_HEAD_DIM
    tk = NA_KROWS * GRID_W
    dn = (((1,), (1,)), ((), ()))
    ones_w = jnp.ones((tk, dh), jnp.bfloat16)
    ones_c = jnp.ones((kc_ref.shape[0], dh), jnp.bfloat16)
    for hh in range(NA_STEP_HEADS):
        ls = slice(hh * dh, (hh + 1) * dh)
        q = (q_ref[:, ls].astype(jnp.float32) * qscale).astype(jnp.bfloat16)
        kw = k_ref[pl.ds(start, tk), ls]
        vw = jnp.concatenate([v_ref[pl.ds(start, tk), ls], ones_w], axis=1)
        vc = jnp.concatenate([vc_ref[:, ls], ones_c], axis=1)
        s_nb = lax.dot_general(q, kw, dn, preferred_element_type=jnp.float32) + bias_ref[hh, 0]
        s_cx = lax.dot_general(q, kc_ref[:, ls], dn, preferred_element_type=jnp.float32)
        m = jnp.maximum(jnp.max(s_nb, axis=-1, keepdims=True), jnp.max(s_cx, axis=-1, keepdims=True))
        p_nb = jnp.exp2(s_nb - m).astype(jnp.bfloat16)
        p_cx = jnp.exp2(s_cx - m).astype(jnp.bfloat16)
        ol = (jnp.dot(p_nb, vw, preferred_element_type=jnp.float32)
              + jnp.dot(p_cx, vc, preferred_element_type=jnp.float32))
        o_ref[:, ls] = (ol[:, :dh] / ol[:, dh:dh + 1]).astype(o_ref.dtype)


def na_attention(qkv, rpb, B, n_lat, n_ctx):
    rows = qkv.shape[0]
    H, dh = NA_HEADS, NA_HEAD_DIM
    HS = NA_STEP_HEADS
    HG = H // HS
    R = n_lat // GRID_W
    nblk = R // NA_QROWS
    tq = NA_QROWS * GRID_W
    tk = NA_KROWS * GRID_W
    assert n_ctx == tq and n_lat % tq == 0
    bias, tmap, ws = na_bias_tables(rpb, R)
    bias = jnp.maximum(bias * LOG2E, MASKED)
    tmap = np.concatenate([tmap, [bias.shape[1] - 1]]).astype(np.int32)
    ws = np.concatenate([ws, [0]]).astype(np.int32)

    def qblk(b, r):
        return jnp.where(r < nblk, b * nblk + r, B * nblk + b)

    grid_spec = pltpu.PrefetchScalarGridSpec(
        num_scalar_prefetch=2,
        grid=(B, HG, nblk + 1),
        in_specs=[
            pl.BlockSpec((tq, HS * dh), lambda b, h, r, tm, ws: (qblk(b, r), h)),
            pl.BlockSpec((n_lat, HS * dh), lambda b, h, r, tm, ws: (b, HG + h)),
            pl.BlockSpec((n_lat, HS * dh), lambda b, h, r, tm, ws: (b, 2 * HG + h)),
            pl.BlockSpec((n_ctx, HS * dh), lambda b, h, r, tm, ws: (B * nblk + b, HG + h)),
            pl.BlockSpec((n_ctx, HS * dh), lambda b, h, r, tm, ws: (B * nblk + b, 2 * HG + h)),
            pl.BlockSpec((HS, 1, tq, tk), lambda b, h, r, tm, ws: (h, tm[r], 0, 0)),
        ],
        out_specs=pl.BlockSpec((tq, HS * dh), lambda b, h, r, tm, ws: (qblk(b, r), h)),
    )
    return pl.pallas_call(
        functools.partial(_na_kernel, qscale=dh ** -0.5 * LOG2E),
        grid_spec=grid_spec,
        out_shape=jax.ShapeDtypeStruct((rows, H * dh), jnp.bfloat16),
        compiler_params=pltpu.CompilerParams(vmem_limit_bytes=VMEM_LIMIT),
        name="na_attention",
    )(jnp.asarray(tmap), jnp.asarray(ws), qkv, qkv, qkv, qkv, qkv, bias)


MLSTM_STEP_CHUNKS = 2
MLSTM_STEP_ROWS = MLSTM_CHUNK * MLSTM_STEP_CHUNKS


def rope_tables(n):
    dh = MLSTM_HEAD_DIM
    quarter = dh // 4
    t = np.arange(n)
    pos = np.stack([t // GRID_W, t % GRID_W], axis=1).astype(np.float32)
    lane = np.arange(dh)
    inv = jnp.asarray(ROPE_BASE, jnp.float32) ** (-jnp.arange(quarter, dtype=jnp.float32) / quarter)
    ang = jnp.asarray(pos)[:, lane // (dh // 2)] * inv[lane % quarter][None, :]
    cos, sin = jnp.cos(ang), jnp.sin(ang)
    first = jnp.asarray((lane % (dh // 2)) < quarter)[None, :]
    sin_up = jnp.where(first, -sin, 0.0)
    sin_dn = jnp.where(first, 0.0, sin)
    pad = lambda a, v: jnp.concatenate([a, jnp.full((MLSTM_STEP_ROWS, dh), v, jnp.float32)], axis=0)
    return pad(cos, 1.0), pad(sin_up, 0.0), pad(sin_dn, 0.0)


def _log_sigmoid(x):
    return jnp.minimum(x, 0.0) - jnp.log1p(jnp.exp(-jnp.abs(x)))


def _mlstm_chunk(qb, kb, v, gt, b_i, b_f, state, rev):
    L = MLSTM_CHUNK
    dh = MLSTM_HEAD_DIM
    f32, bf16 = jnp.float32, jnp.bfloat16
    Cn, m = state

    row = lax.broadcasted_iota(jnp.int32, gt.shape, 0)
    gl = jnp.where(row == 0, gt + b_i, jnp.where(row == 1, _log_sigmoid(gt + b_f), 0.0))
    glT = gl.T
    li_row, lf_row = gl[0:1], gl[1:2]
    li_col, lf_col = glT[:, 0:1], glT[:, 1:2]
    jj = lax.broadcasted_iota(jnp.int32, (L, L), 0)
    ss = lax.broadcasted_iota(jnp.int32, (L, L), 1)
    tri = (ss >= jj) if rev else (ss <= jj)
    tri_t = (ss <= jj) if rev else (ss >= jj)
    b_col = jnp.sum(jnp.where(tri, lf_row, 0.0), axis=1, keepdims=True)
    b_row = jnp.sum(jnp.where(tri_t, lf_col, 0.0), axis=0, keepdims=True)
    g = jnp.sum(lf_row, axis=1, keepdims=True)

    Dm = jnp.where(tri, b_col - b_row + li_row, MASKED)
    inter = b_col + m
    m_j = jnp.maximum(inter, jnp.max(Dm, axis=1, keepdims=True))
    nt = (((1,), (1,)), ((), ()))
    S = lax.dot_general(qb, kb, nt, preferred_element_type=f32) * jnp.exp(Dm - m_j)
    w_int = jnp.exp(inter - m_j)
    qCn = lax.dot_general(qb, Cn.astype(bf16), nt, preferred_element_type=f32)
    num = jnp.dot(S.astype(bf16), v.astype(bf16), preferred_element_type=f32) + w_int * qCn[:, :dh]
    den = jnp.sum(S, axis=1, keepdims=True) + w_int * qCn[:, dh:dh + 1]
    out = num / jnp.maximum(jnp.abs(den), jnp.exp(-m_j))

    a_row = g - b_row + li_row
    a_col = g - b_col + li_col
    m_loc = jnp.max(a_row, axis=1, keepdims=True)
    wa = jnp.exp(a_col - m_loc)
    C_loc = lax.dot_general((v * wa).astype(bf16), kb, (((0,), (0,)), ((), ())), preferred_element_type=f32)
    n_loc = jnp.sum(kb.astype(f32) * wa, axis=0, keepdims=True)
    upd = jnp.concatenate([C_loc, jnp.broadcast_to(n_loc, (8, dh))], axis=0)
    m_new = jnp.maximum(g + m, m_loc)
    sp = jnp.exp(g + m - m_new)
    sl = jnp.exp(m_loc - m_new)
    return out, (sp * Cn + sl * upd, m_new)


def _mlstm_direction(q_ref, k_ref, v_ref, g_ref, b_i, b_f, state_refs, o_ref, rev):
    L = MLSTM_CHUNK
    state = tuple(r[...] for r in state_refs)
    order = range(MLSTM_STEP_CHUNKS - 1, -1, -1) if rev else range(MLSTM_STEP_CHUNKS)
    for t in order:
        rs = slice(t * L, (t + 1) * L)
        out, state = _mlstm_chunk(q_ref[rs, :], k_ref[rs, :], v_ref[rs, :], g_ref[0, 0, :, rs], b_i, b_f, state, rev)
        o_ref[rs, :] = out
    for r, val in zip(state_refs, state):
        r[...] = val


def _mlstm_kernel(gb_ref, qf, kf, vf, gf, qb, kb, vb, gbk, of, ob, Cf, mf, Cb, mb):
    h = pl.program_id(1)

    @pl.when(pl.program_id(2) == 0)
    def _():
        for r in (Cf, mf, Cb, mb):
            r[...] = jnp.zeros_like(r)

    _mlstm_direction(qf, kf, vf, gf, gb_ref[0, h], gb_ref[1, h], (Cf, mf), of, False)
    _mlstm_direction(qb, kb, vb, gbk, gb_ref[2, h], gb_ref[3, h], (Cb, mb), ob, True)


def _rope_kernel(p_ref, cos_ref, su_ref, sd_ref, o_ref):
    dh = MLSTM_HEAD_DIM
    cos, su, sd = cos_ref[...], su_ref[...], sd_ref[...]
    for hs in range(2 * MLSTM_HEADS):
        ls = slice(hs * dh, (hs + 1) * dh)
        x = p_ref[:, ls] if hs < MLSTM_HEADS else p_ref[:, ls] * (dh ** -0.5)
        y = x * cos + pltpu.roll(x, dh - dh // 4, 1) * su + pltpu.roll(x, dh // 4, 1) * sd
        o_ref[:, ls] = y.astype(o_ref.dtype)


def rope_qk(P, rope, B, n_lat):
    rows = P.shape[0]
    tm = ROW_TILE
    dh = MLSTM_HEAD_DIM
    per_sample, lat_tiles = n_lat // tm, B * n_lat // tm
    tab = pl.BlockSpec((tm, dh), lambda t: (jnp.where(t < lat_tiles, t % per_sample, per_sample), 0))
    return pl.pallas_call(
        _rope_kernel,
        grid=(rows // tm,),
        in_specs=[pl.BlockSpec((tm, 2 * MLSTM_WIDTH), lambda t: (t, 0)), tab, tab, tab],
        out_specs=pl.BlockSpec((tm, 2 * MLSTM_WIDTH), lambda t: (t, 0)),
        out_shape=jax.ShapeDtypeStruct((rows, 2 * MLSTM_WIDTH), jnp.bfloat16),
        compiler_params=pltpu.CompilerParams(vmem_limit_bytes=VMEM_LIMIT),
        name="rope_qk",
    )(P, *rope)


def mlstm_bidir(QK, P, GT, gate_b, B, n_lat, n_ctx):
    H, dh = MLSTM_HEADS, MLSTM_HEAD_DIM
    L = MLSTM_STEP_ROWS
    rows = P.shape[0]
    NB, CB = n_lat // L, n_ctx // L

    def blk(b, s, rev):
        sc = (CB - 1 - s) if rev else s
        sl = (NB - 1 - (s - CB)) if rev else (s - CB)
        return jnp.where(s < CB, B * NB + b * CB + sc, b * NB + sl)

    def specs(rev):
        d = 1 if rev else 0
        return [
            pl.BlockSpec((L, dh), lambda b, h, s, gb: (blk(b, s, rev), h)),
            pl.BlockSpec((L, dh), lambda b, h, s, gb: (blk(b, s, rev), H + h)),
            pl.BlockSpec((L, dh), lambda b, h, s, gb: (blk(b, s, rev), 2 * H + h)),
            pl.BlockSpec((1, 1, 8, L), lambda b, h, s, gb: (d, h, 0, blk(b, s, rev))),
        ]

    state = [pltpu.VMEM((dh + 8, dh), jnp.float32), pltpu.VMEM((1, 1), jnp.float32)]
    grid_spec = pltpu.PrefetchScalarGridSpec(
        num_scalar_prefetch=1,
        grid=(B, H, NB + CB),
        in_specs=specs(False) + specs(True),
        out_specs=[pl.BlockSpec((L, dh), lambda b, h, s, gb: (blk(b, s, False), h)),
                   pl.BlockSpec((L, dh), lambda b, h, s, gb: (blk(b, s, True), h))],
        scratch_shapes=state + state,
    )
    return pl.pallas_call(
        _mlstm_kernel,
        grid_spec=grid_spec,
        out_shape=[jax.ShapeDtypeStruct((rows, H * dh), jnp.float32)] * 2,
        name="mlstm_bidir",
    )(gate_b, QK, QK, P, GT, QK, QK, P, GT)


def gates_layout(G):
    rows = G.shape[0]
    g = G.T.reshape(2, 2, MLSTM_HEADS, rows).transpose(0, 2, 1, 3)
    return jnp.pad(g, ((0, 0), (0, 0), (0, 6), (0, 0)))


FFN_TF = 256


def _ffn_kernel(x_ref, wg_ref, wu_ref, wd_ref, gate_ref, o_ref, acc_ref):
    f = pl.program_id(2)
    x = x_ref[0, 0]
    a = jnp.dot(x, wg_ref[0, 0].astype(jnp.bfloat16), preferred_element_type=jnp.float32)
    u = jnp.dot(x, wu_ref[0, 0].astype(jnp.bfloat16), preferred_element_type=jnp.float32)
    act = (a * jax.nn.sigmoid(a) * u).astype(jnp.bfloat16)
    y = jnp.dot(act, wd_ref[0, 0].astype(jnp.bfloat16), preferred_element_type=jnp.float32)

    @pl.when(f == 0)
    def _():
        acc_ref[...] = y

    @pl.when(f > 0)
    def _():
        acc_ref[...] += y

    @pl.when(f == pl.num_programs(2) - 1)
    def _():
        o_ref[0, 0] = (acc_ref[...] * gate_ref[0, 0]).astype(o_ref.dtype)


def expert_ffn(xin, w_gate, w_up, w_down, layer, gate):
    B, E, C, D = xin.shape
    F = w_gate.shape[-1]
    return pl.pallas_call(
        _ffn_kernel,
        grid=(E, B, F // FFN_TF),
        in_specs=[pl.BlockSpec((1, 1, C, D), lambda e, b, f: (b, e, 0, 0)),
                  pl.BlockSpec((1, 1, D, FFN_TF), lambda e, b, f: (layer, e, 0, f)),
                  pl.BlockSpec((1, 1, D, FFN_TF), lambda e, b, f: (layer, e, 0, f)),
                  pl.BlockSpec((1, 1, FFN_TF, D), lambda e, b, f: (layer, e, f, 0)),
                  pl.BlockSpec((1, 1, C, 1), lambda e, b, f: (b, e, 0, 0))],
        out_specs=pl.BlockSpec((1, 1, C, D), lambda e, b, f: (b, e, 0, 0)),
        out_shape=jax.ShapeDtypeStruct((B, E, C, D), jnp.bfloat16),
        scratch_shapes=[pltpu.VMEM((C, D), jnp.float32)],
        compiler_params=pltpu.CompilerParams(vmem_limit_bytes=VMEM_LIMIT),
        name="expert_ffn",
    )(xin, w_gate, w_up, w_down, gate)


COMBINE_TOKENS = 512
COMBINE_ROWS = 256


def _combine_kernel(tile_ref, blk_ref, first_ref, valid_ref, y_ref, tok_ref, o_ref):
    i = pl.program_id(0)

    @pl.when(first_ref[i] == 1)
    def _():
        o_ref[...] = jnp.zeros_like(o_ref)

    @pl.when(valid_ref[i] == 1)
    def _():
        base = tile_ref[i] * COMBINE_TOKENS
        t = base + lax.broadcasted_iota(jnp.int32, (COMBINE_TOKENS, COMBINE_ROWS), 0)
        onehot = jnp.where(tok_ref[0] == t, 1.0, 0.0).astype(jnp.bfloat16)
        o_ref[...] += jnp.dot(onehot, y_ref[...], preferred_element_type=jnp.float32)


def combine_routed(ys, tok_sorted, rows):
    R, D = ys.shape
    n_tiles, n_blk = rows // COMBINE_TOKENS, R // COMBINE_ROWS
    n_items = n_tiles + n_blk
    edges = jnp.arange(n_tiles + 1, dtype=jnp.int32) * COMBINE_TOKENS
    bounds = jnp.sum((tok_sorted[None, :] < edges[:, None]).astype(jnp.int32), axis=1)
    first_blk = jnp.minimum(bounds[:-1] // COMBINE_ROWS, n_blk - 1)
    last_blk = jnp.maximum((bounds[1:] - 1) // COMBINE_ROWS, first_blk)
    start = jnp.cumsum(last_blk - first_blk + 1) - (last_blk - first_blk + 1)
    item = jnp.arange(n_items, dtype=jnp.int32)
    tile = jnp.sum((start[None, :] <= item[:, None]).astype(jnp.int32), axis=1) - 1
    blk = first_blk[tile] + item - start[tile]
    valid = (blk <= last_blk[tile]).astype(jnp.int32)
    blk = jnp.minimum(blk, n_blk - 1).astype(jnp.int32)
    first = (item == start[tile]).astype(jnp.int32)
    grid_spec = pltpu.PrefetchScalarGridSpec(
        num_scalar_prefetch=4,
        grid=(n_items,),
        in_specs=[pl.BlockSpec((COMBINE_ROWS, D), lambda i, tile, blk, first, valid: (blk[i], 0)),
                  pl.BlockSpec((1, 1, COMBINE_ROWS), lambda i, tile, blk, first, valid: (blk[i], 0, 0))],
        out_specs=pl.BlockSpec((COMBINE_TOKENS, D), lambda i, tile, blk, first, valid: (tile[i], 0)),
    )
    return pl.pallas_call(
        _combine_kernel,
        grid_spec=grid_spec,
        out_shape=jax.ShapeDtypeStruct((rows, D), jnp.float32),
        compiler_params=pltpu.CompilerParams(vmem_limit_bytes=VMEM_LIMIT),
        name="combine_routed",
    )(tile, blk, first, valid, ys, tok_sorted.reshape(n_blk, 1, COMBINE_ROWS))


ROW_TILE = 256


def _res_norm_kernel(*refs, has_res, has_proj, has_router):
    it = iter(refs)
    x_ref = next(it)
    f_ref = next(it) if has_res else None
    w_ref = next(it) if has_proj else None
    gate_ref = next(it) if has_res else None
    g_ref, shift_ref, scale1_ref = next(it), next(it), next(it)
    wr_ref = next(it) if has_router else None
    xo_ref = next(it) if has_res else None
    h_ref = next(it)
    aff_ref = next(it) if has_router else None

    x = x_ref[...]
    if has_res:
        f = jnp.dot(f_ref[...], w_ref[...], preferred_element_type=jnp.float32) if has_proj else f_ref[...]
        x = x + gate_ref[0] * f
        xo_ref[...] = x
    y = x * lax.rsqrt(jnp.mean(x * x, axis=-1, keepdims=True) + EPS) * g_ref[...]
    h = (y * scale1_ref[0] + shift_ref[0]).astype(h_ref.dtype)
    h_ref[...] = h
    if has_router:
        logits = jnp.dot(h, wr_ref[...], preferred_element_type=jnp.float32)
        e = jnp.exp(logits - jnp.max(logits, axis=-1, keepdims=True))
        aff_ref[...] = e / jnp.sum(e, axis=-1, keepdims=True)


def res_norm(X, g, shift, scale1, n_lat, n_groups, res=None, w_router=None, out_dtype=jnp.bfloat16):
    rows, D = X.shape
    tm = ROW_TILE
    per_sample = n_lat // tm
    grp = lambda t: (jnp.minimum(t // per_sample, n_groups - 1), 0, 0)
    row_spec = pl.BlockSpec((tm, D), lambda t: (t, 0))
    vec_spec = pl.BlockSpec((1, 1, D), grp)
    ins, in_specs = [X], [row_spec]
    has_proj = res is not None and isinstance(res[0], tuple)
    if has_proj:
        (A, W), gate = res
        ins += [A, W, gate]
        in_specs += [pl.BlockSpec((tm, A.shape[1]), lambda t: (t, 0)), pl.BlockSpec(W.shape, lambda t: (0, 0)), vec_spec]
    elif res is not None:
        F, gate = res
        ins += [F, gate]
        in_specs += [row_spec, vec_spec]
    ins += [g.reshape(1, D), shift, scale1]
    in_specs += [pl.BlockSpec((1, D), lambda t: (0, 0)), vec_spec, vec_spec]
    outs, out_specs = [], []
    if res is not None:
        outs.append(jax.ShapeDtypeStruct((rows, D), X.dtype))
        out_specs.append(row_spec)
    outs.append(jax.ShapeDtypeStruct((rows, D), out_dtype))
    out_specs.append(row_spec)
    if w_router is not None:
        E = w_router.shape[1]
        ins.append(w_router)
        in_specs.append(pl.BlockSpec((D, E), lambda t: (0, 0)))
        outs.append(jax.ShapeDtypeStruct((rows, E), jnp.float32))
        out_specs.append(pl.BlockSpec((tm, E), lambda t: (t, 0)))
    return pl.pallas_call(
        functools.partial(_res_norm_kernel, has_res=res is not None, has_proj=has_proj,
                          has_router=w_router is not None),
        grid=(rows // tm,),
        in_specs=in_specs,
        out_specs=out_specs,
        out_shape=outs,
        input_output_aliases={0: 0} if res is not None else {},
        compiler_params=pltpu.CompilerParams(vmem_limit_bytes=VMEM_LIMIT),
        name="res_norm",
    )(*ins)


POOL_HALO = 8


def _even_mix_kernel(hf_ref, hb_ref, o_ref, u_ref, up_ref, un_ref, hg_ref, pw_ref, ps_ref, mix_ref, xe_ref,
                     *, n_lat, lat_tiles, n_ctx):
    tm = ROW_TILE
    H, dh = MLSTM_HEADS, MLSTM_HEAD_DIM
    t = pl.program_id(0)
    hh = hf_ref[...] + hb_ref[...]
    for h in range(H):
        sl = slice(h * dh, (h + 1) * dh)
        x = hh[:, sl]
        hn = x * lax.rsqrt(jnp.mean(x * x, axis=-1, keepdims=True) + EPS) * hg_ref[:, sl]
        mix_ref[:, sl] = (hn * jax.nn.sigmoid(o_ref[:, sl])).astype(mix_ref.dtype)

    is_lat = t < lat_tiles
    n_seq = jnp.where(is_lat, n_lat, n_ctx)
    p0 = jnp.where(is_lat, (t % (n_lat // tm)) * tm, ((t - lat_tiles) % (n_ctx // tm)) * tm)
    pos_prev = p0 - POOL_HALO + lax.broadcasted_iota(jnp.int32, (POOL_HALO, 1), 0)
    pos_next = p0 + tm + lax.broadcasted_iota(jnp.int32, (POOL_HALO, 1), 0)
    xe_ref[0:POOL_HALO, :] = jnp.where(pos_prev >= 0, up_ref[...], 0.0)
    xe_ref[POOL_HALO:POOL_HALO + tm, :] = u_ref[...]
    xe_ref[POOL_HALO + tm:, :] = jnp.where(pos_next < n_seq, un_ref[...], 0.0)
    pos = p0 + lax.broadcasted_iota(jnp.int32, (tm, 1), 0)
    for gi, w in enumerate(POOL_WINDOWS):
        cs = slice(gi * POOL_GROUP_DIM, (gi + 1) * POOL_GROUP_DIM)
        acc = xe_ref[POOL_HALO - w // 2:POOL_HALO - w // 2 + tm, cs]
        for d in range(1, w):
            acc = acc + xe_ref[POOL_HALO - w // 2 + d:POOL_HALO - w // 2 + d + tm, cs]
        cnt = (jnp.minimum(pos + w // 2, n_seq) - jnp.maximum(pos - w // 2, 0)).astype(jnp.float32)
        diff = (acc / cnt - u_ref[:, cs]).astype(jnp.bfloat16)
        y = jnp.dot(diff, pw_ref[gi], preferred_element_type=jnp.float32) * ps_ref[:, cs]
        mix_ref[:, MLSTM_WIDTH + gi * POOL_GROUP_DIM:MLSTM_WIDTH + (gi + 1) * POOL_GROUP_DIM] = y.astype(mix_ref.dtype)


def even_mix(hf, hb, P, head_g, pool_w, pool_scale, B, n_lat, n_ctx):
    rows = hf.shape[0]
    tm = ROW_TILE
    hpt = tm // POOL_HALO
    last_halo = rows // POOL_HALO - 1
    wide = pl.BlockSpec((tm, MLSTM_WIDTH), lambda t: (t, 0))
    return pl.pallas_call(
        functools.partial(_even_mix_kernel, n_lat=n_lat, lat_tiles=B * n_lat // tm, n_ctx=n_ctx),
        grid=(rows // tm,),
        in_specs=[wide, wide,
                  pl.BlockSpec((tm, MLSTM_WIDTH), lambda t: (t, 3)),
                  pl.BlockSpec((tm, POOL_WIDTH), lambda t: (t, 4)),
                  pl.BlockSpec((POOL_HALO, POOL_WIDTH), lambda t: (jnp.maximum(t * hpt - 1, 0), 4)),
                  pl.BlockSpec((POOL_HALO, POOL_WIDTH), lambda t: (jnp.minimum((t + 1) * hpt, last_halo), 4)),
                  pl.BlockSpec((1, MLSTM_WIDTH), lambda t: (0, 0)),
                  pl.BlockSpec((POOL_GROUPS, POOL_GROUP_DIM, POOL_GROUP_DIM), lambda t: (0, 0, 0)),
                  pl.BlockSpec((1, POOL_WIDTH), lambda t: (0, 0))],
        out_specs=pl.BlockSpec((tm, MLSTM_WIDTH + POOL_WIDTH), lambda t: (t, 0)),
        out_shape=jax.ShapeDtypeStruct((rows, MLSTM_WIDTH + POOL_WIDTH), jnp.bfloat16),
        scratch_shapes=[pltpu.VMEM((tm + 2 * POOL_HALO, POOL_WIDTH), jnp.float32)],
        compiler_params=pltpu.CompilerParams(vmem_limit_bytes=VMEM_LIMIT),
        name="even_mix",
    )(hf, hb, P, P, P, P, head_g.reshape(1, -1), pool_w.astype(jnp.bfloat16), pool_scale.reshape(1, -1))


def mlstm_pool_mixer(h, w_in, gate_b, head_g, pool_w, pool_scale, w_out, rope, B, n_lat, n_ctx):
    H = MLSTM_HEADS
    W4 = 4 * MLSTM_WIDTH
    w_in = w_in.astype(jnp.bfloat16)
    w_main = jnp.concatenate([w_in[:, :W4], w_in[:, W4 + 4 * H:]], axis=1)
    P = matmul(h, w_main)
    G = matmul(h, w_in[:, W4:W4 + 4 * H])
    hf, hb = mlstm_bidir(rope_qk(P, rope, B, n_lat), P, gates_layout(G), gate_b, B, n_lat, n_ctx)
    mix = even_mix(hf, hb, P, head_g, pool_w, pool_scale, B, n_lat, n_ctx)
    return mix, w_out.astype(jnp.bfloat16)


def na_mixer(h, w_in, rpb, w_out, B, n_lat, n_ctx):
    qkv = matmul(h, w_in.astype(jnp.bfloat16), jnp.bfloat16)
    return na_attention(qkv, rpb, B, n_lat, n_ctx), w_out.astype(jnp.bfloat16)


def expert_choice_ffn(h, aff, w_gate, w_up, w_down, layer, B, n_lat, n_ctx):
    E = N_EXPERTS
    rows, D = h.shape
    picks = []
    for lo, n in ((0, n_lat), (B * n_lat, n_ctx)):
        a = aff[lo:lo + B * n].reshape(B, n, E)
        gate, idx = lax.top_k(jnp.swapaxes(a, 1, 2), CAPACITY_FACTOR * n // E)
        picks.append((gate, idx + lo + (jnp.arange(B) * n)[:, None, None]))
    gate = jnp.concatenate([p[0] for p in picks], axis=-1)
    idx = jnp.concatenate([p[1] for p in picks], axis=-1)
    xin = h[idx]
    y = expert_ffn(xin, w_gate, w_up, w_down, layer, gate[..., None])
    tok = idx.reshape(-1)
    order = jnp.argsort(tok)
    return combine_routed(y.reshape(-1, D)[order], tok[order], rows)


def kernel(x, c, ctx, c_ctx, ada_w, ada_b, norm_g, final_g, ev_w_in, ev_gate_b, ev_head_g, ev_pool_w,
           ev_pool_scale, ev_w_out, na_w_in, na_rpb, na_w_out, moe_w_router, moe_w_gate, moe_w_up, moe_w_down):
    B, n_lat, D = x.shape
    n_ctx = ctx.shape[1]
    G = B + 1
    X = jnp.concatenate([x.reshape(B * n_lat, D), ctx.reshape(B * n_ctx, D)], axis=0)
    rope = rope_tables(n_lat)
    cc = jax.nn.silu(jnp.concatenate([c, c_ctx[None]], axis=0)).astype(jnp.bfloat16)
    cc = jnp.pad(cc, ((0, 16 - G), (0, 0)))
    norm = functools.partial(res_norm, n_lat=n_lat, n_groups=G)
    ada = ada_modulation(cc, ada_w, ada_b)[:, :G]
    res = None
    for i in range(DEPTH):
        j = i // 2
        sh, sc, ga, shf, scf, gaf = (v[:, None, :] for v in jnp.split(ada[i], 6, axis=-1))
        if res is None:
            h = norm(X, norm_g[i, 0], sh, 1 + sc)[0]
        else:
            X, h = norm(X, norm_g[i, 0], sh, 1 + sc, res=res)
        if i % 2 == 0:
            o = mlstm_pool_mixer(h, ev_w_in[j], ev_gate_b[j], ev_head_g[j], ev_pool_w[j], ev_pool_scale[j],
                                 ev_w_out[j], rope, B, n_lat, n_ctx)
        else:
            o = na_mixer(h, na_w_in[j], na_rpb[j], na_w_out[j], B, n_lat, n_ctx)
        X, h, aff = norm(X, norm_g[i, 1], shf, 1 + scf, res=(o, ga), w_router=moe_w_router[i].astype(jnp.bfloat16))
        f = expert_choice_ffn(h, aff, moe_w_gate, moe_w_up, moe_w_down, i, B, n_lat, n_ctx)
        res = (f, gaf)
    zeros = jnp.zeros((G, 1, D), jnp.float32)
    _, out = norm(X, final_g, zeros, zeros + 1, res=res, out_dtype=jnp.float32)
    return out[:B * n_lat].reshape(B, n_lat, D)
```

```python
import functools

import jax
import jax.numpy as jnp
import numpy as np
from jax import lax
from jax.experimental import pallas as pl
from jax.experimental.pallas import tpu as pltpu

D_MODEL = 2048
DEPTH = 4
GRID_W = 64
MLSTM_WIDTH = D_MODEL // 2
MLSTM_HEADS = 8
MLSTM_HEAD_DIM = MLSTM_WIDTH // MLSTM_HEADS
MLSTM_CHUNK = 256
POOL_WIDTH = D_MODEL // 2
POOL_WINDOWS = (2, 4, 8, 16)
POOL_GROUPS = len(POOL_WINDOWS)
POOL_GROUP_DIM = POOL_WIDTH // POOL_GROUPS
NA_HEADS = 16
NA_HEAD_DIM = D_MODEL // NA_HEADS
NA_WIDTH = NA_HEADS * NA_HEAD_DIM
NA_ROWS = 8
NA_COLS = 16
RPB_R = 2 * NA_ROWS - 1
RPB_C = 2 * NA_COLS - 1
N_EXPERTS = 16
EXPERT_FF = D_MODEL // 2
CAPACITY_FACTOR = 2
ROPE_BASE = 10000.0
EPS = 1e-6
MASKED = -1e30
VMEM_LIMIT = 48 * 1024 * 1024


def _mm_kernel(a_ref, w_ref, o_ref):
    o_ref[...] = jnp.dot(a_ref[...], w_ref[...], preferred_element_type=jnp.float32).astype(o_ref.dtype)


def matmul(a, w, out_dtype=jnp.float32, tm=512, tn=1024):
    M, K = a.shape
    N = w.shape[1]
    tm, tn = min(tm, M), min(tn, N)
    assert a.dtype == w.dtype == jnp.bfloat16 and M % tm == 0 and N % tn == 0
    return pl.pallas_call(
        _mm_kernel,
        grid=(N // tn, M // tm),
        in_specs=[pl.BlockSpec((tm, K), lambda j, i: (i, 0)),
                  pl.BlockSpec((K, tn), lambda j, i: (0, j))],
        out_specs=pl.BlockSpec((tm, tn), lambda j, i: (i, j)),
        out_shape=jax.ShapeDtypeStruct((M, N), out_dtype),
        compiler_params=pltpu.CompilerParams(vmem_limit_bytes=VMEM_LIMIT),
        name="matmul",
    )(a, w)


def _ada_kernel(c_ref, w_ref, b_ref, o_ref):
    o_ref[0] = jnp.dot(c_ref[...], w_ref[0].astype(jnp.bfloat16), preferred_element_type=jnp.float32) + b_ref[0]


def ada_modulation(cc, ada_w, ada_b, tn=1024):
    M, D = cc.shape
    depth, _, N = ada_w.shape
    return pl.pallas_call(
        _ada_kernel,
        grid=(depth, N // tn),
        in_specs=[pl.BlockSpec((M, D), lambda l, j: (0, 0)),
                  pl.BlockSpec((1, D, tn), lambda l, j: (l, 0, j)),
                  pl.BlockSpec((1, 1, tn), lambda l, j: (l, 0, j))],
        out_specs=pl.BlockSpec((1, M, tn), lambda l, j: (l, 0, j)),
        out_shape=jax.ShapeDtypeStruct((depth, M, N), jnp.float32),
        compiler_params=pltpu.CompilerParams(vmem_limit_bytes=VMEM_LIMIT),
        name="ada_modulation",
    )(cc, ada_w, ada_b.reshape(depth, 1, N))


NA_QROWS = 4
NA_KROWS = 12
NA_STEP_HEADS = 2
LOG2E = 1.4426950408889634


def _toeplitz(v):
    W = GRID_W
    lead = v.shape[:-1]
    y = jnp.concatenate([v, jnp.zeros(lead + (1,), v.dtype)], axis=-1)
    t = jnp.tile(y, (1,) * len(lead) + (W,))[..., : W * (2 * W - 1)].reshape(lead + (W, 2 * W - 1))
    return t[..., W - 1:]


def na_bias_tables(rpb, R):
    H = rpb.shape[0]
    W = GRID_W
    nblk = R // NA_QROWS
    r0 = np.arange(nblk) * NA_QROWS
    ws = np.clip(r0 - NA_ROWS // 2, 0, R - NA_KROWS)
    types, tmap = np.unique(ws - r0, return_inverse=True)
    dc = np.arange(-(W - 1), W)
    in_rpb = np.abs(dc + 0) <= NA_COLS - 1
    cols_idx = np.clip(dc + NA_COLS - 1, 0, RPB_C - 1)
    v = jnp.where(jnp.asarray(in_rpb), rpb[:, :, cols_idx], MASKED)
    toe = _toeplitz(v)
    c = np.arange(W)[:, None]
    kc = np.arange(W)[None, :]
    cs = np.clip(c - NA_COLS // 2, 0, W - NA_COLS)
    col_ok = (kc >= cs) & (kc < cs + NA_COLS)
    toe = jnp.where(jnp.asarray(col_ok), toe, MASKED)
    toe = jnp.concatenate([toe, jnp.full((H, 1, W, W), MASKED, toe.dtype)], axis=1)
    qi = np.arange(NA_QROWS)[:, None]
    kj = np.arange(NA_KROWS)[None, :]
    tabs = []
    for ti in range(len(types)):
        blk = int(np.nonzero(tmap == ti)[0][0])
        r = r0[blk] + qi
        kr = ws[blk] + kj
        rs = np.clip(r - NA_ROWS // 2, 0, R - NA_ROWS)
        row_ok = (kr >= rs) & (kr < rs + NA_ROWS)
        dr = np.where(row_ok, kr - r + NA_ROWS - 1, RPB_R)
        tab = toe[:, dr]
        tabs.append(tab.transpose(0, 1, 3, 2, 4).reshape(H, NA_QROWS * W, NA_KROWS * W))
    tabs.append(jnp.full((H, NA_QROWS * W, NA_KROWS * W), MASKED, jnp.float32))
    return jnp.stack(tabs, axis=1).astype(jnp.float32), tmap.astype(np.int32), ws.astype(np.int32)


def _na_kernel(tmap_ref, ws_ref, q_ref, k_ref, v_ref, kc_ref, vc_ref, bias_ref, o_ref, *, qscale):
    rb = pl.program_id(2)
    start = pl.multiple_of(ws_ref[rb] * GRID_W, GRID_W)
    dh = NA_HEAD_DIM
    tk = NA_KROWS * GRID_W
    dn = (((1,), (1,)), ((), ()))
    ones_w = jnp.ones((tk, dh), jnp.bfloat16)
    ones_c = jnp.ones((kc_ref.shape[0], dh), jnp.bfloat16)
    for hh in range(NA_STEP_HEADS):
        ls = slice(hh * dh, (hh + 1) * dh)
        q = (q_ref[:, ls].astype(jnp.float32) * qscale).astype(jnp.bfloat16)
        kw = k_ref[pl.ds(start, tk), ls]
        vw = jnp.concatenate([v_ref[pl.ds(start, tk), ls], ones_w], axis=1)
        vc = jnp.concatenate([vc_ref[:, ls], ones_c], axis=1)
        s_nb = lax.dot_general(q, kw, dn, preferred_element_type=jnp.float32) + bias_ref[hh, 0]
        s_cx = lax.dot_general(q, kc_ref[:, ls], dn, preferred_element_type=jnp.float32)
        m = jnp.maximum(jnp.max(s_nb, axis=-1, keepdims=True), jnp.max(s_cx, axis=-1, keepdims=True))
        p_nb = jnp.exp2(s_nb - m).astype(jnp.bfloat16)
        p_cx = jnp.exp2(s_cx - m).astype(jnp.bfloat16)
        ol = (jnp.dot(p_nb, vw, preferred_element_type=jnp.float32)
              + jnp.dot(p_cx, vc, preferred_element_type=jnp.float32))
        o_ref[:, ls] = (ol[:, :dh] / ol[:, dh:dh + 1]).astype(o_ref.dtype)


def na_attention(qkv, rpb, B, n_lat, n_ctx):
    rows = qkv.shape[0]
    H, dh = NA_HEADS, NA_HEAD_DIM
    HS = NA_STEP_HEADS
    HG = H // HS
    R = n_lat // GRID_W
    nblk = R // NA_QROWS
    tq = NA_QROWS * GRID_W
    tk = NA_KROWS * GRID_W
    assert n_ctx == tq and n_lat % tq == 0
    bias, tmap, ws = na_bias_tables(rpb, R)
    bias = jnp.maximum(bias * LOG2E, MASKED)
    tmap = np.concatenate([tmap, [bias.shape[1] - 1]]).astype(np.int32)
    ws = np.concatenate([ws, [0]]).astype(np.int32)

    def qblk(b, r):
        return jnp.where(r < nblk, b * nblk + r, B * nblk + b)

    grid_spec = pltpu.PrefetchScalarGridSpec(
        num_scalar_prefetch=2,
        grid=(B, HG, nblk + 1),
        in_specs=[
            pl.BlockSpec((tq, HS * dh), lambda b, h, r, tm, ws: (qblk(b, r), h)),
            pl.BlockSpec((n_lat, HS * dh), lambda b, h, r, tm, ws: (b, HG + h)),
            pl.BlockSpec((n_lat, HS * dh), lambda b, h, r, tm, ws: (b, 2 * HG + h)),
            pl.BlockSpec((n_ctx, HS * dh), lambda b, h, r, tm, ws: (B * nblk + b, HG + h)),
            pl.BlockSpec((n_ctx, HS * dh), lambda b, h, r, tm, ws: (B * nblk + b, 2 * HG + h)),
            pl.BlockSpec((HS, 1, tq, tk), lambda b, h, r, tm, ws: (h, tm[r], 0, 0)),
        ],
        out_specs=pl.BlockSpec((tq, HS * dh), lambda b, h, r, tm, ws: (qblk(b, r), h)),
    )
    return pl.pallas_call(
        functools.partial(_na_kernel, qscale=dh ** -0.5 * LOG2E),
        grid_spec=grid_spec,
        out_shape=jax.ShapeDtypeStruct((rows, H * dh), jnp.bfloat16),
        compiler_params=pltpu.CompilerParams(vmem_limit_bytes=VMEM_LIMIT),
        name="na_attention",
    )(jnp.asarray(tmap), jnp.asarray(ws), qkv, qkv, qkv, qkv, qkv, bias)


MLSTM_STEP_CHUNKS = 1
MLSTM_STEP_ROWS = MLSTM_CHUNK * MLSTM_STEP_CHUNKS


def rope_tables(n):
    dh = MLSTM_HEAD_DIM
    quarter = dh // 4
    t = np.arange(n)
    pos = np.stack([t // GRID_W, t % GRID_W], axis=1).astype(np.float32)
    lane = np.arange(dh)
    inv = jnp.asarray(ROPE_BASE, jnp.float32) ** (-jnp.arange(quarter, dtype=jnp.float32) / quarter)
    ang = jnp.asarray(pos)[:, lane // (dh // 2)] * inv[lane % quarter][None, :]
    cos, sin = jnp.cos(ang), jnp.sin(ang)
    first = jnp.asarray((lane % (dh // 2)) < quarter)[None, :]
    sin_up = jnp.where(first, -sin, 0.0)
    sin_dn = jnp.where(first, 0.0, sin)
    pad = lambda a, v: jnp.concatenate([a, jnp.full((MLSTM_STEP_ROWS, dh), v, jnp.float32)], axis=0)
    return pad(cos, 1.0), pad(sin_up, 0.0), pad(sin_dn, 0.0)


def _log_sigmoid(x):
    return jnp.minimum(x, 0.0) - jnp.log1p(jnp.exp(-jnp.abs(x)))


def _mlstm_chunk(qb, kb, v, gt, b_i, b_f, state, rev):
    L = MLSTM_CHUNK
    dh = MLSTM_HEAD_DIM
    f32, bf16 = jnp.float32, jnp.bfloat16
    Cn, m = state

    row = lax.broadcasted_iota(jnp.int32, gt.shape, 0)
    gl = jnp.where(row == 0, gt + b_i, jnp.where(row == 1, _log_sigmoid(gt + b_f), 0.0))
    glT = gl.T
    li_row, lf_row = gl[0:1], gl[1:2]
    li_col, lf_col = glT[:, 0:1], glT[:, 1:2]
    jj = lax.broadcasted_iota(jnp.int32, (L, L), 0)
    ss = lax.broadcasted_iota(jnp.int32, (L, L), 1)
    tri = (ss >= jj) if rev else (ss <= jj)
    tri_t = (ss <= jj) if rev else (ss >= jj)
    b_col = jnp.sum(jnp.where(tri, lf_row, 0.0), axis=1, keepdims=True)
    b_row = jnp.sum(jnp.where(tri_t, lf_col, 0.0), axis=0, keepdims=True)
    g = jnp.sum(lf_row, axis=1, keepdims=True)

    Dm = jnp.where(tri, b_col - b_row + li_row, MASKED)
    inter = b_col + m
    m_j = jnp.maximum(inter, jnp.max(Dm, axis=1, keepdims=True))
    nt = (((1,), (1,)), ((), ()))
    S = lax.dot_general(qb, kb, nt, preferred_element_type=f32) * jnp.exp(Dm - m_j)
    w_int = jnp.exp(inter - m_j)
    qCn = lax.dot_general(qb, Cn.astype(bf16), nt, preferred_element_type=f32)
    num = jnp.dot(S.astype(bf16), v.astype(bf16), preferred_element_type=f32) + w_int * qCn[:, :dh]
    den = jnp.sum(S, axis=1, keepdims=True) + w_int * qCn[:, dh:dh + 1]
    out = num / jnp.maximum(jnp.abs(den), jnp.exp(-m_j))

    a_row = g - b_row + li_row
    a_col = g - b_col + li_col
    m_loc = jnp.max(a_row, axis=1, keepdims=True)
    wa = jnp.exp(a_col - m_loc)
    C_loc = lax.dot_general((v * wa).astype(bf16), kb, (((0,), (0,)), ((), ())), preferred_element_type=f32)
    n_loc = jnp.sum(kb.astype(f32) * wa, axis=0, keepdims=True)
    upd = jnp.concatenate([C_loc, jnp.broadcast_to(n_loc, (8, dh))], axis=0)
    m_new = jnp.maximum(g + m, m_loc)
    sp = jnp.exp(g + m - m_new)
    sl = jnp.exp(m_loc - m_new)
    return out, (sp * Cn + sl * upd, m_new)


def _mlstm_direction(q_ref, k_ref, v_ref, g_ref, b_i, b_f, state_refs, o_ref, rev):
    L = MLSTM_CHUNK
    state = tuple(r[...] for r in state_refs)
    order = range(MLSTM_STEP_CHUNKS - 1, -1, -1) if rev else range(MLSTM_STEP_CHUNKS)
    for t in order:
        rs = slice(t * L, (t + 1) * L)
        out, state = _mlstm_chunk(q_ref[rs, :], k_ref[rs, :], v_ref[rs, :], g_ref[0, 0, :, rs], b_i, b_f, state, rev)
        o_ref[rs, :] = out
    for r, val in zip(state_refs, state):
        r[...] = val


def _mlstm_kernel(gb_ref, qf, kf, vf, gf, qb, kb, vb, gbk, of, ob, Cf, mf, Cb, mb):
    h = pl.program_id(1)

    @pl.when(pl.program_id(2) == 0)
    def _():
        for r in (Cf, mf, Cb, mb):
            r[...] = jnp.zeros_like(r)

    _mlstm_direction(qf, kf, vf, gf, gb_ref[0, h], gb_ref[1, h], (Cf, mf), of, False)
    _mlstm_direction(qb, kb, vb, gbk, gb_ref[2, h], gb_ref[3, h], (Cb, mb), ob, True)


def _rope_kernel(p_ref, cos_ref, su_ref, sd_ref, o_ref):
    dh = MLSTM_HEAD_DIM
    cos, su, sd = cos_ref[...], su_ref[...], sd_ref[...]
    for hs in range(2 * MLSTM_HEADS):
        ls = slice(hs * dh, (hs + 1) * dh)
        x = p_ref[:, ls] if hs < MLSTM_HEADS else p_ref[:, ls] * (dh ** -0.5)
        y = x * cos + pltpu.roll(x, dh - dh // 4, 1) * su + pltpu.roll(x, dh // 4, 1) * sd
        o_ref[:, ls] = y.astype(o_ref.dtype)


def rope_qk(P, rope, B, n_lat):
    rows = P.shape[0]
    tm = ROW_TILE
    dh = MLSTM_HEAD_DIM
    per_sample, lat_tiles = n_lat // tm, B * n_lat // tm
    tab = pl.BlockSpec((tm, dh), lambda t: (jnp.where(t < lat_tiles, t % per_sample, per_sample), 0))
    return pl.pallas_call(
        _rope_kernel,
        grid=(rows // tm,),
        in_specs=[pl.BlockSpec((tm, 2 * MLSTM_WIDTH), lambda t: (t, 0)), tab, tab, tab],
        out_specs=pl.BlockSpec((tm, 2 * MLSTM_WIDTH), lambda t: (t, 0)),
        out_shape=jax.ShapeDtypeStruct((rows, 2 * MLSTM_WIDTH), jnp.bfloat16),
        compiler_params=pltpu.CompilerParams(vmem_limit_bytes=VMEM_LIMIT),
        name="rope_qk",
    )(P, *rope)


def mlstm_bidir(QK, P, GT, gate_b, B, n_lat, n_ctx):
    H, dh = MLSTM_HEADS, MLSTM_HEAD_DIM
    L = MLSTM_STEP_ROWS
    rows = P.shape[0]
    NB, CB = n_lat // L, n_ctx // L

    def blk(b, s, rev):
        sc = (CB - 1 - s) if rev else s
        sl = (NB - 1 - (s - CB)) if rev else (s - CB)
        return jnp.where(s < CB, B * NB + b * CB + sc, b * NB + sl)

    def specs(rev):
        d = 1 if rev else 0
        return [
            pl.BlockSpec((L, dh), lambda b, h, s, gb: (blk(b, s, rev), h)),
            pl.BlockSpec((L, dh), lambda b, h, s, gb: (blk(b, s, rev), H + h)),
            pl.BlockSpec((L, dh), lambda b, h, s, gb: (blk(b, s, rev), 2 * H + h)),
            pl.BlockSpec((1, 1, 8, L), lambda b, h, s, gb: (d, h, 0, blk(b, s, rev))),
        ]

    state = [pltpu.VMEM((dh + 8, dh), jnp.float32), pltpu.VMEM((1, 1), jnp.float32)]
    grid_spec = pltpu.PrefetchScalarGridSpec(
        num_scalar_prefetch=1,
        grid=(B, H, NB + CB),
        in_specs=specs(False) + specs(True),
        out_specs=[pl.BlockSpec((L, dh), lambda b, h, s, gb: (blk(b, s, False), h)),
                   pl.BlockSpec((L, dh), lambda b, h, s, gb: (blk(b, s, True), h))],
        scratch_shapes=state + state,
    )
    return pl.pallas_call(
        _mlstm_kernel,
        grid_spec=grid_spec,
        out_shape=[jax.ShapeDtypeStruct((rows, H * dh), jnp.float32)] * 2,
        name="mlstm_bidir",
    )(gate_b, QK, QK, P, GT, QK, QK, P, GT)


def gates_layout(G):
    rows = G.shape[0]
    g = G.T.reshape(2, 2, MLSTM_HEADS, rows).transpose(0, 2, 1, 3)
    return jnp.pad(g, ((0, 0), (0, 0), (0, 6), (0, 0)))


FFN_TF = 256


def _ffn_kernel(x_ref, wg_ref, wu_ref, wd_ref, gate_ref, o_ref, acc_ref):
    f = pl.program_id(2)
    x = x_ref[0, 0]
    a = jnp.dot(x, wg_ref[0, 0].astype(jnp.bfloat16), preferred_element_type=jnp.float32)
    u = jnp.dot(x, wu_ref[0, 0].astype(jnp.bfloat16), preferred_element_type=jnp.float32)
    act = (a * jax.nn.sigmoid(a) * u).astype(jnp.bfloat16)
    y = jnp.dot(act, wd_ref[0, 0].astype(jnp.bfloat16), preferred_element_type=jnp.float32)

    @pl.when(f == 0)
    def _():
        acc_ref[...] = y

    @pl.when(f > 0)
    def _():
        acc_ref[...] += y

    @pl.when(f == pl.num_programs(2) - 1)
    def _():
        o_ref[0, 0] = (acc_ref[...] * gate_ref[0, 0]).astype(o_ref.dtype)


def expert_ffn(xin, w_gate, w_up, w_down, layer, gate):
    B, E, C, D = xin.shape
    F = w_gate.shape[-1]
    return pl.pallas_call(
        _ffn_kernel,
        grid=(E, B, F // FFN_TF),
        in_specs=[pl.BlockSpec((1, 1, C, D), lambda e, b, f: (b, e, 0, 0)),
                  pl.BlockSpec((1, 1, D, FFN_TF), lambda e, b, f: (layer, e, 0, f)),
                  pl.BlockSpec((1, 1, D, FFN_TF), lambda e, b, f: (layer, e, 0, f)),
                  pl.BlockSpec((1, 1, FFN_TF, D), lambda e, b, f: (layer, e, f, 0)),
                  pl.BlockSpec((1, 1, C, 1), lambda e, b, f: (b, e, 0, 0))],
        out_specs=pl.BlockSpec((1, 1, C, D), lambda e, b, f: (b, e, 0, 0)),
        out_shape=jax.ShapeDtypeStruct((B, E, C, D), jnp.bfloat16),
        scratch_shapes=[pltpu.VMEM((C, D), jnp.float32)],
        compiler_params=pltpu.CompilerParams(vmem_limit_bytes=VMEM_LIMIT),
        name="expert_ffn",
    )(xin, w_gate, w_up, w_down, gate)


COMBINE_TOKENS = 512
COMBINE_ROWS = 256


def _combine_kernel(tile_ref, blk_ref, first_ref, valid_ref, y_ref, tok_ref, o_ref):
    i = pl.program_id(0)

    @pl.when(first_ref[i] == 1)
    def _():
        o_ref[...] = jnp.zeros_like(o_ref)

    @pl.when(valid_ref[i] == 1)
    def _():
        base = tile_ref[i] * COMBINE_TOKENS
        t = base + lax.broadcasted_iota(jnp.int32, (COMBINE_TOKENS, COMBINE_ROWS), 0)
        onehot = jnp.where(tok_ref[0] == t, 1.0, 0.0).astype(jnp.bfloat16)
        o_ref[...] += jnp.dot(onehot, y_ref[...], preferred_element_type=jnp.float32)


def combine_routed(ys, tok_sorted, rows):
    R, D = ys.shape
    n_tiles, n_blk = rows // COMBINE_TOKENS, R // COMBINE_ROWS
    n_items = n_tiles + n_blk
    edges = jnp.arange(n_tiles + 1, dtype=jnp.int32) * COMBINE_TOKENS
    bounds = jnp.sum((tok_sorted[None, :] < edges[:, None]).astype(jnp.int32), axis=1)
    first_blk = jnp.minimum(bounds[:-1] // COMBINE_ROWS, n_blk - 1)
    last_blk = jnp.maximum((bounds[1:] - 1) // COMBINE_ROWS, first_blk)
    start = jnp.cumsum(last_blk - first_blk + 1) - (last_blk - first_blk + 1)
    item = jnp.arange(n_items, dtype=jnp.int32)
    tile = jnp.sum((start[None, :] <= item[:, None]).astype(jnp.int32), axis=1) - 1
    blk = first_blk[tile] + item - start[tile]
    valid = (blk <= last_blk[tile]).astype(jnp.int32)
    blk = jnp.minimum(blk, n_blk - 1).astype(jnp.int32)
    first = (item == start[tile]).astype(jnp.int32)
    grid_spec = pltpu.PrefetchScalarGridSpec(
        num_scalar_prefetch=4,
        grid=(n_items,),
        in_specs=[pl.BlockSpec((COMBINE_ROWS, D), lambda i, tile, blk, first, valid: (blk[i], 0)),
                  pl.BlockSpec((1, 1, COMBINE_ROWS), lambda i, tile, blk, first, valid: (blk[i], 0, 0))],
        out_specs=pl.BlockSpec((COMBINE_TOKENS, D), lambda i, tile, blk, first, valid: (tile[i], 0)),
    )
    return pl.pallas_call(
        _combine_kernel,
        grid_spec=grid_spec,
        out_shape=jax.ShapeDtypeStruct((rows, D), jnp.float32),
        compiler_params=pltpu.CompilerParams(vmem_limit_bytes=VMEM_LIMIT),
        name="combine_routed",
    )(tile, blk, first, valid, ys, tok_sorted.reshape(n_blk, 1, COMBINE_ROWS))


ROW_TILE = 256


def _res_norm_kernel(*refs, has_res, has_proj, has_router):
    it = iter(refs)
    x_ref = next(it)
    f_ref = next(it) if has_res else None
    w_ref = next(it) if has_proj else None
    gate_ref = next(it) if has_res else None
    g_ref, shift_ref, scale1_ref = next(it), next(it), next(it)
    wr_ref = next(it) if has_router else None
    xo_ref = next(it) if has_res else None
    h_ref = next(it)
    aff_ref = next(it) if has_router else None

    x = x_ref[...]
    if has_res:
        f = jnp.dot(f_ref[...], w_ref[...], preferred_element_type=jnp.float32) if has_proj else f_ref[...]
        x = x + gate_ref[0] * f
        xo_ref[...] = x
    y = x * lax.rsqrt(jnp.mean(x * x, axis=-1, keepdims=True) + EPS) * g_ref[...]
    h = (y * scale1_ref[0] + shift_ref[0]).astype(h_ref.dtype)
    h_ref[...] = h
    if has_router:
        logits = jnp.dot(h, wr_ref[...], preferred_element_type=jnp.float32)
        e = jnp.exp(logits - jnp.max(logits, axis=-1, keepdims=True))
        aff_ref[...] = e / jnp.sum(e, axis=-1, keepdims=True)


def res_norm(X, g, shift, scale1, n_lat, n_groups, res=None, w_router=None, out_dtype=jnp.bfloat16):
    rows, D = X.shape
    tm = ROW_TILE
    per_sample = n_lat // tm
    grp = lambda t: (jnp.minimum(t // per_sample, n_groups - 1), 0, 0)
    row_spec = pl.BlockSpec((tm, D), lambda t: (t, 0))
    vec_spec = pl.BlockSpec((1, 1, D), grp)
    ins, in_specs = [X], [row_spec]
    has_proj = res is not None and isinstance(res[0], tuple)
    if has_proj:
        (A, W), gate = res
        ins += [A, W, gate]
        in_specs += [pl.BlockSpec((tm, A.shape[1]), lambda t: (t, 0)), pl.BlockSpec(W.shape, lambda t: (0, 0)), vec_spec]
    elif res is not None:
        F, gate = res
        ins += [F, gate]
        in_specs += [row_spec, vec_spec]
    ins += [g.reshape(1, D), shift, scale1]
    in_specs += [pl.BlockSpec((1, D), lambda t: (0, 0)), vec_spec, vec_spec]
    outs, out_specs = [], []
    if res is not None:
        outs.append(jax.ShapeDtypeStruct((rows, D), X.dtype))
        out_specs.append(row_spec)
    outs.append(jax.ShapeDtypeStruct((rows, D), out_dtype))
    out_specs.append(row_spec)
    if w_router is not None:
        E = w_router.shape[1]
        ins.append(w_router)
        in_specs.append(pl.BlockSpec((D, E), lambda t: (0, 0)))
        outs.append(jax.ShapeDtypeStruct((rows, E), jnp.float32))
        out_specs.append(pl.BlockSpec((tm, E), lambda t: (t, 0)))
    return pl.pallas_call(
        functools.partial(_res_norm_kernel, has_res=res is not None, has_proj=has_proj,
                          has_router=w_router is not None),
        grid=(rows // tm,),
        in_specs=in_specs,
        out_specs=out_specs,
        out_shape=outs,
        input_output_aliases={0: 0} if res is not None else {},
        compiler_params=pltpu.CompilerParams(vmem_limit_bytes=VMEM_LIMIT),
        name="res_norm",
    )(*ins)


POOL_HALO = 8


def _even_mix_kernel(hf_ref, hb_ref, o_ref, u_ref, up_ref, un_ref, hg_ref, pw_ref, ps_ref, mix_ref, xe_ref,
                     *, n_lat, lat_tiles, n_ctx):
    tm = ROW_TILE
    H, dh = MLSTM_HEADS, MLSTM_HEAD_DIM
    t = pl.program_id(0)
    hh = hf_ref[...] + hb_ref[...]
    for h in range(H):
        sl = slice(h * dh, (h + 1) * dh)
        x = hh[:, sl]
        hn = x * lax.rsqrt(jnp.mean(x * x, axis=-1, keepdims=True) + EPS) * hg_ref[:, sl]
        mix_ref[:, sl] = (hn * jax.nn.sigmoid(o_ref[:, sl])).astype(mix_ref.dtype)

    is_lat = t < lat_tiles
    n_seq = jnp.where(is_lat, n_lat, n_ctx)
    p0 = jnp.where(is_lat, (t % (n_lat // tm)) * tm, ((t - lat_tiles) % (n_ctx // tm)) * tm)
    pos_prev = p0 - POOL_HALO + lax.broadcasted_iota(jnp.int32, (POOL_HALO, 1), 0)
    pos_next = p0 + tm + lax.broadcasted_iota(jnp.int32, (POOL_HALO, 1), 0)
    xe_ref[0:POOL_HALO, :] = jnp.where(pos_prev >= 0, up_ref[...], 0.0)
    xe_ref[POOL_HALO:POOL_HALO + tm, :] = u_ref[...]
    xe_ref[POOL_HALO + tm:, :] = jnp.where(pos_next < n_seq, un_ref[...], 0.0)
    pos = p0 + lax.broadcasted_iota(jnp.int32, (tm, 1), 0)
    for gi, w in enumerate(POOL_WINDOWS):
        cs = slice(gi * POOL_GROUP_DIM, (gi + 1) * POOL_GROUP_DIM)
        acc = xe_ref[POOL_HALO - w // 2:POOL_HALO - w // 2 + tm, cs]
        for d in range(1, w):
            acc = acc + xe_ref[POOL_HALO - w // 2 + d:POOL_HALO - w // 2 + d + tm, cs]
        cnt = (jnp.minimum(pos + w // 2, n_seq) - jnp.maximum(pos - w // 2, 0)).astype(jnp.float32)
        diff = (acc / cnt - u_ref[:, cs]).astype(jnp.bfloat16)
        y = jnp.dot(diff, pw_ref[gi], preferred_element_type=jnp.float32) * ps_ref[:, cs]
        mix_ref[:, MLSTM_WIDTH + gi * POOL_GROUP_DIM:MLSTM_WIDTH + (gi + 1) * POOL_GROUP_DIM] = y.astype(mix_ref.dtype)


def even_mix(hf, hb, P, head_g, pool_w, pool_scale, B, n_lat, n_ctx):
    rows = hf.shape[0]
    tm = ROW_TILE
    hpt = tm // POOL_HALO
    last_halo = rows // POOL_HALO - 1
    wide = pl.BlockSpec((tm, MLSTM_WIDTH), lambda t: (t, 0))
    return pl.pallas_call(
        functools.partial(_even_mix_kernel, n_lat=n_lat, lat_tiles=B * n_lat // tm, n_ctx=n_ctx),
        grid=(rows // tm,),
        in_specs=[wide, wide,
                  pl.BlockSpec((tm, MLSTM_WIDTH), lambda t: (t, 3)),
                  pl.BlockSpec((tm, POOL_WIDTH), lambda t: (t, 4)),
                  pl.BlockSpec((POOL_HALO, POOL_WIDTH), lambda t: (jnp.maximum(t * hpt - 1, 0), 4)),
                  pl.BlockSpec((POOL_HALO, POOL_WIDTH), lambda t: (jnp.minimum((t + 1) * hpt, last_halo), 4)),
                  pl.BlockSpec((1, MLSTM_WIDTH), lambda t: (0, 0)),
                  pl.BlockSpec((POOL_GROUPS, POOL_GROUP_DIM, POOL_GROUP_DIM), lambda t: (0, 0, 0)),
                  pl.BlockSpec((1, POOL_WIDTH), lambda t: (0, 0))],
        out_specs=pl.BlockSpec((tm, MLSTM_WIDTH + POOL_WIDTH), lambda t: (t, 0)),
        out_shape=jax.ShapeDtypeStruct((rows, MLSTM_WIDTH + POOL_WIDTH), jnp.bfloat16),
        scratch_shapes=[pltpu.VMEM((tm + 2 * POOL_HALO, POOL_WIDTH), jnp.float32)],
        compiler_params=pltpu.CompilerParams(vmem_limit_bytes=VMEM_LIMIT),
        name="even_mix",
    )(hf, hb, P, P, P, P, head_g.reshape(1, -1), pool_w.astype(jnp.bfloat16), pool_scale.reshape(1, -1))


def mlstm_pool_mixer(h, w_in, gate_b, head_g, pool_w, pool_scale, w_out, rope, B, n_lat, n_ctx):
    H = MLSTM_HEADS
    W4 = 4 * MLSTM_WIDTH
    w_in = w_in.astype(jnp.bfloat16)
    w_main = jnp.concatenate([w_in[:, :W4], w_in[:, W4 + 4 * H:]], axis=1)
    P = matmul(h, w_main)
    G = matmul(h, w_in[:, W4:W4 + 4 * H])
    hf, hb = mlstm_bidir(rope_qk(P, rope, B, n_lat), P, gates_layout(G), gate_b, B, n_lat, n_ctx)
    mix = even_mix(hf, hb, P, head_g, pool_w, pool_scale, B, n_lat, n_ctx)
    return mix, w_out.astype(jnp.bfloat16)


def na_mixer(h, w_in, rpb, w_out, B, n_lat, n_ctx):
    qkv = matmul(h, w_in.astype(jnp.bfloat16), jnp.bfloat16)
    return na_attention(qkv, rpb, B, n_lat, n_ctx), w_out.astype(jnp.bfloat16)


def expert_choice_ffn(h, aff, w_gate, w_up, w_down, layer, B, n_lat, n_ctx):
    E = N_EXPERTS
    rows, D = h.shape
    picks = []
    for lo, n in ((0, n_lat), (B * n_lat, n_ctx)):
        a = aff[lo:lo + B * n].reshape(B, n, E)
        gate, idx = lax.top_k(jnp.swapaxes(a, 1, 2), CAPACITY_FACTOR * n // E)
        picks.append((gate, idx + lo + (jnp.arange(B) * n)[:, None, None]))
    gate = jnp.concatenate([p[0] for p in picks], axis=-1)
    idx = jnp.concatenate([p[1] for p in picks], axis=-1)
    xin = h[idx]
    y = expert_ffn(xin, w_gate, w_up, w_down, layer, gate[..., None])
    tok = idx.reshape(-1)
    order = jnp.argsort(tok)
    return combine_routed(y.reshape(-1, D)[order], tok[order], rows)


def kernel(x, c, ctx, c_ctx, ada_w, ada_b, norm_g, final_g, ev_w_in, ev_gate_b, ev_head_g, ev_pool_w,
           ev_pool_scale, ev_w_out, na_w_in, na_rpb, na_w_out, moe_w_router, moe_w_gate, moe_w_up, moe_w_down):
    B, n_lat, D = x.shape
    n_ctx = ctx.shape[1]
    G = B + 1
    X = jnp.concatenate([x.reshape(B * n_lat, D), ctx.reshape(B * n_ctx, D)], axis=0)
    rope = rope_tables(n_lat)
    cc = jax.nn.silu(jnp.concatenate([c, c_ctx[None]], axis=0)).astype(jnp.bfloat16)
    cc = jnp.pad(cc, ((0, 16 - G), (0, 0)))
    norm = functools.partial(res_norm, n_lat=n_lat, n_groups=G)
    ada = ada_modulation(cc, ada_w, ada_b)[:, :G]
    res = None
    for i in range(DEPTH):
        j = i // 2
        sh, sc, ga, shf, scf, gaf = (v[:, None, :] for v in jnp.split(ada[i], 6, axis=-1))
        if res is None:
            h = norm(X, norm_g[i, 0], sh, 1 + sc)[0]
        else:
            X, h = norm(X, norm_g[i, 0], sh, 1 + sc, res=res)
        if i % 2 == 0:
            o = mlstm_pool_mixer(h, ev_w_in[j], ev_gate_b[j], ev_head_g[j], ev_pool_w[j], ev_pool_scale[j],
                                 ev_w_out[j], rope, B, n_lat, n_ctx)
        else:
            o = na_mixer(h, na_w_in[j], na_rpb[j], na_w_out[j], B, n_lat, n_ctx)
        X, h, aff = norm(X, norm_g[i, 1], shf, 1 + scf, res=(o, ga), w_router=moe_w_router[i].astype(jnp.bfloat16))
        f = expert_choice_ffn(h, aff, moe_w_gate, moe_w_up, moe_w_down, i, B, n_lat, n_ctx)
        res = (f, gaf)
    zeros = jnp.zeros((G, 1, D), jnp.float32)
    _, out = norm(X, final_g, zeros, zeros + 1, res=res, out_dtype=jnp.float32)
    return out[:B * n_lat].reshape(B, n_lat, D)
```

```python
import functools

import jax
import jax.numpy as jnp
import numpy as np
from jax import lax
from jax.experimental import pallas as pl
from jax.experimental.pallas import tpu as pltpu

D_MODEL = 2048
DEPTH = 4
GRID_W = 64
MLSTM_WIDTH = D_MODEL // 2
MLSTM_HEADS = 8
MLSTM_HEAD_DIM = MLSTM_WIDTH // MLSTM_HEADS
MLSTM_CHUNK = 256
POOL_WIDTH = D_MODEL // 2
POOL_WINDOWS = (2, 4, 8, 16)
POOL_GROUPS = len(POOL_WINDOWS)
POOL_GROUP_DIM = POOL_WIDTH // POOL_GROUPS
NA_HEADS = 16
NA_HEAD_DIM = D_MODEL // NA_HEADS
NA_WIDTH = NA_HEADS * NA_HEAD_DIM
NA_ROWS = 8
NA_COLS = 16
RPB_R = 2 * NA_ROWS - 1
RPB_C = 2 * NA_COLS - 1
N_EXPERTS = 16
EXPERT_FF = D_MODEL // 2
CAPACITY_FACTOR = 2
ROPE_BASE = 10000.0
EPS = 1e-6
MASKED = -1e30
VMEM_LIMIT = 48 * 1024 * 1024


def _mm_kernel(a_ref, w_ref, o_ref):
    o_ref[...] = jnp.dot(a_ref[...], w_ref[...], preferred_element_type=jnp.float32).astype(o_ref.dtype)


def matmul(a, w, out_dtype=jnp.float32, tm=512, tn=1024):
    M, K = a.shape
    N = w.shape[1]
    tm, tn = min(tm, M), min(tn, N)
    assert a.dtype == w.dtype == jnp.bfloat16 and M % tm == 0 and N % tn == 0
    return pl.pallas_call(
        _mm_kernel,
        grid=(N // tn, M // tm),
        in_specs=[pl.BlockSpec((tm, K), lambda j, i: (i, 0)),
                  pl.BlockSpec((K, tn), lambda j, i: (0, j))],
        out_specs=pl.BlockSpec((tm, tn), lambda j, i: (i, j)),
        out_shape=jax.ShapeDtypeStruct((M, N), out_dtype),
        compiler_params=pltpu.CompilerParams(vmem_limit_bytes=VMEM_LIMIT),
        name="matmul",
    )(a, w)


def _ada_kernel(c_ref, w_ref, b_ref, o_ref):
    o_ref[0] = jnp.dot(c_ref[...], w_ref[0].astype(jnp.bfloat16), preferred_element_type=jnp.float32) + b_ref[0]


def ada_modulation(cc, ada_w, ada_b, tn=1024):
    M, D = cc.shape
    depth, _, N = ada_w.shape
    return pl.pallas_call(
        _ada_kernel,
        grid=(depth, N // tn),
        in_specs=[pl.BlockSpec((M, D), lambda l, j: (0, 0)),
                  pl.BlockSpec((1, D, tn), lambda l, j: (l, 0, j)),
                  pl.BlockSpec((1, 1, tn), lambda l, j: (l, 0, j))],
        out_specs=pl.BlockSpec((1, M, tn), lambda l, j: (l, 0, j)),
        out_shape=jax.ShapeDtypeStruct((depth, M, N), jnp.float32),
        compiler_params=pltpu.CompilerParams(vmem_limit_bytes=VMEM_LIMIT),
        name="ada_modulation",
    )(cc, ada_w, ada_b.reshape(depth, 1, N))


NA_QROWS = 4
NA_KROWS = 12
NA_STEP_HEADS = 2
LOG2E = 1.4426950408889634


def _toeplitz(v):
    W = GRID_W
    lead = v.shape[:-1]
    y = jnp.concatenate([v, jnp.zeros(lead + (1,), v.dtype)], axis=-1)
    t = jnp.tile(y, (1,) * len(lead) + (W,))[..., : W * (2 * W - 1)].reshape(lead + (W, 2 * W - 1))
    return t[..., W - 1:]


def na_bias_tables(rpb, R):
    H = rpb.shape[0]
    W = GRID_W
    nblk = R // NA_QROWS
    r0 = np.arange(nblk) * NA_QROWS
    ws = np.clip(r0 - NA_ROWS // 2, 0, R - NA_KROWS)
    types, tmap = np.unique(ws - r0, return_inverse=True)
    dc = np.arange(-(W - 1), W)
    in_rpb = np.abs(dc + 0) <= NA_COLS - 1
    cols_idx = np.clip(dc + NA_COLS - 1, 0, RPB_C - 1)
    v = jnp.where(jnp.asarray(in_rpb), rpb[:, :, cols_idx], MASKED)
    toe = _toeplitz(v)
    c = np.arange(W)[:, None]
    kc = np.arange(W)[None, :]
    cs = np.clip(c - NA_COLS // 2, 0, W - NA_COLS)
    col_ok = (kc >= cs) & (kc < cs + NA_COLS)
    toe = jnp.where(jnp.asarray(col_ok), toe, MASKED)
    toe = jnp.concatenate([toe, jnp.full((H, 1, W, W), MASKED, toe.dtype)], axis=1)
    qi = np.arange(NA_QROWS)[:, None]
    kj = np.arange(NA_KROWS)[None, :]
    tabs = []
    for ti in range(len(types)):
        blk = int(np.nonzero(tmap == ti)[0][0])
        r = r0[blk] + qi
        kr = ws[blk] + kj
        rs = np.clip(r - NA_ROWS // 2, 0, R - NA_ROWS)
        row_ok = (kr >= rs) & (kr < rs + NA_ROWS)
        dr = np.where(row_ok, kr - r + NA_ROWS - 1, RPB_R)
        tab = toe[:, dr]
        tabs.append(tab.transpose(0, 1, 3, 2, 4).reshape(H, NA_QROWS * W, NA_KROWS * W))
    tabs.append(jnp.full((H, NA_QROWS * W, NA_KROWS * W), MASKED, jnp.float32))
    return jnp.stack(tabs, axis=1).astype(jnp.float32), tmap.astype(np.int32), ws.astype(np.int32)


def _na_kernel(tmap_ref, ws_ref, q_ref, k_ref, v_ref, kc_ref, vc_ref, bias_ref, o_ref, *, qscale):
    rb = pl.program_id(2)
    start = pl.multiple_of(ws_ref[rb] * GRID_W, GRID_W)
    dh = NA_HEAD_DIM
    tk = NA_KROWS * GRID_W
    dn = (((1,), (1,)), ((), ()))
    ones_w = jnp.ones((tk, dh), jnp.bfloat16)
    ones_c = jnp.ones((kc_ref.shape[0], dh), jnp.bfloat16)
    for hh in range(NA_STEP_HEADS):
        ls = slice(hh * dh, (hh + 1) * dh)
        q = (q_ref[:, ls].astype(jnp.float32) * qscale).astype(jnp.bfloat16)
        kw = k_ref[pl.ds(start, tk), ls]
        vw = jnp.concatenate([v_ref[pl.ds(start, tk), ls], ones_w], axis=1)
        vc = jnp.concatenate([vc_ref[:, ls], ones_c], axis=1)
        s_nb = lax.dot_general(q, kw, dn, preferred_element_type=jnp.float32) + bias_ref[hh, 0]
        s_cx = lax.dot_general(q, kc_ref[:, ls], dn, preferred_element_type=jnp.float32)
        m = jnp.maximum(jnp.max(s_nb, axis=-1, keepdims=True), jnp.max(s_cx, axis=-1, keepdims=True))
        p_nb = jnp.exp2(s_nb - m).astype(jnp.bfloat16)
        p_cx = jnp.exp2(s_cx - m).astype(jnp.bfloat16)
        ol = (jnp.dot(p_nb, vw, preferred_element_type=jnp.float32)
              + jnp.dot(p_cx, vc, preferred_element_type=jnp.float32))
        o_ref[:, ls] = (ol[:, :dh] / ol[:, dh:dh + 1]).astype(o_ref.dtype)


def na_attention(qkv, rpb, B, n_lat, n_ctx):
    rows = qkv.shape[0]
    H, dh = NA_HEADS, NA_HEAD_DIM
    HS = NA_STEP_HEADS
    HG = H // HS
    R = n_lat // GRID_W
    nblk = R // NA_QROWS
    tq = NA_QROWS * GRID_W
    tk = NA_KROWS * GRID_W
    assert n_ctx == tq and n_lat % tq == 0
    bias, tmap, ws = na_bias_tables(rpb, R)
    bias = jnp.maximum(bias * LOG2E, MASKED)
    tmap = np.concatenate([tmap, [bias.shape[1] - 1]]).astype(np.int32)
    ws = np.concatenate([ws, [0]]).astype(np.int32)

    def qblk(b, r):
        return jnp.where(r < nblk, b * nblk + r, B * nblk + b)

    grid_spec = pltpu.PrefetchScalarGridSpec(
        num_scalar_prefetch=2,
        grid=(B, HG, nblk + 1),
        in_specs=[
            pl.BlockSpec((tq, HS * dh), lambda b, h, r, tm, ws: (qblk(b, r), h)),
            pl.BlockSpec((n_lat, HS * dh), lambda b, h, r, tm, ws: (b, HG + h)),
            pl.BlockSpec((n_lat, HS * dh), lambda b, h, r, tm, ws: (b, 2 * HG + h)),
            pl.BlockSpec((n_ctx, HS * dh), lambda b, h, r, tm, ws: (B * nblk + b, HG + h)),
            pl.BlockSpec((n_ctx, HS * dh), lambda b, h, r, tm, ws: (B * nblk + b, 2 * HG + h)),
            pl.BlockSpec((HS, 1, tq, tk), lambda b, h, r, tm, ws: (h, tm[r], 0, 0)),
        ],
        out_specs=pl.BlockSpec((tq, HS * dh), lambda b, h, r, tm, ws: (qblk(b, r), h)),
    )
    return pl.pallas_call(
        functools.partial(_na_kernel, qscale=dh ** -0.5 * LOG2E),
        grid_spec=grid_spec,
        out_shape=jax.ShapeDtypeStruct((rows, H * dh), jnp.bfloat16),
        compiler_params=pltpu.CompilerParams(vmem_limit_bytes=VMEM_LIMIT),
        name="na_attention",
    )(jnp.asarray(tmap), jnp.asarray(ws), qkv, qkv, qkv, qkv, qkv, bias)


MLSTM_STEP_CHUNKS = 1
MLSTM_STEP_ROWS = MLSTM_CHUNK * MLSTM_STEP_CHUNKS


def rope_tables(n):
    dh = MLSTM_HEAD_DIM
    quarter = dh // 4
    t = np.arange(n)
    pos = np.stack([t // GRID_W, t % GRID_W], axis=1).astype(np.float32)
    lane = np.arange(dh)
    inv = jnp.asarray(ROPE_BASE, jnp.float32) ** (-jnp.arange(quarter, dtype=jnp.float32) / quarter)
    ang = jnp.asarray(pos)[:, lane // (dh // 2)] * inv[lane % quarter][None, :]
    cos, sin = jnp.cos(ang), jnp.sin(ang)
    first = jnp.asarray((lane % (dh // 2)) < quarter)[None, :]
    sin_up = jnp.where(first, -sin, 0.0)
    sin_dn = jnp.where(first, 0.0, sin)
    pad = lambda a, v: jnp.concatenate([a, jnp.full((MLSTM_STEP_ROWS, dh), v, jnp.float32)], axis=0)
    return pad(cos, 1.0), pad(sin_up, 0.0), pad(sin_dn, 0.0)


def _log_sigmoid(x):
    return jnp.minimum(x, 0.0) - jnp.log1p(jnp.exp(-jnp.abs(x)))


def _mlstm_chunk(qb, kb, v, gt, b_i, b_f, state, rev):
    L = MLSTM_CHUNK
    dh = MLSTM_HEAD_DIM
    f32, bf16 = jnp.float32, jnp.bfloat16
    Cn, m = state

    row = lax.broadcasted_iota(jnp.int32, gt.shape, 0)
    gl = jnp.where(row == 0, gt + b_i, jnp.where(row == 1, _log_sigmoid(gt + b_f), 0.0))
    glT = gl.T
    li_row, lf_row = gl[0:1], gl[1:2]
    li_col, lf_col = glT[:, 0:1], glT[:, 1:2]
    jj = lax.broadcasted_iota(jnp.int32, (L, L), 0)
    ss = lax.broadcasted_iota(jnp.int32, (L, L), 1)
    tri = (ss >= jj) if rev else (ss <= jj)
    tri_t = (ss <= jj) if rev else (ss >= jj)
    b_col = jnp.sum(jnp.where(tri, lf_row, 0.0), axis=1, keepdims=True)
    b_row = jnp.sum(jnp.where(tri_t, lf_col, 0.0), axis=0, keepdims=True)
    g = jnp.sum(lf_row, axis=1, keepdims=True)

    Dm = jnp.where(tri, b_col - b_row + li_row, MASKED)
    inter = b_col + m
    m_j = jnp.maximum(inter, jnp.max(Dm, axis=1, keepdims=True))
    nt = (((1,), (1,)), ((), ()))
    S = lax.dot_general(qb, kb, nt, preferred_element_type=f32) * jnp.exp(Dm - m_j)
    w_int = jnp.exp(inter - m_j)
    qCn = lax.dot_general(qb, Cn.astype(bf16), nt, preferred_element_type=f32)
    num = jnp.dot(S.astype(bf16), v.astype(bf16), preferred_element_type=f32) + w_int * qCn[:, :dh]
    den = jnp.sum(S, axis=1, keepdims=True) + w_int * qCn[:, dh:dh + 1]
    out = num / jnp.maximum(jnp.abs(den), jnp.exp(-m_j))

    a_row = g - b_row + li_row
    a_col = g - b_col + li_col
    m_loc = jnp.max(a_row, axis=1, keepdims=True)
    wa = jnp.exp(a_col - m_loc)
    C_loc = lax.dot_general((v * wa).astype(bf16), kb, (((0,), (0,)), ((), ())), preferred_element_type=f32)
    n_loc = jnp.sum(kb.astype(f32) * wa, axis=0, keepdims=True)
    upd = jnp.concatenate([C_loc, jnp.broadcast_to(n_loc, (8, dh))], axis=0)
    m_new = jnp.maximum(g + m, m_loc)
    sp = jnp.exp(g + m - m_new)
    sl = jnp.exp(m_loc - m_new)
    return out, (sp * Cn + sl * upd, m_new)


def _mlstm_direction(q_ref, k_ref, v_ref, g_ref, b_i, b_f, state_refs, o_ref, rev):
    L = MLSTM_CHUNK
    state = tuple(r[...] for r in state_refs)
    order = range(MLSTM_STEP_CHUNKS - 1, -1, -1) if rev else range(MLSTM_STEP_CHUNKS)
    for t in order:
        rs = slice(t * L, (t + 1) * L)
        out, state = _mlstm_chunk(q_ref[rs, :], k_ref[rs, :], v_ref[rs, :], g_ref[0, 0, :, rs], b_i, b_f, state, rev)
        o_ref[rs, :] = out
    for r, val in zip(state_refs, state):
        r[...] = val


def _mlstm_kernel(gb_ref, qf, kf, vf, gf, qb, kb, vb, gbk, of, ob, Cf, mf, Cb, mb):
    h = pl.program_id(1)

    @pl.when(pl.program_id(2) == 0)
    def _():
        for r in (Cf, mf, Cb, mb):
            r[...] = jnp.zeros_like(r)

    _mlstm_direction(qf, kf, vf, gf, gb_ref[0, h], gb_ref[1, h], (Cf, mf), of, False)
    _mlstm_direction(qb, kb, vb, gbk, gb_ref[2, h], gb_ref[3, h], (Cb, mb), ob, True)


def _rope_kernel(p_ref, cos_ref, su_ref, sd_ref, o_ref):
    dh = MLSTM_HEAD_DIM
    cos, su, sd = cos_ref[...], su_ref[...], sd_ref[...]
    for hs in range(2 * MLSTM_HEADS):
        ls = slice(hs * dh, (hs + 1) * dh)
        x = p_ref[:, ls] if hs < MLSTM_HEADS else p_ref[:, ls] * (dh ** -0.5)
        y = x * cos + pltpu.roll(x, dh - dh // 4, 1) * su + pltpu.roll(x, dh // 4, 1) * sd
        o_ref[:, ls] = y.astype(o_ref.dtype)


def rope_qk(P, rope, B, n_lat):
    rows = P.shape[0]
    tm = ROW_TILE
    dh = MLSTM_HEAD_DIM
    per_sample, lat_tiles = n_lat // tm, B * n_lat // tm
    tab = pl.BlockSpec((tm, dh), lambda t: (jnp.where(t < lat_tiles, t % per_sample, per_sample), 0))
    return pl.pallas_call(
        _rope_kernel,
        grid=(rows // tm,),
        in_specs=[pl.BlockSpec((tm, 2 * MLSTM_WIDTH), lambda t: (t, 0)), tab, tab, tab],
        out_specs=pl.BlockSpec((tm, 2 * MLSTM_WIDTH), lambda t: (t, 0)),
        out_shape=jax.ShapeDtypeStruct((rows, 2 * MLSTM_WIDTH), jnp.bfloat16),
        compiler_params=pltpu.CompilerParams(vmem_limit_bytes=VMEM_LIMIT),
        name="rope_qk",
    )(P, *rope)


def mlstm_bidir(QK, P, GT, gate_b, B, n_lat, n_ctx):
    H, dh = MLSTM_HEADS, MLSTM_HEAD_DIM
    L = MLSTM_STEP_ROWS
    rows = P.shape[0]
    NB, CB = n_lat // L, n_ctx // L

    def blk(b, s, rev):
        sc = (CB - 1 - s) if rev else s
        sl = (NB - 1 - (s - CB)) if rev else (s - CB)
        return jnp.where(s < CB, B * NB + b * CB + sc, b * NB + sl)

    def specs(rev):
        d = 1 if rev else 0
        return [
            pl.BlockSpec((L, dh), lambda b, h, s, gb: (blk(b, s, rev), h)),
            pl.BlockSpec((L, dh), lambda b, h, s, gb: (blk(b, s, rev), H + h)),
            pl.BlockSpec((L, dh), lambda b, h, s, gb: (blk(b, s, rev), 2 * H + h)),
            pl.BlockSpec((1, 1, 8, L), lambda b, h, s, gb: (d, h, 0, blk(b, s, rev))),
        ]

    state = [pltpu.VMEM((dh + 8, dh), jnp.float32), pltpu.VMEM((1, 1), jnp.float32)]
    grid_spec = pltpu.PrefetchScalarGridSpec(
        num_scalar_prefetch=1,
        grid=(B, H, NB + CB),
        in_specs=specs(False) + specs(True),
        out_specs=[pl.BlockSpec((L, dh), lambda b, h, s, gb: (blk(b, s, False), h)),
                   pl.BlockSpec((L, dh), lambda b, h, s, gb: (blk(b, s, True), h))],
        scratch_shapes=state + state,
    )
    return pl.pallas_call(
        _mlstm_kernel,
        grid_spec=grid_spec,
        out_shape=[jax.ShapeDtypeStruct((rows, H * dh), jnp.float32)] * 2,
        name="mlstm_bidir",
    )(gate_b, QK, QK, P, GT, QK, QK, P, GT)


def gates_layout(G):
    rows = G.shape[0]
    g = G.T.reshape(2, 2, MLSTM_HEADS, rows).transpose(0, 2, 1, 3)
    return jnp.pad(g, ((0, 0), (0, 0), (0, 6), (0, 0)))


FFN_TF = 256
FFN_ROW_SPLIT = 2


def _ffn_kernel(x_ref, wg_ref, wu_ref, wd_ref, gate_ref, o_ref, acc_ref):
    f = pl.program_id(2)
    @pl.when(f == 0)
    def _():
        acc_ref[...] = jnp.zeros_like(acc_ref)

    wg, wu, wd = (r[0, 0].astype(jnp.bfloat16) for r in (wg_ref, wu_ref, wd_ref))
    C = acc_ref.shape[0]
    for r0 in range(0, C, C // FFN_ROW_SPLIT):
        rs = slice(r0, r0 + C // FFN_ROW_SPLIT)
        x = x_ref[0, 0, rs, :]
        a = jnp.dot(x, wg, preferred_element_type=jnp.float32)
        u = jnp.dot(x, wu, preferred_element_type=jnp.float32)
        act = (a * jax.nn.sigmoid(a) * u).astype(jnp.bfloat16)
        acc_ref[rs, :] += jnp.dot(act, wd, preferred_element_type=jnp.float32)

    @pl.when(f == pl.num_programs(2) - 1)
    def _():
        o_ref[0, 0] = (acc_ref[...] * gate_ref[0, 0]).astype(o_ref.dtype)


def expert_ffn(xin, w_gate, w_up, w_down, layer, gate):
    B, E, C, D = xin.shape
    F = w_gate.shape[-1]
    return pl.pallas_call(
        _ffn_kernel,
        grid=(E, B, F // FFN_TF),
        in_specs=[pl.BlockSpec((1, 1, C, D), lambda e, b, f: (b, e, 0, 0)),
                  pl.BlockSpec((1, 1, D, FFN_TF), lambda e, b, f: (layer, e, 0, f)),
                  pl.BlockSpec((1, 1, D, FFN_TF), lambda e, b, f: (layer, e, 0, f)),
                  pl.BlockSpec((1, 1, FFN_TF, D), lambda e, b, f: (layer, e, f, 0)),
                  pl.BlockSpec((1, 1, C, 1), lambda e, b, f: (b, e, 0, 0))],
        out_specs=pl.BlockSpec((1, 1, C, D), lambda e, b, f: (b, e, 0, 0)),
        out_shape=jax.ShapeDtypeStruct((B, E, C, D), jnp.bfloat16),
        scratch_shapes=[pltpu.VMEM((C, D), jnp.float32)],
        compiler_params=pltpu.CompilerParams(vmem_limit_bytes=VMEM_LIMIT),
        name="expert_ffn",
    )(xin, w_gate, w_up, w_down, gate)


COMBINE_TOKENS = 512
COMBINE_ROWS = 256


def _combine_kernel(tile_ref, blk_ref, first_ref, valid_ref, y_ref, tok_ref, o_ref):
    i = pl.program_id(0)

    @pl.when(first_ref[i] == 1)
    def _():
        o_ref[...] = jnp.zeros_like(o_ref)

    @pl.when(valid_ref[i] == 1)
    def _():
        base = tile_ref[i] * COMBINE_TOKENS
        t = base + lax.broadcasted_iota(jnp.int32, (COMBINE_TOKENS, COMBINE_ROWS), 0)
        onehot = jnp.where(tok_ref[0] == t, 1.0, 0.0).astype(jnp.bfloat16)
        o_ref[...] += jnp.dot(onehot, y_ref[...], preferred_element_type=jnp.float32)


def combine_routed(ys, tok_sorted, rows):
    R, D = ys.shape
    n_tiles, n_blk = rows // COMBINE_TOKENS, R // COMBINE_ROWS
    n_items = n_tiles + n_blk
    edges = jnp.arange(n_tiles + 1, dtype=jnp.int32) * COMBINE_TOKENS
    bounds = jnp.sum((tok_sorted[None, :] < edges[:, None]).astype(jnp.int32), axis=1)
    first_blk = jnp.minimum(bounds[:-1] // COMBINE_ROWS, n_blk - 1)
    last_blk = jnp.maximum((bounds[1:] - 1) // COMBINE_ROWS, first_blk)
    start = jnp.cumsum(last_blk - first_blk + 1) - (last_blk - first_blk + 1)
    item = jnp.arange(n_items, dtype=jnp.int32)
    tile = jnp.sum((start[None, :] <= item[:, None]).astype(jnp.int32), axis=1) - 1
    blk = first_blk[tile] + item - start[tile]
    valid = (blk <= last_blk[tile]).astype(jnp.int32)
    blk = jnp.minimum(blk, n_blk - 1).astype(jnp.int32)
    first = (item == start[tile]).astype(jnp.int32)
    grid_spec = pltpu.PrefetchScalarGridSpec(
        num_scalar_prefetch=4,
        grid=(n_items,),
        in_specs=[pl.BlockSpec((COMBINE_ROWS, D), lambda i, tile, blk, first, valid: (blk[i], 0)),
                  pl.BlockSpec((1, 1, COMBINE_ROWS), lambda i, tile, blk, first, valid: (blk[i], 0, 0))],
        out_specs=pl.BlockSpec((COMBINE_TOKENS, D), lambda i, tile, blk, first, valid: (tile[i], 0)),
    )
    return pl.pallas_call(
        _combine_kernel,
        grid_spec=grid_spec,
        out_shape=jax.ShapeDtypeStruct((rows, D), jnp.float32),
        compiler_params=pltpu.CompilerParams(vmem_limit_bytes=VMEM_LIMIT),
        name="combine_routed",
    )(tile, blk, first, valid, ys, tok_sorted.reshape(n_blk, 1, COMBINE_ROWS))


ROW_TILE = 256
NORM_TILE = 512


def _res_norm_kernel(*refs, has_res, has_proj, has_router, keep_x):
    it = iter(refs)
    x_ref = next(it)
    f_ref = next(it) if has_res else None
    w_ref = next(it) if has_proj else None
    gate_ref = next(it) if has_res else None
    g_ref, shift_ref, scale1_ref = next(it), next(it), next(it)
    wr_ref = next(it) if has_router else None
    xo_ref = next(it) if keep_x else None
    h_ref = next(it)
    aff_ref = next(it) if has_router else None

    x = x_ref[...]
    if has_res:
        f = jnp.dot(f_ref[...], w_ref[...], preferred_element_type=jnp.float32) if has_proj else f_ref[...]
        x = x + gate_ref[0] * f
    if keep_x:
        xo_ref[...] = x
    y = x * lax.rsqrt(jnp.mean(x * x, axis=-1, keepdims=True) + EPS) * g_ref[...]
    h = (y * scale1_ref[0] + shift_ref[0]).astype(h_ref.dtype)
    h_ref[...] = h
    if has_router:
        logits = jnp.dot(h, wr_ref[...], preferred_element_type=jnp.float32)
        e = jnp.exp(logits - jnp.max(logits, axis=-1, keepdims=True))
        aff_ref[...] = e / jnp.sum(e, axis=-1, keepdims=True)


def res_norm(X, g, shift, scale1, n_lat, n_groups, res=None, w_router=None, out_dtype=jnp.bfloat16, rows_out=None):
    D = X.shape[1]
    keep_x = res is not None and rows_out is None
    rows = X.shape[0] if rows_out is None else rows_out
    tm = NORM_TILE
    per_sample = n_lat // tm
    grp = lambda t: (jnp.minimum(t // per_sample, n_groups - 1), 0, 0)
    row_spec = pl.BlockSpec((tm, D), lambda t: (t, 0))
    vec_spec = pl.BlockSpec((1, 1, D), grp)
    ins, in_specs = [X], [row_spec]
    has_proj = res is not None and isinstance(res[0], tuple)
    if has_proj:
        (A, W), gate = res
        ins += [A, W, gate]
        in_specs += [pl.BlockSpec((tm, A.shape[1]), lambda t: (t, 0)), pl.BlockSpec(W.shape, lambda t: (0, 0)), vec_spec]
    elif res is not None:
        F, gate = res
        ins += [F, gate]
        in_specs += [row_spec, vec_spec]
    ins += [g.reshape(1, D), shift, scale1]
    in_specs += [pl.BlockSpec((1, D), lambda t: (0, 0)), vec_spec, vec_spec]
    outs, out_specs = [], []
    if keep_x:
        outs.append(jax.ShapeDtypeStruct((rows, D), X.dtype))
        out_specs.append(row_spec)
    outs.append(jax.ShapeDtypeStruct((rows, D), out_dtype))
    out_specs.append(row_spec)
    if w_router is not None:
        E = w_router.shape[1]
        ins.append(w_router)
        in_specs.append(pl.BlockSpec((D, E), lambda t: (0, 0)))
        outs.append(jax.ShapeDtypeStruct((rows, E), jnp.float32))
        out_specs.append(pl.BlockSpec((tm, E), lambda t: (t, 0)))
    return pl.pallas_call(
        functools.partial(_res_norm_kernel, has_res=res is not None, has_proj=has_proj,
                          has_router=w_router is not None, keep_x=keep_x),
        grid=(rows // tm,),
        in_specs=in_specs,
        out_specs=out_specs,
        out_shape=outs,
        input_output_aliases={0: 0} if keep_x else {},
        compiler_params=pltpu.CompilerParams(vmem_limit_bytes=VMEM_LIMIT),
        name="res_norm",
    )(*ins)


POOL_HALO = 8


def _even_mix_kernel(hf_ref, hb_ref, o_ref, u_ref, up_ref, un_ref, hg_ref, pw_ref, ps_ref, mix_ref, xe_ref,
                     *, n_lat, lat_tiles, n_ctx):
    tm = ROW_TILE
    H, dh = MLSTM_HEADS, MLSTM_HEAD_DIM
    t = pl.program_id(0)
    hh = hf_ref[...] + hb_ref[...]
    for h in range(H):
        sl = slice(h * dh, (h + 1) * dh)
        x = hh[:, sl]
        hn = x * lax.rsqrt(jnp.mean(x * x, axis=-1, keepdims=True) + EPS) * hg_ref[:, sl]
        mix_ref[:, sl] = (hn * jax.nn.sigmoid(o_ref[:, sl])).astype(mix_ref.dtype)

    is_lat = t < lat_tiles
    n_seq = jnp.where(is_lat, n_lat, n_ctx)
    p0 = jnp.where(is_lat, (t % (n_lat // tm)) * tm, ((t - lat_tiles) % (n_ctx // tm)) * tm)
    pos_prev = p0 - POOL_HALO + lax.broadcasted_iota(jnp.int32, (POOL_HALO, 1), 0)
    pos_next = p0 + tm + lax.broadcasted_iota(jnp.int32, (POOL_HALO, 1), 0)
    xe_ref[0:POOL_HALO, :] = jnp.where(pos_prev >= 0, up_ref[...], 0.0)
    xe_ref[POOL_HALO:POOL_HALO + tm, :] = u_ref[...]
    xe_ref[POOL_HALO + tm:, :] = jnp.where(pos_next < n_seq, un_ref[...], 0.0)
    pos = p0 + lax.broadcasted_iota(jnp.int32, (tm, 1), 0)
    for gi, w in enumerate(POOL_WINDOWS):
        cs = slice(gi * POOL_GROUP_DIM, (gi + 1) * POOL_GROUP_DIM)
        acc = xe_ref[POOL_HALO - w // 2:POOL_HALO - w // 2 + tm, cs]
        for d in range(1, w):
            acc = acc + xe_ref[POOL_HALO - w // 2 + d:POOL_HALO - w // 2 + d + tm, cs]
        cnt = (jnp.minimum(pos + w // 2, n_seq) - jnp.maximum(pos - w // 2, 0)).astype(jnp.float32)
        diff = (acc / cnt - u_ref[:, cs]).astype(jnp.bfloat16)
        y = jnp.dot(diff, pw_ref[gi], preferred_element_type=jnp.float32) * ps_ref[:, cs]
        mix_ref[:, MLSTM_WIDTH + gi * POOL_GROUP_DIM:MLSTM_WIDTH + (gi + 1) * POOL_GROUP_DIM] = y.astype(mix_ref.dtype)


def even_mix(hf, hb, P, head_g, pool_w, pool_scale, B, n_lat, n_ctx):
    rows = hf.shape[0]
    tm = ROW_TILE
    hpt = tm // POOL_HALO
    last_halo = rows // POOL_HALO - 1
    wide = pl.BlockSpec((tm, MLSTM_WIDTH), lambda t: (t, 0))
    return pl.pallas_call(
        functools.partial(_even_mix_kernel, n_lat=n_lat, lat_tiles=B * n_lat // tm, n_ctx=n_ctx),
        grid=(rows // tm,),
        in_specs=[wide, wide,
                  pl.BlockSpec((tm, MLSTM_WIDTH), lambda t: (t, 3)),
                  pl.BlockSpec((tm, POOL_WIDTH), lambda t: (t, 4)),
                  pl.BlockSpec((POOL_HALO, POOL_WIDTH), lambda t: (jnp.maximum(t * hpt - 1, 0), 4)),
                  pl.BlockSpec((POOL_HALO, POOL_WIDTH), lambda t: (jnp.minimum((t + 1) * hpt, last_halo), 4)),
                  pl.BlockSpec((1, MLSTM_WIDTH), lambda t: (0, 0)),
                  pl.BlockSpec((POOL_GROUPS, POOL_GROUP_DIM, POOL_GROUP_DIM), lambda t: (0, 0, 0)),
                  pl.BlockSpec((1, POOL_WIDTH), lambda t: (0, 0))],
        out_specs=pl.BlockSpec((tm, MLSTM_WIDTH + POOL_WIDTH), lambda t: (t, 0)),
        out_shape=jax.ShapeDtypeStruct((rows, MLSTM_WIDTH + POOL_WIDTH), jnp.bfloat16),
        scratch_shapes=[pltpu.VMEM((tm + 2 * POOL_HALO, POOL_WIDTH), jnp.float32)],
        compiler_params=pltpu.CompilerParams(vmem_limit_bytes=VMEM_LIMIT),
        name="even_mix",
    )(hf, hb, P, P, P, P, head_g.reshape(1, -1), pool_w.astype(jnp.bfloat16), pool_scale.reshape(1, -1))


def mlstm_pool_mixer(h, w_in, gate_b, head_g, pool_w, pool_scale, w_out, rope, B, n_lat, n_ctx):
    H = MLSTM_HEADS
    W4 = 4 * MLSTM_WIDTH
    w_in = w_in.astype(jnp.bfloat16)
    w_main = jnp.concatenate([w_in[:, :W4], w_in[:, W4 + 4 * H:]], axis=1)
    P = matmul(h, w_main)
    G = matmul(h, w_in[:, W4:W4 + 4 * H])
    hf, hb = mlstm_bidir(rope_qk(P, rope, B, n_lat), P, gates_layout(G), gate_b, B, n_lat, n_ctx)
    mix = even_mix(hf, hb, P, head_g, pool_w, pool_scale, B, n_lat, n_ctx)
    return mix, w_out.astype(jnp.bfloat16)


def na_mixer(h, w_in, rpb, w_out, B, n_lat, n_ctx):
    qkv = matmul(h, w_in.astype(jnp.bfloat16), jnp.bfloat16)
    return na_attention(qkv, rpb, B, n_lat, n_ctx), w_out.astype(jnp.bfloat16)


def expert_choice_ffn(h, aff, w_gate, w_up, w_down, layer, B, n_lat, n_ctx):
    E = N_EXPERTS
    rows, D = h.shape
    picks = []
    for lo, n in ((0, n_lat), (B * n_lat, n_ctx)):
        a = aff[lo:lo + B * n].reshape(B, n, E)
        gate, idx = lax.top_k(jnp.swapaxes(a, 1, 2), CAPACITY_FACTOR * n // E)
        picks.append((gate, idx + lo + (jnp.arange(B) * n)[:, None, None]))
    gate = jnp.concatenate([p[0] for p in picks], axis=-1)
    idx = jnp.concatenate([p[1] for p in picks], axis=-1)
    xin = h[idx]
    y = expert_ffn(xin, w_gate, w_up, w_down, layer, gate[..., None])
    tok = idx.reshape(-1)
    order = jnp.argsort(tok)
    return combine_routed(y.reshape(-1, D)[order], tok[order], rows)


def kernel(x, c, ctx, c_ctx, ada_w, ada_b, norm_g, final_g, ev_w_in, ev_gate_b, ev_head_g, ev_pool_w,
           ev_pool_scale, ev_w_out, na_w_in, na_rpb, na_w_out, moe_w_router, moe_w_gate, moe_w_up, moe_w_down):
    B, n_lat, D = x.shape
    n_ctx = ctx.shape[1]
    G = B + 1
    X = jnp.concatenate([x.reshape(B * n_lat, D), ctx.reshape(B * n_ctx, D)], axis=0)
    rope = rope_tables(n_lat)
    cc = jax.nn.silu(jnp.concatenate([c, c_ctx[None]], axis=0)).astype(jnp.bfloat16)
    cc = jnp.pad(cc, ((0, 16 - G), (0, 0)))
    norm = functools.partial(res_norm, n_lat=n_lat, n_groups=G)
    ada = ada_modulation(cc, ada_w, ada_b)[:, :G]
    res = None
    for i in range(DEPTH):
        j = i // 2
        sh, sc, ga, shf, scf, gaf = (v[:, None, :] for v in jnp.split(ada[i], 6, axis=-1))
        if res is None:
            h = norm(X, norm_g[i, 0], sh, 1 + sc)[0]
        else:
            X, h = norm(X, norm_g[i, 0], sh, 1 + sc, res=res)
        if i % 2 == 0:
            o = mlstm_pool_mixer(h, ev_w_in[j], ev_gate_b[j], ev_head_g[j], ev_pool_w[j], ev_pool_scale[j],
                                 ev_w_out[j], rope, B, n_lat, n_ctx)
        else:
            o = na_mixer(h, na_w_in[j], na_rpb[j], na_w_out[j], B, n_lat, n_ctx)
        X, h, aff = norm(X, norm_g[i, 1], shf, 1 + scf, res=(o, ga), w_router=moe_w_router[i].astype(jnp.bfloat16))
        f = expert_choice_ffn(h, aff, moe_w_gate, moe_w_up, moe_w_down, i, B, n_lat, n_ctx)
        res = (f, gaf)
    zeros = jnp.zeros((G, 1, D), jnp.float32)
    out, = norm(X, final_g, zeros, zeros + 1, res=res, out_dtype=jnp.float32, rows_out=B * n_lat)
    return out.reshape(B, n_lat, D)
```

```python
import functools

import jax
import jax.numpy as jnp
import numpy as np
from jax import lax
from jax.experimental import pallas as pl
from jax.experimental.pallas import tpu as pltpu

D_MODEL = 2048
DEPTH = 4
GRID_W = 64
MLSTM_WIDTH = D_MODEL // 2
MLSTM_HEADS = 8
MLSTM_HEAD_DIM = MLSTM_WIDTH // MLSTM_HEADS
MLSTM_CHUNK = 256
POOL_WIDTH = D_MODEL // 2
POOL_WINDOWS = (2, 4, 8, 16)
POOL_GROUPS = len(POOL_WINDOWS)
POOL_GROUP_DIM = POOL_WIDTH // POOL_GROUPS
NA_HEADS = 16
NA_HEAD_DIM = D_MODEL // NA_HEADS
NA_WIDTH = NA_HEADS * NA_HEAD_DIM
NA_ROWS = 8
NA_COLS = 16
RPB_R = 2 * NA_ROWS - 1
RPB_C = 2 * NA_COLS - 1
N_EXPERTS = 16
EXPERT_FF = D_MODEL // 2
CAPACITY_FACTOR = 2
ROPE_BASE = 10000.0
EPS = 1e-6
MASKED = -1e30
VMEM_LIMIT = 48 * 1024 * 1024


def _mm_kernel(a_ref, w_ref, o_ref):
    o_ref[...] = jnp.dot(a_ref[...], w_ref[...], preferred_element_type=jnp.float32).astype(o_ref.dtype)


def matmul(a, w, out_dtype=jnp.float32, tm=512, tn=1024):
    M, K = a.shape
    N = w.shape[1]
    tm, tn = min(tm, M), min(tn, N)
    assert a.dtype == w.dtype == jnp.bfloat16 and M % tm == 0 and N % tn == 0
    return pl.pallas_call(
        _mm_kernel,
        grid=(N // tn, M // tm),
        in_specs=[pl.BlockSpec((tm, K), lambda j, i: (i, 0)),
                  pl.BlockSpec((K, tn), lambda j, i: (0, j))],
        out_specs=pl.BlockSpec((tm, tn), lambda j, i: (i, j)),
        out_shape=jax.ShapeDtypeStruct((M, N), out_dtype),
        compiler_params=pltpu.CompilerParams(vmem_limit_bytes=VMEM_LIMIT),
        name="matmul",
    )(a, w)


def _ada_kernel(c_ref, w_ref, b_ref, o_ref):
    o_ref[0] = jnp.dot(c_ref[...], w_ref[0].astype(jnp.bfloat16), preferred_element_type=jnp.float32) + b_ref[0]


def ada_modulation(cc, ada_w, ada_b, tn=1024):
    M, D = cc.shape
    depth, _, N = ada_w.shape
    return pl.pallas_call(
        _ada_kernel,
        grid=(depth, N // tn),
        in_specs=[pl.BlockSpec((M, D), lambda l, j: (0, 0)),
                  pl.BlockSpec((1, D, tn), lambda l, j: (l, 0, j)),
                  pl.BlockSpec((1, 1, tn), lambda l, j: (l, 0, j))],
        out_specs=pl.BlockSpec((1, M, tn), lambda l, j: (l, 0, j)),
        out_shape=jax.ShapeDtypeStruct((depth, M, N), jnp.float32),
        compiler_params=pltpu.CompilerParams(vmem_limit_bytes=VMEM_LIMIT),
        name="ada_modulation",
    )(cc, ada_w, ada_b.reshape(depth, 1, N))


NA_QROWS = 4
NA_KROWS = 12
NA_STEP_HEADS = 4
LOG2E = 1.4426950408889634


def _toeplitz(v):
    W = GRID_W
    lead = v.shape[:-1]
    y = jnp.concatenate([v, jnp.zeros(lead + (1,), v.dtype)], axis=-1)
    t = jnp.tile(y, (1,) * len(lead) + (W,))[..., : W * (2 * W - 1)].reshape(lead + (W, 2 * W - 1))
    return t[..., W - 1:]


def na_bias_tables(rpb, R):
    H = rpb.shape[0]
    W = GRID_W
    nblk = R // NA_QROWS
    r0 = np.arange(nblk) * NA_QROWS
    ws = np.clip(r0 - NA_ROWS // 2, 0, R - NA_KROWS)
    types, tmap = np.unique(ws - r0, return_inverse=True)
    dc = np.arange(-(W - 1), W)
    in_rpb = np.abs(dc + 0) <= NA_COLS - 1
    cols_idx = np.clip(dc + NA_COLS - 1, 0, RPB_C - 1)
    v = jnp.where(jnp.asarray(in_rpb), rpb[:, :, cols_idx], MASKED)
    toe = _toeplitz(v)
    c = np.arange(W)[:, None]
    kc = np.arange(W)[None, :]
    cs = np.clip(c - NA_COLS // 2, 0, W - NA_COLS)
    col_ok = (kc >= cs) & (kc < cs + NA_COLS)
    toe = jnp.where(jnp.asarray(col_ok), toe, MASKED)
    toe = jnp.concatenate([toe, jnp.full((H, 1, W, W), MASKED, toe.dtype)], axis=1)
    qi = np.arange(NA_QROWS)[:, None]
    kj = np.arange(NA_KROWS)[None, :]
    tabs = []
    for ti in range(len(types)):
        blk = int(np.nonzero(tmap == ti)[0][0])
        r = r0[blk] + qi
        kr = ws[blk] + kj
        rs = np.clip(r - NA_ROWS // 2, 0, R - NA_ROWS)
        row_ok = (kr >= rs) & (kr < rs + NA_ROWS)
        dr = np.where(row_ok, kr - r + NA_ROWS - 1, RPB_R)
        tab = toe[:, dr]
        tabs.append(tab.transpose(0, 1, 3, 2, 4).reshape(H, NA_QROWS * W, NA_KROWS * W))
    tabs.append(jnp.full((H, NA_QROWS * W, NA_KROWS * W), MASKED, jnp.float32))
    return jnp.stack(tabs, axis=1).astype(jnp.float32), tmap.astype(np.int32), ws.astype(np.int32)


def _na_kernel(tmap_ref, ws_ref, q_ref, k_ref, v_ref, kc_ref, vc_ref, bias_ref, o_ref, *, qscale):
    rb = pl.program_id(2)
    start = pl.multiple_of(ws_ref[rb] * GRID_W, GRID_W)
    dh = NA_HEAD_DIM
    tk = NA_KROWS * GRID_W
    dn = (((1,), (1,)), ((), ()))
    ones_w = jnp.ones((tk, dh), jnp.bfloat16)
    ones_c = jnp.ones((kc_ref.shape[0], dh), jnp.bfloat16)
    for hh in range(NA_STEP_HEADS):
        ls = slice(hh * dh, (hh + 1) * dh)
        q = (q_ref[:, ls].astype(jnp.float32) * qscale).astype(jnp.bfloat16)
        kw = k_ref[pl.ds(start, tk), ls]
        vw = jnp.concatenate([v_ref[pl.ds(start, tk), ls], ones_w], axis=1)
        vc = jnp.concatenate([vc_ref[:, ls], ones_c], axis=1)
        s_nb = lax.dot_general(q, kw, dn, preferred_element_type=jnp.float32) + bias_ref[hh, 0]
        s_cx = lax.dot_general(q, kc_ref[:, ls], dn, preferred_element_type=jnp.float32)
        m = jnp.maximum(jnp.max(s_nb, axis=-1, keepdims=True), jnp.max(s_cx, axis=-1, keepdims=True))
        p_nb = jnp.exp2(s_nb - m).astype(jnp.bfloat16)
        p_cx = jnp.exp2(s_cx - m).astype(jnp.bfloat16)
        ol = (jnp.dot(p_nb, vw, preferred_element_type=jnp.float32)
              + jnp.dot(p_cx, vc, preferred_element_type=jnp.float32))
        o_ref[:, ls] = (ol[:, :dh] / ol[:, dh:dh + 1]).astype(o_ref.dtype)


def na_attention(qkv, rpb, B, n_lat, n_ctx):
    rows = qkv.shape[0]
    H, dh = NA_HEADS, NA_HEAD_DIM
    HS = NA_STEP_HEADS
    HG = H // HS
    R = n_lat // GRID_W
    nblk = R // NA_QROWS
    tq = NA_QROWS * GRID_W
    tk = NA_KROWS * GRID_W
    assert n_ctx == tq and n_lat % tq == 0
    bias, tmap, ws = na_bias_tables(rpb, R)
    bias = jnp.maximum(bias * LOG2E, MASKED)
    tmap = np.concatenate([tmap, [bias.shape[1] - 1]]).astype(np.int32)
    ws = np.concatenate([ws, [0]]).astype(np.int32)

    def qblk(b, r):
        return jnp.where(r < nblk, b * nblk + r, B * nblk + b)

    grid_spec = pltpu.PrefetchScalarGridSpec(
        num_scalar_prefetch=2,
        grid=(B, HG, nblk + 1),
        in_specs=[
            pl.BlockSpec((tq, HS * dh), lambda b, h, r, tm, ws: (qblk(b, r), h)),
            pl.BlockSpec((n_lat, HS * dh), lambda b, h, r, tm, ws: (b, HG + h)),
            pl.BlockSpec((n_lat, HS * dh), lambda b, h, r, tm, ws: (b, 2 * HG + h)),
            pl.BlockSpec((n_ctx, HS * dh), lambda b, h, r, tm, ws: (B * nblk + b, HG + h)),
            pl.BlockSpec((n_ctx, HS * dh), lambda b, h, r, tm, ws: (B * nblk + b, 2 * HG + h)),
            pl.BlockSpec((HS, 1, tq, tk), lambda b, h, r, tm, ws: (h, tm[r], 0, 0)),
        ],
        out_specs=pl.BlockSpec((tq, HS * dh), lambda b, h, r, tm, ws: (qblk(b, r), h)),
    )
    return pl.pallas_call(
        functools.partial(_na_kernel, qscale=dh ** -0.5 * LOG2E),
        grid_spec=grid_spec,
        out_shape=jax.ShapeDtypeStruct((rows, H * dh), jnp.bfloat16),
        compiler_params=pltpu.CompilerParams(vmem_limit_bytes=VMEM_LIMIT),
        name="na_attention",
    )(jnp.asarray(tmap), jnp.asarray(ws), qkv, qkv, qkv, qkv, qkv, bias)


MLSTM_STEP_HEADS = 2


def rope_tables(n):
    dh = MLSTM_HEAD_DIM
    quarter = dh // 4
    t = np.arange(n)
    pos = np.stack([t // GRID_W, t % GRID_W], axis=1).astype(np.float32)
    lane = np.arange(dh)
    inv = jnp.asarray(ROPE_BASE, jnp.float32) ** (-jnp.arange(quarter, dtype=jnp.float32) / quarter)
    ang = jnp.asarray(pos)[:, lane // (dh // 2)] * inv[lane % quarter][None, :]
    cos, sin = jnp.cos(ang), jnp.sin(ang)
    first = jnp.asarray((lane % (dh // 2)) < quarter)[None, :]
    sin_up = jnp.where(first, -sin, 0.0)
    sin_dn = jnp.where(first, 0.0, sin)
    pad = lambda a, v: jnp.concatenate([a, jnp.full((ROW_TILE, dh), v, jnp.float32)], axis=0)
    return pad(cos, 1.0), pad(sin_up, 0.0), pad(sin_dn, 0.0)


def _log_sigmoid(x):
    return jnp.minimum(x, 0.0) - jnp.log1p(jnp.exp(-jnp.abs(x)))


def _mlstm_chunk(qb, kb, v, gt, b_i, b_f, state, rev):
    L = MLSTM_CHUNK
    dh = MLSTM_HEAD_DIM
    f32, bf16 = jnp.float32, jnp.bfloat16
    Cn, m = state

    row = lax.broadcasted_iota(jnp.int32, gt.shape, 0)
    gl = jnp.where(row == 0, gt + b_i, jnp.where(row == 1, _log_sigmoid(gt + b_f), 0.0))
    glT = gl.T
    li_row, lf_row = gl[0:1], gl[1:2]
    li_col, lf_col = glT[:, 0:1], glT[:, 1:2]
    jj = lax.broadcasted_iota(jnp.int32, (L, L), 0)
    ss = lax.broadcasted_iota(jnp.int32, (L, L), 1)
    tri = (ss >= jj) if rev else (ss <= jj)
    tri_t = (ss <= jj) if rev else (ss >= jj)
    b_col = jnp.sum(jnp.where(tri, lf_row, 0.0), axis=1, keepdims=True)
    b_row = jnp.sum(jnp.where(tri_t, lf_col, 0.0), axis=0, keepdims=True)
    g = jnp.sum(lf_row, axis=1, keepdims=True)

    Dm = jnp.where(tri, b_col - b_row + li_row, MASKED)
    inter = b_col + m
    m_j = jnp.maximum(inter, jnp.max(Dm, axis=1, keepdims=True))
    nt = (((1,), (1,)), ((), ()))
    S = lax.dot_general(qb, kb, nt, preferred_element_type=f32) * jnp.exp(Dm - m_j)
    w_int = jnp.exp(inter - m_j)
    qCn = lax.dot_general(qb, Cn.astype(bf16), nt, preferred_element_type=f32)
    num = jnp.dot(S.astype(bf16), v.astype(bf16), preferred_element_type=f32) + w_int * qCn[:, :dh]
    den = jnp.sum(S, axis=1, keepdims=True) + w_int * qCn[:, dh:dh + 1]
    out = num / jnp.maximum(jnp.abs(den), jnp.exp(-m_j))

    a_row = g - b_row + li_row
    a_col = g - b_col + li_col
    m_loc = jnp.max(a_row, axis=1, keepdims=True)
    wa = jnp.exp(a_col - m_loc)
    C_loc = lax.dot_general((v * wa).astype(bf16), kb, (((0,), (0,)), ((), ())), preferred_element_type=f32)
    n_loc = jnp.sum(kb.astype(f32) * wa, axis=0, keepdims=True)
    upd = jnp.concatenate([C_loc, jnp.broadcast_to(n_loc, (8, dh))], axis=0)
    m_new = jnp.maximum(g + m, m_loc)
    sp = jnp.exp(g + m - m_new)
    sl = jnp.exp(m_loc - m_new)
    return out, (sp * Cn + sl * upd, m_new)


def _mlstm_direction(q_ref, k_ref, v_ref, g_ref, gb_ref, state_refs, o_ref, rev):
    dh = MLSTM_HEAD_DIM
    d = 2 if rev else 0
    for hh in range(MLSTM_STEP_HEADS):
        h = pl.program_id(1) * MLSTM_STEP_HEADS + hh
        ls = slice(hh * dh, (hh + 1) * dh)
        Cn_ref, m_ref = state_refs[2 * hh], state_refs[2 * hh + 1]
        out, (Cn, m) = _mlstm_chunk(q_ref[:, ls], k_ref[:, ls], v_ref[:, ls], g_ref[0, hh], gb_ref[d, h],
                                    gb_ref[d + 1, h], (Cn_ref[...], m_ref[...]), rev)
        o_ref[:, ls] = out
        Cn_ref[...] = Cn
        m_ref[...] = m


def _mlstm_kernel(gb_ref, qf, kf, vf, gf, qb, kb, vb, gbk, of, ob, *state_refs):
    @pl.when(pl.program_id(2) == 0)
    def _():
        for r in state_refs:
            r[...] = jnp.zeros_like(r)

    half = len(state_refs) // 2
    _mlstm_direction(qf, kf, vf, gf, gb_ref, state_refs[:half], of, False)
    _mlstm_direction(qb, kb, vb, gbk, gb_ref, state_refs[half:], ob, True)


def _rope_kernel(p_ref, cos_ref, su_ref, sd_ref, o_ref):
    dh = MLSTM_HEAD_DIM
    cos, su, sd = cos_ref[...], su_ref[...], sd_ref[...]
    for hs in range(2 * MLSTM_HEADS):
        ls = slice(hs * dh, (hs + 1) * dh)
        x = p_ref[:, ls] if hs < MLSTM_HEADS else p_ref[:, ls] * (dh ** -0.5)
        y = x * cos + pltpu.roll(x, dh - dh // 4, 1) * su + pltpu.roll(x, dh // 4, 1) * sd
        o_ref[:, ls] = y.astype(o_ref.dtype)


def rope_qk(P, rope, B, n_lat):
    rows = P.shape[0]
    tm = ROW_TILE
    dh = MLSTM_HEAD_DIM
    per_sample, lat_tiles = n_lat // tm, B * n_lat // tm
    tab = pl.BlockSpec((tm, dh), lambda t: (jnp.where(t < lat_tiles, t % per_sample, per_sample), 0))
    return pl.pallas_call(
        _rope_kernel,
        grid=(rows // tm,),
        in_specs=[pl.BlockSpec((tm, 2 * MLSTM_WIDTH), lambda t: (t, 0)), tab, tab, tab],
        out_specs=pl.BlockSpec((tm, 2 * MLSTM_WIDTH), lambda t: (t, 0)),
        out_shape=jax.ShapeDtypeStruct((rows, 2 * MLSTM_WIDTH), jnp.bfloat16),
        compiler_params=pltpu.CompilerParams(vmem_limit_bytes=VMEM_LIMIT),
        name="rope_qk",
    )(P, *rope)


def mlstm_bidir(QK, P, GT, gate_b, B, n_lat, n_ctx):
    H, dh = MLSTM_HEADS, MLSTM_HEAD_DIM
    HS = MLSTM_STEP_HEADS
    HG = H // HS
    L = MLSTM_CHUNK
    rows = P.shape[0]
    NB, CB = n_lat // L, n_ctx // L

    def blk(b, s, rev):
        sc = (CB - 1 - s) if rev else s
        sl = (NB - 1 - (s - CB)) if rev else (s - CB)
        return jnp.where(s < CB, B * NB + b * CB + sc, b * NB + sl)

    def specs(rev):
        d = 1 if rev else 0
        return [
            pl.BlockSpec((L, HS * dh), lambda b, h, s, gb: (blk(b, s, rev), h)),
            pl.BlockSpec((L, HS * dh), lambda b, h, s, gb: (blk(b, s, rev), HG + h)),
            pl.BlockSpec((L, HS * dh), lambda b, h, s, gb: (blk(b, s, rev), 2 * HG + h)),
            pl.BlockSpec((1, HS, 8, L), lambda b, h, s, gb: (d, h, 0, blk(b, s, rev))),
        ]

    state = [pltpu.VMEM((dh + 8, dh), jnp.float32), pltpu.VMEM((1, 1), jnp.float32)] * (2 * HS)
    grid_spec = pltpu.PrefetchScalarGridSpec(
        num_scalar_prefetch=1,
        grid=(B, HG, NB + CB),
        in_specs=specs(False) + specs(True),
        out_specs=[pl.BlockSpec((L, HS * dh), lambda b, h, s, gb: (blk(b, s, False), h)),
                   pl.BlockSpec((L, HS * dh), lambda b, h, s, gb: (blk(b, s, True), h))],
        scratch_shapes=state,
    )
    return pl.pallas_call(
        _mlstm_kernel,
        grid_spec=grid_spec,
        out_shape=[jax.ShapeDtypeStruct((rows, H * dh), jnp.float32)] * 2,
        name="mlstm_bidir",
    )(gate_b, QK, QK, P, GT, QK, QK, P, GT)


def gates_layout(G):
    rows = G.shape[0]
    g = G.T.reshape(2, 2, MLSTM_HEADS, rows).transpose(0, 2, 1, 3)
    return jnp.pad(g, ((0, 0), (0, 0), (0, 6), (0, 0)))


FFN_TF = 256
FFN_ROW_SPLIT = 2


def _ffn_kernel(x_ref, wg_ref, wu_ref, wd_ref, gate_ref, o_ref, acc_ref):
    f = pl.program_id(2)
    @pl.when(f == 0)
    def _():
        acc_ref[...] = jnp.zeros_like(acc_ref)

    wg, wu, wd = (r[0, 0].astype(jnp.bfloat16) for r in (wg_ref, wu_ref, wd_ref))
    C = acc_ref.shape[0]
    for r0 in range(0, C, C // FFN_ROW_SPLIT):
        rs = slice(r0, r0 + C // FFN_ROW_SPLIT)
        x = x_ref[0, 0, rs, :]
        a = jnp.dot(x, wg, preferred_element_type=jnp.float32)
        u = jnp.dot(x, wu, preferred_element_type=jnp.float32)
        act = (a * jax.nn.sigmoid(a) * u).astype(jnp.bfloat16)
        acc_ref[rs, :] += jnp.dot(act, wd, preferred_element_type=jnp.float32)

    @pl.when(f == pl.num_programs(2) - 1)
    def _():
        o_ref[0, 0] = (acc_ref[...] * gate_ref[0, 0]).astype(o_ref.dtype)


def expert_ffn(xin, w_gate, w_up, w_down, layer, gate):
    B, E, C, D = xin.shape
    F = w_gate.shape[-1]
    return pl.pallas_call(
        _ffn_kernel,
        grid=(E, B, F // FFN_TF),
        in_specs=[pl.BlockSpec((1, 1, C, D), lambda e, b, f: (b, e, 0, 0)),
                  pl.BlockSpec((1, 1, D, FFN_TF), lambda e, b, f: (layer, e, 0, f)),
                  pl.BlockSpec((1, 1, D, FFN_TF), lambda e, b, f: (layer, e, 0, f)),
                  pl.BlockSpec((1, 1, FFN_TF, D), lambda e, b, f: (layer, e, f, 0)),
                  pl.BlockSpec((1, 1, C, 1), lambda e, b, f: (b, e, 0, 0))],
        out_specs=pl.BlockSpec((1, 1, C, D), lambda e, b, f: (b, e, 0, 0)),
        out_shape=jax.ShapeDtypeStruct((B, E, C, D), jnp.bfloat16),
        scratch_shapes=[pltpu.VMEM((C, D), jnp.float32)],
        compiler_params=pltpu.CompilerParams(vmem_limit_bytes=VMEM_LIMIT),
        name="expert_ffn",
    )(xin, w_gate, w_up, w_down, gate)


COMBINE_TOKENS = 512
COMBINE_ROWS = 256


def _combine_kernel(tile_ref, blk_ref, first_ref, valid_ref, y_ref, tok_ref, o_ref):
    i = pl.program_id(0)

    @pl.when(first_ref[i] == 1)
    def _():
        o_ref[...] = jnp.zeros_like(o_ref)

    @pl.when(valid_ref[i] == 1)
    def _():
        base = tile_ref[i] * COMBINE_TOKENS
        t = base + lax.broadcasted_iota(jnp.int32, (COMBINE_TOKENS, COMBINE_ROWS), 0)
        onehot = jnp.where(tok_ref[0] == t, 1.0, 0.0).astype(jnp.bfloat16)
        o_ref[...] += jnp.dot(onehot, y_ref[...], preferred_element_type=jnp.float32)


def combine_routed(ys, tok_sorted, rows):
    R, D = ys.shape
    n_tiles, n_blk = rows // COMBINE_TOKENS, R // COMBINE_ROWS
    n_items = n_tiles + n_blk
    edges = jnp.arange(n_tiles + 1, dtype=jnp.int32) * COMBINE_TOKENS
    bounds = jnp.sum((tok_sorted[None, :] < edges[:, None]).astype(jnp.int32), axis=1)
    first_blk = jnp.minimum(bounds[:-1] // COMBINE_ROWS, n_blk - 1)
    last_blk = jnp.maximum((bounds[1:] - 1) // COMBINE_ROWS, first_blk)
    start = jnp.cumsum(last_blk - first_blk + 1) - (last_blk - first_blk + 1)
    item = jnp.arange(n_items, dtype=jnp.int32)
    tile = jnp.sum((start[None, :] <= item[:, None]).astype(jnp.int32), axis=1) - 1
    blk = first_blk[tile] + item - start[tile]
    valid = (blk <= last_blk[tile]).astype(jnp.int32)
    blk = jnp.minimum(blk, n_blk - 1).astype(jnp.int32)
    first = (item == start[tile]).astype(jnp.int32)
    grid_spec = pltpu.PrefetchScalarGridSpec(
        num_scalar_prefetch=4,
        grid=(n_items,),
        in_specs=[pl.BlockSpec((COMBINE_ROWS, D), lambda i, tile, blk, first, valid: (blk[i], 0)),
                  pl.BlockSpec((1, 1, COMBINE_ROWS), lambda i, tile, blk, first, valid: (blk[i], 0, 0))],
        out_specs=pl.BlockSpec((COMBINE_TOKENS, D), lambda i, tile, blk, first, valid: (tile[i], 0)),
    )
    return pl.pallas_call(
        _combine_kernel,
        grid_spec=grid_spec,
        out_shape=jax.ShapeDtypeStruct((rows, D), jnp.float32),
        compiler_params=pltpu.CompilerParams(vmem_limit_bytes=VMEM_LIMIT),
        name="combine_routed",
    )(tile, blk, first, valid, ys, tok_sorted.reshape(n_blk, 1, COMBINE_ROWS))


ROW_TILE = 256
NORM_TILE = 512


def _res_norm_kernel(*refs, has_res, has_proj, has_router, keep_x):
    it = iter(refs)
    x_ref = next(it)
    f_ref = next(it) if has_res else None
    w_ref = next(it) if has_proj else None
    gate_ref = next(it) if has_res else None
    g_ref, shift_ref, scale1_ref = next(it), next(it), next(it)
    wr_ref = next(it) if has_router else None
    xo_ref = next(it) if keep_x else None
    h_ref = next(it)
    aff_ref = next(it) if has_router else None

    x = x_ref[...]
    if has_res:
        f = jnp.dot(f_ref[...], w_ref[...], preferred_element_type=jnp.float32) if has_proj else f_ref[...]
        x = x + gate_ref[0] * f
    if keep_x:
        xo_ref[...] = x
    y = x * lax.rsqrt(jnp.mean(x * x, axis=-1, keepdims=True) + EPS) * g_ref[...]
    h = (y * scale1_ref[0] + shift_ref[0]).astype(h_ref.dtype)
    h_ref[...] = h
    if has_router:
        logits = jnp.dot(h, wr_ref[...], preferred_element_type=jnp.float32)
        e = jnp.exp(logits - jnp.max(logits, axis=-1, keepdims=True))
        aff_ref[...] = e / jnp.sum(e, axis=-1, keepdims=True)


def res_norm(X, g, shift, scale1, n_lat, n_groups, res=None, w_router=None, out_dtype=jnp.bfloat16, rows_out=None):
    D = X.shape[1]
    keep_x = res is not None and rows_out is None
    rows = X.shape[0] if rows_out is None else rows_out
    tm = NORM_TILE
    per_sample = n_lat // tm
    grp = lambda t: (jnp.minimum(t // per_sample, n_groups - 1), 0, 0)
    row_spec = pl.BlockSpec((tm, D), lambda t: (t, 0))
    vec_spec = pl.BlockSpec((1, 1, D), grp)
    ins, in_specs = [X], [row_spec]
    has_proj = res is not None and isinstance(res[0], tuple)
    if has_proj:
        (A, W), gate = res
        ins += [A, W, gate]
        in_specs += [pl.BlockSpec((tm, A.shape[1]), lambda t: (t, 0)), pl.BlockSpec(W.shape, lambda t: (0, 0)), vec_spec]
    elif res is not None:
        F, gate = res
        ins += [F, gate]
        in_specs += [row_spec, vec_spec]
    ins += [g.reshape(1, D), shift, scale1]
    in_specs += [pl.BlockSpec((1, D), lambda t: (0, 0)), vec_spec, vec_spec]
    outs, out_specs = [], []
    if keep_x:
        outs.append(jax.ShapeDtypeStruct((rows, D), X.dtype))
        out_specs.append(row_spec)
    outs.append(jax.ShapeDtypeStruct((rows, D), out_dtype))
    out_specs.append(row_spec)
    if w_router is not None:
        E = w_router.shape[1]
        ins.append(w_router)
        in_specs.append(pl.BlockSpec((D, E), lambda t: (0, 0)))
        outs.append(jax.ShapeDtypeStruct((rows, E), jnp.float32))
        out_specs.append(pl.BlockSpec((tm, E), lambda t: (t, 0)))
    return pl.pallas_call(
        functools.partial(_res_norm_kernel, has_res=res is not None, has_proj=has_proj,
                          has_router=w_router is not None, keep_x=keep_x),
        grid=(rows // tm,),
        in_specs=in_specs,
        out_specs=out_specs,
        out_shape=outs,
        input_output_aliases={0: 0} if keep_x else {},
        compiler_params=pltpu.CompilerParams(vmem_limit_bytes=VMEM_LIMIT),
        name="res_norm",
    )(*ins)


POOL_HALO = 8


def _even_mix_kernel(hf_ref, hb_ref, o_ref, u_ref, up_ref, un_ref, hg_ref, pw_ref, ps_ref, mix_ref, xe_ref,
                     *, n_lat, lat_tiles, n_ctx):
    tm = ROW_TILE
    H, dh = MLSTM_HEADS, MLSTM_HEAD_DIM
    t = pl.program_id(0)
    hh = hf_ref[...] + hb_ref[...]
    for h in range(H):
        sl = slice(h * dh, (h + 1) * dh)
        x = hh[:, sl]
        hn = x * lax.rsqrt(jnp.mean(x * x, axis=-1, keepdims=True) + EPS) * hg_ref[:, sl]
        mix_ref[:, sl] = (hn * jax.nn.sigmoid(o_ref[:, sl])).astype(mix_ref.dtype)

    is_lat = t < lat_tiles
    n_seq = jnp.where(is_lat, n_lat, n_ctx)
    p0 = jnp.where(is_lat, (t % (n_lat // tm)) * tm, ((t - lat_tiles) % (n_ctx // tm)) * tm)
    pos_prev = p0 - POOL_HALO + lax.broadcasted_iota(jnp.int32, (POOL_HALO, 1), 0)
    pos_next = p0 + tm + lax.broadcasted_iota(jnp.int32, (POOL_HALO, 1), 0)
    xe_ref[0:POOL_HALO, :] = jnp.where(pos_prev >= 0, up_ref[...], 0.0)
    xe_ref[POOL_HALO:POOL_HALO + tm, :] = u_ref[...]
    xe_ref[POOL_HALO + tm:, :] = jnp.where(pos_next < n_seq, un_ref[...], 0.0)
    pos = p0 + lax.broadcasted_iota(jnp.int32, (tm, 1), 0)
    for gi, w in enumerate(POOL_WINDOWS):
        cs = slice(gi * POOL_GROUP_DIM, (gi + 1) * POOL_GROUP_DIM)
        acc = xe_ref[POOL_HALO - w // 2:POOL_HALO - w // 2 + tm, cs]
        for d in range(1, w):
            acc = acc + xe_ref[POOL_HALO - w // 2 + d:POOL_HALO - w // 2 + d + tm, cs]
        cnt = (jnp.minimum(pos + w // 2, n_seq) - jnp.maximum(pos - w // 2, 0)).astype(jnp.float32)
        diff = (acc / cnt - u_ref[:, cs]).astype(jnp.bfloat16)
        y = jnp.dot(diff, pw_ref[gi], preferred_element_type=jnp.float32) * ps_ref[:, cs]
        mix_ref[:, MLSTM_WIDTH + gi * POOL_GROUP_DIM:MLSTM_WIDTH + (gi + 1) * POOL_GROUP_DIM] = y.astype(mix_ref.dtype)


def even_mix(hf, hb, P, head_g, pool_w, pool_scale, B, n_lat, n_ctx):
    rows = hf.shape[0]
    tm = ROW_TILE
    hpt = tm // POOL_HALO
    last_halo = rows // POOL_HALO - 1
    wide = pl.BlockSpec((tm, MLSTM_WIDTH), lambda t: (t, 0))
    return pl.pallas_call(
        functools.partial(_even_mix_kernel, n_lat=n_lat, lat_tiles=B * n_lat // tm, n_ctx=n_ctx),
        grid=(rows // tm,),
        in_specs=[wide, wide,
                  pl.BlockSpec((tm, MLSTM_WIDTH), lambda t: (t, 3)),
                  pl.BlockSpec((tm, POOL_WIDTH), lambda t: (t, 4)),
                  pl.BlockSpec((POOL_HALO, POOL_WIDTH), lambda t: (jnp.maximum(t * hpt - 1, 0), 4)),
                  pl.BlockSpec((POOL_HALO, POOL_WIDTH), lambda t: (jnp.minimum((t + 1) * hpt, last_halo), 4)),
                  pl.BlockSpec((1, MLSTM_WIDTH), lambda t: (0, 0)),
                  pl.BlockSpec((POOL_GROUPS, POOL_GROUP_DIM, POOL_GROUP_DIM), lambda t: (0, 0, 0)),
                  pl.BlockSpec((1, POOL_WIDTH), lambda t: (0, 0))],
        out_specs=pl.BlockSpec((tm, MLSTM_WIDTH + POOL_WIDTH), lambda t: (t, 0)),
        out_shape=jax.ShapeDtypeStruct((rows, MLSTM_WIDTH + POOL_WIDTH), jnp.bfloat16),
        scratch_shapes=[pltpu.VMEM((tm + 2 * POOL_HALO, POOL_WIDTH), jnp.float32)],
        compiler_params=pltpu.CompilerParams(vmem_limit_bytes=VMEM_LIMIT),
        name="even_mix",
    )(hf, hb, P, P, P, P, head_g.reshape(1, -1), pool_w.astype(jnp.bfloat16), pool_scale.reshape(1, -1))


def mlstm_pool_mixer(h, w_in, gate_b, head_g, pool_w, pool_scale, w_out, rope, B, n_lat, n_ctx):
    H = MLSTM_HEADS
    W4 = 4 * MLSTM_WIDTH
    w_in = w_in.astype(jnp.bfloat16)
    w_main = jnp.concatenate([w_in[:, :W4], w_in[:, W4 + 4 * H:]], axis=1)
    P = matmul(h, w_main)
    G = matmul(h, w_in[:, W4:W4 + 4 * H])
    hf, hb = mlstm_bidir(rope_qk(P, rope, B, n_lat), P, gates_layout(G), gate_b, B, n_lat, n_ctx)
    mix = even_mix(hf, hb, P, head_g, pool_w, pool_scale, B, n_lat, n_ctx)
    return mix, w_out.astype(jnp.bfloat16)


def na_mixer(h, w_in, rpb, w_out, B, n_lat, n_ctx):
    qkv = matmul(h, w_in.astype(jnp.bfloat16), jnp.bfloat16)
    return na_attention(qkv, rpb, B, n_lat, n_ctx), w_out.astype(jnp.bfloat16)


def expert_choice_ffn(h, aff, w_gate, w_up, w_down, layer, B, n_lat, n_ctx):
    E = N_EXPERTS
    rows, D = h.shape
    picks = []
    for lo, n in ((0, n_lat), (B * n_lat, n_ctx)):
        a = aff[lo:lo + B * n].reshape(B, n, E)
        gate, idx = lax.top_k(jnp.swapaxes(a, 1, 2), CAPACITY_FACTOR * n // E)
        picks.append((gate, idx + lo + (jnp.arange(B) * n)[:, None, None]))
    gate = jnp.concatenate([p[0] for p in picks], axis=-1)
    idx = jnp.concatenate([p[1] for p in picks], axis=-1)
    xin = h[idx]
    y = expert_ffn(xin, w_gate, w_up, w_down, layer, gate[..., None])
    tok = idx.reshape(-1)
    order = jnp.argsort(tok)
    return combine_routed(y.reshape(-1, D)[order], tok[order], rows)


def kernel(x, c, ctx, c_ctx, ada_w, ada_b, norm_g, final_g, ev_w_in, ev_gate_b, ev_head_g, ev_pool_w,
           ev_pool_scale, ev_w_out, na_w_in, na_rpb, na_w_out, moe_w_router, moe_w_gate, moe_w_up, moe_w_down):
    B, n_lat, D = x.shape
    n_ctx = ctx.shape[1]
    G = B + 1
    X = jnp.concatenate([x.reshape(B * n_lat, D), ctx.reshape(B * n_ctx, D)], axis=0)
    rope = rope_tables(n_lat)
    cc = jax.nn.silu(jnp.concatenate([c, c_ctx[None]], axis=0)).astype(jnp.bfloat16)
    cc = jnp.pad(cc, ((0, 16 - G), (0, 0)))
    norm = functools.partial(res_norm, n_lat=n_lat, n_groups=G)
    ada = ada_modulation(cc, ada_w, ada_b)[:, :G]
    res = None
    for i in range(DEPTH):
        j = i // 2
        sh, sc, ga, shf, scf, gaf = (v[:, None, :] for v in jnp.split(ada[i], 6, axis=-1))
        if res is None:
            h = norm(X, norm_g[i, 0], sh, 1 + sc)[0]
        else:
            X, h = norm(X, norm_g[i, 0], sh, 1 + sc, res=res)
        if i % 2 == 0:
            o = mlstm_pool_mixer(h, ev_w_in[j], ev_gate_b[j], ev_head_g[j], ev_pool_w[j], ev_pool_scale[j],
                                 ev_w_out[j], rope, B, n_lat, n_ctx)
        else:
            o = na_mixer(h, na_w_in[j], na_rpb[j], na_w_out[j], B, n_lat, n_ctx)
        X, h, aff = norm(X, norm_g[i, 1], shf, 1 + scf, res=(o, ga), w_router=moe_w_router[i].astype(jnp.bfloat16))
        f = expert_choice_ffn(h, aff, moe_w_gate, moe_w_up, moe_w_down, i, B, n_lat, n_ctx)
        res = (f, gaf)
    zeros = jnp.zeros((G, 1, D), jnp.float32)
    out, = norm(X, final_g, zeros, zeros + 1, res=res, out_dtype=jnp.float32, rows_out=B * n_lat)
    return out.reshape(B, n_lat, D)
```

```python
import functools

import jax
import jax.numpy as jnp
import numpy as np
from jax import lax
from jax.experimental import pallas as pl
from jax.experimental.pallas import tpu as pltpu

D_MODEL = 2048
DEPTH = 4
GRID_W = 64
MLSTM_WIDTH = D_MODEL // 2
MLSTM_HEADS = 8
MLSTM_HEAD_DIM = MLSTM_WIDTH // MLSTM_HEADS
MLSTM_CHUNK = 256
POOL_WIDTH = D_MODEL // 2
POOL_WINDOWS = (2, 4, 8, 16)
POOL_GROUPS = len(POOL_WINDOWS)
POOL_GROUP_DIM = POOL_WIDTH // POOL_GROUPS
NA_HEADS = 16
NA_HEAD_DIM = D_MODEL // NA_HEADS
NA_WIDTH = NA_HEADS * NA_HEAD_DIM
NA_ROWS = 8
NA_COLS = 16
RPB_R = 2 * NA_ROWS - 1
RPB_C = 2 * NA_COLS - 1
N_EXPERTS = 16
EXPERT_FF = D_MODEL // 2
CAPACITY_FACTOR = 2
ROPE_BASE = 10000.0
EPS = 1e-6
MASKED = -1e30
VMEM_LIMIT = 48 * 1024 * 1024


def _mm_kernel(a_ref, w_ref, o_ref):
    o_ref[...] = jnp.dot(a_ref[...], w_ref[...], preferred_element_type=jnp.float32).astype(o_ref.dtype)


def matmul(a, w, out_dtype=jnp.float32, tm=512, tn=1024):
    M, K = a.shape
    N = w.shape[1]
    tm, tn = min(tm, M), min(tn, N)
    assert a.dtype == w.dtype == jnp.bfloat16 and M % tm == 0 and N % tn == 0
    return pl.pallas_call(
        _mm_kernel,
        grid=(N // tn, M // tm),
        in_specs=[pl.BlockSpec((tm, K), lambda j, i: (i, 0)),
                  pl.BlockSpec((K, tn), lambda j, i: (0, j))],
        out_specs=pl.BlockSpec((tm, tn), lambda j, i: (i, j)),
        out_shape=jax.ShapeDtypeStruct((M, N), out_dtype),
        compiler_params=pltpu.CompilerParams(vmem_limit_bytes=VMEM_LIMIT),
        name="matmul",
    )(a, w)


def _ada_kernel(c_ref, w_ref, b_ref, o_ref):
    o_ref[0] = jnp.dot(c_ref[...], w_ref[0].astype(jnp.bfloat16), preferred_element_type=jnp.float32) + b_ref[0]


def ada_modulation(cc, ada_w, ada_b, tn=1024):
    M, D = cc.shape
    depth, _, N = ada_w.shape
    return pl.pallas_call(
        _ada_kernel,
        grid=(depth, N // tn),
        in_specs=[pl.BlockSpec((M, D), lambda l, j: (0, 0)),
                  pl.BlockSpec((1, D, tn), lambda l, j: (l, 0, j)),
                  pl.BlockSpec((1, 1, tn), lambda l, j: (l, 0, j))],
        out_specs=pl.BlockSpec((1, M, tn), lambda l, j: (l, 0, j)),
        out_shape=jax.ShapeDtypeStruct((depth, M, N), jnp.float32),
        compiler_params=pltpu.CompilerParams(vmem_limit_bytes=VMEM_LIMIT),
        name="ada_modulation",
    )(cc, ada_w, ada_b.reshape(depth, 1, N))


NA_QROWS = 4
NA_KROWS = 12
NA_STEP_HEADS = 4
LOG2E = 1.4426950408889634


def _toeplitz(v):
    W = GRID_W
    lead = v.shape[:-1]
    y = jnp.concatenate([v, jnp.zeros(lead + (1,), v.dtype)], axis=-1)
    t = jnp.tile(y, (1,) * len(lead) + (W,))[..., : W * (2 * W - 1)].reshape(lead + (W, 2 * W - 1))
    return t[..., W - 1:]


def na_bias_tables(rpb, R):
    H = rpb.shape[0]
    W = GRID_W
    nblk = R // NA_QROWS
    r0 = np.arange(nblk) * NA_QROWS
    ws = np.clip(r0 - NA_ROWS // 2, 0, R - NA_KROWS)
    types, tmap = np.unique(ws - r0, return_inverse=True)
    dc = np.arange(-(W - 1), W)
    in_rpb = np.abs(dc + 0) <= NA_COLS - 1
    cols_idx = np.clip(dc + NA_COLS - 1, 0, RPB_C - 1)
    v = jnp.where(jnp.asarray(in_rpb), rpb[:, :, cols_idx], MASKED)
    toe = _toeplitz(v)
    c = np.arange(W)[:, None]
    kc = np.arange(W)[None, :]
    cs = np.clip(c - NA_COLS // 2, 0, W - NA_COLS)
    col_ok = (kc >= cs) & (kc < cs + NA_COLS)
    toe = jnp.where(jnp.asarray(col_ok), toe, MASKED)
    toe = jnp.concatenate([toe, jnp.full((H, 1, W, W), MASKED, toe.dtype)], axis=1)
    qi = np.arange(NA_QROWS)[:, None]
    kj = np.arange(NA_KROWS)[None, :]
    tabs = []
    for ti in range(len(types)):
        blk = int(np.nonzero(tmap == ti)[0][0])
        r = r0[blk] + qi
        kr = ws[blk] + kj
        rs = np.clip(r - NA_ROWS // 2, 0, R - NA_ROWS)
        row_ok = (kr >= rs) & (kr < rs + NA_ROWS)
        dr = np.where(row_ok, kr - r + NA_ROWS - 1, RPB_R)
        tab = toe[:, dr]
        tabs.append(tab.transpose(0, 1, 3, 2, 4).reshape(H, NA_QROWS * W, NA_KROWS * W))
    tabs.append(jnp.full((H, NA_QROWS * W, NA_KROWS * W), MASKED, jnp.float32))
    return jnp.stack(tabs, axis=1).astype(jnp.float32), tmap.astype(np.int32), ws.astype(np.int32)


def _na_kernel(tmap_ref, ws_ref, q_ref, k_ref, v_ref, kc_ref, vc_ref, bias_ref, o_ref, *, qscale):
    rb = pl.program_id(2)
    start = pl.multiple_of(ws_ref[rb] * GRID_W, GRID_W)
    dh = NA_HEAD_DIM
    tk = NA_KROWS * GRID_W
    dn = (((1,), (1,)), ((), ()))
    ones_w = jnp.ones((tk, dh), jnp.bfloat16)
    ones_c = jnp.ones((kc_ref.shape[0], dh), jnp.bfloat16)
    for hh in range(NA_STEP_HEADS):
        ls = slice(hh * dh, (hh + 1) * dh)
        q = (q_ref[:, ls].astype(jnp.float32) * qscale).astype(jnp.bfloat16)
        kw = k_ref[pl.ds(start, tk), ls]
        vw = jnp.concatenate([v_ref[pl.ds(start, tk), ls], ones_w], axis=1)
        vc = jnp.concatenate([vc_ref[:, ls], ones_c], axis=1)
        s_nb = lax.dot_general(q, kw, dn, preferred_element_type=jnp.float32) + bias_ref[hh, 0]
        s_cx = lax.dot_general(q, kc_ref[:, ls], dn, preferred_element_type=jnp.float32)
        m = jnp.maximum(jnp.max(s_nb, axis=-1, keepdims=True), jnp.max(s_cx, axis=-1, keepdims=True))
        p_nb = jnp.exp2(s_nb - m).astype(jnp.bfloat16)
        p_cx = jnp.exp2(s_cx - m).astype(jnp.bfloat16)
        ol = (jnp.dot(p_nb, vw, preferred_element_type=jnp.float32)
              + jnp.dot(p_cx, vc, preferred_element_type=jnp.float32))
        o_ref[:, ls] = (ol[:, :dh] / ol[:, dh:dh + 1]).astype(o_ref.dtype)


def na_attention(qkv, rpb, B, n_lat, n_ctx):
    rows = qkv.shape[0]
    H, dh = NA_HEADS, NA_HEAD_DIM
    HS = NA_STEP_HEADS
    HG = H // HS
    R = n_lat // GRID_W
    nblk = R // NA_QROWS
    tq = NA_QROWS * GRID_W
    tk = NA_KROWS * GRID_W
    assert n_ctx == tq and n_lat % tq == 0
    bias, tmap, ws = na_bias_tables(rpb, R)
    bias = jnp.maximum(bias * LOG2E, MASKED)
    tmap = np.concatenate([tmap, [bias.shape[1] - 1]]).astype(np.int32)
    ws = np.concatenate([ws, [0]]).astype(np.int32)

    def qblk(b, r):
        return jnp.where(r < nblk, b * nblk + r, B * nblk + b)

    grid_spec = pltpu.PrefetchScalarGridSpec(
        num_scalar_prefetch=2,
        grid=(B, HG, nblk + 1),
        in_specs=[
            pl.BlockSpec((tq, HS * dh), lambda b, h, r, tm, ws: (qblk(b, r), h)),
            pl.BlockSpec((n_lat, HS * dh), lambda b, h, r, tm, ws: (b, HG + h)),
            pl.BlockSpec((n_lat, HS * dh), lambda b, h, r, tm, ws: (b, 2 * HG + h)),
            pl.BlockSpec((n_ctx, HS * dh), lambda b, h, r, tm, ws: (B * nblk + b, HG + h)),
            pl.BlockSpec((n_ctx, HS * dh), lambda b, h, r, tm, ws: (B * nblk + b, 2 * HG + h)),
            pl.BlockSpec((HS, 1, tq, tk), lambda b, h, r, tm, ws: (h, tm[r], 0, 0)),
        ],
        out_specs=pl.BlockSpec((tq, HS * dh), lambda b, h, r, tm, ws: (qblk(b, r), h)),
    )
    return pl.pallas_call(
        functools.partial(_na_kernel, qscale=dh ** -0.5 * LOG2E),
        grid_spec=grid_spec,
        out_shape=jax.ShapeDtypeStruct((rows, H * dh), jnp.bfloat16),
        compiler_params=pltpu.CompilerParams(vmem_limit_bytes=VMEM_LIMIT),
        name="na_attention",
    )(jnp.asarray(tmap), jnp.asarray(ws), qkv, qkv, qkv, qkv, qkv, bias)


MLSTM_STEP_HEADS = 1
ROPE_TILE = 512


def rope_tables(n):
    dh = MLSTM_HEAD_DIM
    quarter = dh // 4
    t = np.arange(n)
    pos = np.stack([t // GRID_W, t % GRID_W], axis=1).astype(np.float32)
    lane = np.arange(dh)
    inv = jnp.asarray(ROPE_BASE, jnp.float32) ** (-jnp.arange(quarter, dtype=jnp.float32) / quarter)
    ang = jnp.asarray(pos)[:, lane // (dh // 2)] * inv[lane % quarter][None, :]
    cos, sin = jnp.cos(ang), jnp.sin(ang)
    first = jnp.asarray((lane % (dh // 2)) < quarter)[None, :]
    sin_up = jnp.where(first, -sin, 0.0)
    sin_dn = jnp.where(first, 0.0, sin)
    pad = lambda a, v: jnp.concatenate([a, jnp.full((ROPE_TILE, dh), v, jnp.float32)], axis=0)
    return pad(cos, 1.0), pad(sin_up, 0.0), pad(sin_dn, 0.0)


def _log_sigmoid(x):
    return jnp.minimum(x, 0.0) - jnp.log1p(jnp.exp(-jnp.abs(x)))


def _mlstm_chunk(qb, kb, v, gt, b_i, b_f, state, rev):
    L = MLSTM_CHUNK
    dh = MLSTM_HEAD_DIM
    f32, bf16 = jnp.float32, jnp.bfloat16
    Cn, m = state

    row = lax.broadcasted_iota(jnp.int32, gt.shape, 0)
    gl = jnp.where(row == 0, gt + b_i, jnp.where(row == 1, _log_sigmoid(gt + b_f), 0.0))
    glT = gl.T
    li_row, lf_row = gl[0:1], gl[1:2]
    li_col, lf_col = glT[:, 0:1], glT[:, 1:2]
    jj = lax.broadcasted_iota(jnp.int32, (L, L), 0)
    ss = lax.broadcasted_iota(jnp.int32, (L, L), 1)
    tri = (ss >= jj) if rev else (ss <= jj)
    tri_t = (ss <= jj) if rev else (ss >= jj)
    b_col = jnp.sum(jnp.where(tri, lf_row, 0.0), axis=1, keepdims=True)
    b_row = jnp.sum(jnp.where(tri_t, lf_col, 0.0), axis=0, keepdims=True)
    g = jnp.sum(lf_row, axis=1, keepdims=True)

    Dm = jnp.where(tri, b_col - b_row + li_row, MASKED)
    inter = b_col + m
    m_j = jnp.maximum(inter, jnp.max(Dm, axis=1, keepdims=True))
    nt = (((1,), (1,)), ((), ()))
    S = lax.dot_general(qb, kb, nt, preferred_element_type=f32) * jnp.exp(Dm - m_j)
    w_int = jnp.exp(inter - m_j)
    qCn = lax.dot_general(qb, Cn.astype(bf16), nt, preferred_element_type=f32)
    num = jnp.dot(S.astype(bf16), v.astype(bf16), preferred_element_type=f32) + w_int * qCn[:, :dh]
    den = jnp.sum(S, axis=1, keepdims=True) + w_int * qCn[:, dh:dh + 1]
    out = num / jnp.maximum(jnp.abs(den), jnp.exp(-m_j))

    a_row = g - b_row + li_row
    a_col = g - b_col + li_col
    m_loc = jnp.max(a_row, axis=1, keepdims=True)
    wa = jnp.exp(a_col - m_loc)
    C_loc = lax.dot_general((v * wa).astype(bf16), kb, (((0,), (0,)), ((), ())), preferred_element_type=f32)
    n_loc = jnp.sum(kb.astype(f32) * wa, axis=0, keepdims=True)
    upd = jnp.concatenate([C_loc, jnp.broadcast_to(n_loc, (8, dh))], axis=0)
    m_new = jnp.maximum(g + m, m_loc)
    sp = jnp.exp(g + m - m_new)
    sl = jnp.exp(m_loc - m_new)
    return out, (sp * Cn + sl * upd, m_new)


def _mlstm_direction(q_ref, k_ref, v_ref, g_ref, gb_ref, state_refs, o_ref, rev):
    dh = MLSTM_HEAD_DIM
    d = 2 if rev else 0
    for hh in range(MLSTM_STEP_HEADS):
        h = pl.program_id(1) * MLSTM_STEP_HEADS + hh
        ls = slice(hh * dh, (hh + 1) * dh)
        Cn_ref, m_ref = state_refs[2 * hh], state_refs[2 * hh + 1]
        out, (Cn, m) = _mlstm_chunk(q_ref[:, ls], k_ref[:, ls], v_ref[:, ls], g_ref[0, hh], gb_ref[d, h],
                                    gb_ref[d + 1, h], (Cn_ref[...], m_ref[...]), rev)
        o_ref[:, ls] = out
        Cn_ref[...] = Cn
        m_ref[...] = m


def _mlstm_kernel(gb_ref, qf, kf, vf, gf, qb, kb, vb, gbk, of, ob, *state_refs):
    @pl.when(pl.program_id(2) == 0)
    def _():
        for r in state_refs:
            r[...] = jnp.zeros_like(r)

    half = len(state_refs) // 2
    _mlstm_direction(qf, kf, vf, gf, gb_ref, state_refs[:half], of, False)
    _mlstm_direction(qb, kb, vb, gbk, gb_ref, state_refs[half:], ob, True)


def _qk_rope_kernel(a_ref, w_ref, cos_ref, su_ref, sd_ref, o_ref):
    dh = MLSTM_HEAD_DIM
    acc = jnp.dot(a_ref[...], w_ref[...], preferred_element_type=jnp.float32)
    scale = jnp.where(pl.program_id(0) == 0, 1.0, dh ** -0.5)
    cos, su, sd = cos_ref[...], su_ref[...], sd_ref[...]
    for hs in range(MLSTM_HEADS):
        ls = slice(hs * dh, (hs + 1) * dh)
        x = acc[:, ls] * scale
        y = x * cos + pltpu.roll(x, dh - dh // 4, 1) * su + pltpu.roll(x, dh // 4, 1) * sd
        o_ref[:, ls] = y.astype(o_ref.dtype)


def qk_projection(h, w_qk, rope, B, n_lat):
    rows, K = h.shape
    tm, tn = ROPE_TILE, MLSTM_WIDTH
    dh = MLSTM_HEAD_DIM
    per_sample, lat_tiles = n_lat // tm, B * n_lat // tm
    tab = pl.BlockSpec((tm, dh), lambda j, i: (jnp.where(i < lat_tiles, i % per_sample, per_sample), 0))
    return pl.pallas_call(
        _qk_rope_kernel,
        grid=(2, rows // tm),
        in_specs=[pl.BlockSpec((tm, K), lambda j, i: (i, 0)), pl.BlockSpec((K, tn), lambda j, i: (0, j)), tab, tab, tab],
        out_specs=pl.BlockSpec((tm, tn), lambda j, i: (i, j)),
        out_shape=jax.ShapeDtypeStruct((rows, 2 * tn), jnp.bfloat16),
        compiler_params=pltpu.CompilerParams(vmem_limit_bytes=VMEM_LIMIT),
        name="qk_projection",
    )(h, w_qk, *rope)


def mlstm_bidir(QK, P, GT, gate_b, B, n_lat, n_ctx):
    H, dh = MLSTM_HEADS, MLSTM_HEAD_DIM
    HS = MLSTM_STEP_HEADS
    HG = H // HS
    L = MLSTM_CHUNK
    rows = P.shape[0]
    NB, CB = n_lat // L, n_ctx // L

    def blk(b, s, rev):
        sc = (CB - 1 - s) if rev else s
        sl = (NB - 1 - (s - CB)) if rev else (s - CB)
        return jnp.where(s < CB, B * NB + b * CB + sc, b * NB + sl)

    def specs(rev):
        d = 1 if rev else 0
        return [
            pl.BlockSpec((L, HS * dh), lambda b, h, s, gb: (blk(b, s, rev), h)),
            pl.BlockSpec((L, HS * dh), lambda b, h, s, gb: (blk(b, s, rev), HG + h)),
            pl.BlockSpec((L, HS * dh), lambda b, h, s, gb: (blk(b, s, rev), h)),
            pl.BlockSpec((1, HS, 8, L), lambda b, h, s, gb: (d, h, 0, blk(b, s, rev))),
        ]

    state = [pltpu.VMEM((dh + 8, dh), jnp.float32), pltpu.VMEM((1, 1), jnp.float32)] * (2 * HS)
    grid_spec = pltpu.PrefetchScalarGridSpec(
        num_scalar_prefetch=1,
        grid=(B, HG, NB + CB),
        in_specs=specs(False) + specs(True),
        out_specs=[pl.BlockSpec((L, HS * dh), lambda b, h, s, gb: (blk(b, s, False), h)),
                   pl.BlockSpec((L, HS * dh), lambda b, h, s, gb: (blk(b, s, True), h))],
        scratch_shapes=state,
    )
    return pl.pallas_call(
        _mlstm_kernel,
        grid_spec=grid_spec,
        out_shape=[jax.ShapeDtypeStruct((rows, H * dh), jnp.float32)] * 2,
        name="mlstm_bidir",
    )(gate_b, QK, QK, P, GT, QK, QK, P, GT)


def gates_layout(G):
    rows = G.shape[0]
    g = G.T.reshape(2, 2, MLSTM_HEADS, rows).transpose(0, 2, 1, 3)
    return jnp.pad(g, ((0, 0), (0, 0), (0, 6), (0, 0)))


FFN_TF = 256
FFN_ROW_SPLIT = 2


def _ffn_kernel(x_ref, wg_ref, wu_ref, wd_ref, gate_ref, o_ref, acc_ref):
    f = pl.program_id(2)
    @pl.when(f == 0)
    def _():
        acc_ref[...] = jnp.zeros_like(acc_ref)

    wg, wu, wd = (r[0, 0].astype(jnp.bfloat16) for r in (wg_ref, wu_ref, wd_ref))
    C = acc_ref.shape[0]
    for r0 in range(0, C, C // FFN_ROW_SPLIT):
        rs = slice(r0, r0 + C // FFN_ROW_SPLIT)
        x = x_ref[0, 0, rs, :]
        a = jnp.dot(x, wg, preferred_element_type=jnp.float32)
        u = jnp.dot(x, wu, preferred_element_type=jnp.float32)
        act = (a * jax.nn.sigmoid(a) * u).astype(jnp.bfloat16)
        acc_ref[rs, :] += jnp.dot(act, wd, preferred_element_type=jnp.float32)

    @pl.when(f == pl.num_programs(2) - 1)
    def _():
        o_ref[0, 0] = (acc_ref[...] * gate_ref[0, 0]).astype(o_ref.dtype)


def expert_ffn(xin, w_gate, w_up, w_down, layer, gate):
    B, E, C, D = xin.shape
    F = w_gate.shape[-1]
    return pl.pallas_call(
        _ffn_kernel,
        grid=(E, B, F // FFN_TF),
        in_specs=[pl.BlockSpec((1, 1, C, D), lambda e, b, f: (b, e, 0, 0)),
                  pl.BlockSpec((1, 1, D, FFN_TF), lambda e, b, f: (layer, e, 0, f)),
                  pl.BlockSpec((1, 1, D, FFN_TF), lambda e, b, f: (layer, e, 0, f)),
                  pl.BlockSpec((1, 1, FFN_TF, D), lambda e, b, f: (layer, e, f, 0)),
                  pl.BlockSpec((1, 1, C, 1), lambda e, b, f: (b, e, 0, 0))],
        out_specs=pl.BlockSpec((1, 1, C, D), lambda e, b, f: (b, e, 0, 0)),
        out_shape=jax.ShapeDtypeStruct((B, E, C, D), jnp.bfloat16),
        scratch_shapes=[pltpu.VMEM((C, D), jnp.float32)],
        compiler_params=pltpu.CompilerParams(vmem_limit_bytes=VMEM_LIMIT),
        name="expert_ffn",
    )(xin, w_gate, w_up, w_down, gate)


COMBINE_TOKENS = 512
COMBINE_ROWS = 256


def _combine_kernel(tile_ref, blk_ref, first_ref, valid_ref, y_ref, tok_ref, o_ref):
    i = pl.program_id(0)

    @pl.when(first_ref[i] == 1)
    def _():
        o_ref[...] = jnp.zeros_like(o_ref)

    @pl.when(valid_ref[i] == 1)
    def _():
        base = tile_ref[i] * COMBINE_TOKENS
        t = base + lax.broadcasted_iota(jnp.int32, (COMBINE_TOKENS, COMBINE_ROWS), 0)
        onehot = jnp.where(tok_ref[0] == t, 1.0, 0.0).astype(jnp.bfloat16)
        o_ref[...] += jnp.dot(onehot, y_ref[...], preferred_element_type=jnp.float32)


def combine_routed(ys, tok_sorted, rows):
    R, D = ys.shape
    n_tiles, n_blk = rows // COMBINE_TOKENS, R // COMBINE_ROWS
    n_items = n_tiles + n_blk
    edges = jnp.arange(n_tiles + 1, dtype=jnp.int32) * COMBINE_TOKENS
    bounds = jnp.sum((tok_sorted[None, :] < edges[:, None]).astype(jnp.int32), axis=1)
    first_blk = jnp.minimum(bounds[:-1] // COMBINE_ROWS, n_blk - 1)
    last_blk = jnp.maximum((bounds[1:] - 1) // COMBINE_ROWS, first_blk)
    start = jnp.cumsum(last_blk - first_blk + 1) - (last_blk - first_blk + 1)
    item = jnp.arange(n_items, dtype=jnp.int32)
    tile = jnp.sum((start[None, :] <= item[:, None]).astype(jnp.int32), axis=1) - 1
    blk = first_blk[tile] + item - start[tile]
    valid = (blk <= last_blk[tile]).astype(jnp.int32)
    blk = jnp.minimum(blk, n_blk - 1).astype(jnp.int32)
    first = (item == start[tile]).astype(jnp.int32)
    grid_spec = pltpu.PrefetchScalarGridSpec(
        num_scalar_prefetch=4,
        grid=(n_items,),
        in_specs=[pl.BlockSpec((COMBINE_ROWS, D), lambda i, tile, blk, first, valid: (blk[i], 0)),
                  pl.BlockSpec((1, 1, COMBINE_ROWS), lambda i, tile, blk, first, valid: (blk[i], 0, 0))],
        out_specs=pl.BlockSpec((COMBINE_TOKENS, D), lambda i, tile, blk, first, valid: (tile[i], 0)),
    )
    return pl.pallas_call(
        _combine_kernel,
        grid_spec=grid_spec,
        out_shape=jax.ShapeDtypeStruct((rows, D), jnp.float32),
        compiler_params=pltpu.CompilerParams(vmem_limit_bytes=VMEM_LIMIT),
        name="combine_routed",
    )(tile, blk, first, valid, ys, tok_sorted.reshape(n_blk, 1, COMBINE_ROWS))


ROW_TILE = 256
NORM_TILE = 512


def _res_norm_kernel(*refs, has_res, has_proj, has_router, keep_x):
    it = iter(refs)
    x_ref = next(it)
    f_ref = next(it) if has_res else None
    w_ref = next(it) if has_proj else None
    gate_ref = next(it) if has_res else None
    g_ref, shift_ref, scale1_ref = next(it), next(it), next(it)
    wr_ref = next(it) if has_router else None
    xo_ref = next(it) if keep_x else None
    h_ref = next(it)
    aff_ref = next(it) if has_router else None

    x = x_ref[...]
    if has_res:
        f = jnp.dot(f_ref[...], w_ref[...], preferred_element_type=jnp.float32) if has_proj else f_ref[...]
        x = x + gate_ref[0] * f
    if keep_x:
        xo_ref[...] = x
    y = x * lax.rsqrt(jnp.mean(x * x, axis=-1, keepdims=True) + EPS) * g_ref[...]
    h = (y * scale1_ref[0] + shift_ref[0]).astype(h_ref.dtype)
    h_ref[...] = h
    if has_router:
        logits = jnp.dot(h, wr_ref[...], preferred_element_type=jnp.float32)
        e = jnp.exp(logits - jnp.max(logits, axis=-1, keepdims=True))
        aff_ref[...] = e / jnp.sum(e, axis=-1, keepdims=True)


def res_norm(X, g, shift, scale1, n_lat, n_groups, res=None, w_router=None, out_dtype=jnp.bfloat16, rows_out=None):
    D = X.shape[1]
    keep_x = res is not None and rows_out is None
    rows = X.shape[0] if rows_out is None else rows_out
    tm = NORM_TILE
    per_sample = n_lat // tm
    grp = lambda t: (jnp.minimum(t // per_sample, n_groups - 1), 0, 0)
    row_spec = pl.BlockSpec((tm, D), lambda t: (t, 0))
    vec_spec = pl.BlockSpec((1, 1, D), grp)
    ins, in_specs = [X], [row_spec]
    has_proj = res is not None and isinstance(res[0], tuple)
    if has_proj:
        (A, W), gate = res
        ins += [A, W, gate]
        in_specs += [pl.BlockSpec((tm, A.shape[1]), lambda t: (t, 0)), pl.BlockSpec(W.shape, lambda t: (0, 0)), vec_spec]
    elif res is not None:
        F, gate = res
        ins += [F, gate]
        in_specs += [row_spec, vec_spec]
    ins += [g.reshape(1, D), shift, scale1]
    in_specs += [pl.BlockSpec((1, D), lambda t: (0, 0)), vec_spec, vec_spec]
    outs, out_specs = [], []
    if keep_x:
        outs.append(jax.ShapeDtypeStruct((rows, D), X.dtype))
        out_specs.append(row_spec)
    outs.append(jax.ShapeDtypeStruct((rows, D), out_dtype))
    out_specs.append(row_spec)
    if w_router is not None:
        E = w_router.shape[1]
        ins.append(w_router)
        in_specs.append(pl.BlockSpec((D, E), lambda t: (0, 0)))
        outs.append(jax.ShapeDtypeStruct((rows, E), jnp.float32))
        out_specs.append(pl.BlockSpec((tm, E), lambda t: (t, 0)))
    return pl.pallas_call(
        functools.partial(_res_norm_kernel, has_res=res is not None, has_proj=has_proj,
                          has_router=w_router is not None, keep_x=keep_x),
        grid=(rows // tm,),
        in_specs=in_specs,
        out_specs=out_specs,
        out_shape=outs,
        input_output_aliases={0: 0} if keep_x else {},
        compiler_params=pltpu.CompilerParams(vmem_limit_bytes=VMEM_LIMIT),
        name="res_norm",
    )(*ins)


POOL_HALO = 8


def _even_mix_kernel(hf_ref, hb_ref, o_ref, u_ref, up_ref, un_ref, hg_ref, pw_ref, ps_ref, mix_ref, xe_ref,
                     *, n_lat, lat_tiles, n_ctx):
    tm = ROW_TILE
    H, dh = MLSTM_HEADS, MLSTM_HEAD_DIM
    t = pl.program_id(0)
    hh = hf_ref[...] + hb_ref[...]
    for h in range(H):
        sl = slice(h * dh, (h + 1) * dh)
        x = hh[:, sl]
        hn = x * lax.rsqrt(jnp.mean(x * x, axis=-1, keepdims=True) + EPS) * hg_ref[:, sl]
        mix_ref[:, sl] = (hn * jax.nn.sigmoid(o_ref[:, sl])).astype(mix_ref.dtype)

    is_lat = t < lat_tiles
    n_seq = jnp.where(is_lat, n_lat, n_ctx)
    p0 = jnp.where(is_lat, (t % (n_lat // tm)) * tm, ((t - lat_tiles) % (n_ctx // tm)) * tm)
    pos_prev = p0 - POOL_HALO + lax.broadcasted_iota(jnp.int32, (POOL_HALO, 1), 0)
    pos_next = p0 + tm + lax.broadcasted_iota(jnp.int32, (POOL_HALO, 1), 0)
    xe_ref[0:POOL_HALO, :] = jnp.where(pos_prev >= 0, up_ref[...], 0.0)
    xe_ref[POOL_HALO:POOL_HALO + tm, :] = u_ref[...]
    xe_ref[POOL_HALO + tm:, :] = jnp.where(pos_next < n_seq, un_ref[...], 0.0)
    pos = p0 + lax.broadcasted_iota(jnp.int32, (tm, 1), 0)
    for gi, w in enumerate(POOL_WINDOWS):
        cs = slice(gi * POOL_GROUP_DIM, (gi + 1) * POOL_GROUP_DIM)
        acc = xe_ref[POOL_HALO - w // 2:POOL_HALO - w // 2 + tm, cs]
        for d in range(1, w):
            acc = acc + xe_ref[POOL_HALO - w // 2 + d:POOL_HALO - w // 2 + d + tm, cs]
        cnt = (jnp.minimum(pos + w // 2, n_seq) - jnp.maximum(pos - w // 2, 0)).astype(jnp.float32)
        diff = (acc / cnt - u_ref[:, cs]).astype(jnp.bfloat16)
        y = jnp.dot(diff, pw_ref[gi], preferred_element_type=jnp.float32) * ps_ref[:, cs]
        mix_ref[:, MLSTM_WIDTH + gi * POOL_GROUP_DIM:MLSTM_WIDTH + (gi + 1) * POOL_GROUP_DIM] = y.astype(mix_ref.dtype)


def even_mix(hf, hb, P, head_g, pool_w, pool_scale, B, n_lat, n_ctx):
    rows = hf.shape[0]
    tm = ROW_TILE
    hpt = tm // POOL_HALO
    last_halo = rows // POOL_HALO - 1
    wide = pl.BlockSpec((tm, MLSTM_WIDTH), lambda t: (t, 0))
    return pl.pallas_call(
        functools.partial(_even_mix_kernel, n_lat=n_lat, lat_tiles=B * n_lat // tm, n_ctx=n_ctx),
        grid=(rows // tm,),
        in_specs=[wide, wide,
                  pl.BlockSpec((tm, MLSTM_WIDTH), lambda t: (t, 1)),
                  pl.BlockSpec((tm, POOL_WIDTH), lambda t: (t, 2)),
                  pl.BlockSpec((POOL_HALO, POOL_WIDTH), lambda t: (jnp.maximum(t * hpt - 1, 0), 2)),
                  pl.BlockSpec((POOL_HALO, POOL_WIDTH), lambda t: (jnp.minimum((t + 1) * hpt, last_halo), 2)),
                  pl.BlockSpec((1, MLSTM_WIDTH), lambda t: (0, 0)),
                  pl.BlockSpec((POOL_GROUPS, POOL_GROUP_DIM, POOL_GROUP_DIM), lambda t: (0, 0, 0)),
                  pl.BlockSpec((1, POOL_WIDTH), lambda t: (0, 0))],
        out_specs=pl.BlockSpec((tm, MLSTM_WIDTH + POOL_WIDTH), lambda t: (t, 0)),
        out_shape=jax.ShapeDtypeStruct((rows, MLSTM_WIDTH + POOL_WIDTH), jnp.bfloat16),
        scratch_shapes=[pltpu.VMEM((tm + 2 * POOL_HALO, POOL_WIDTH), jnp.float32)],
        compiler_params=pltpu.CompilerParams(vmem_limit_bytes=VMEM_LIMIT),
        name="even_mix",
    )(hf, hb, P, P, P, P, head_g.reshape(1, -1), pool_w.astype(jnp.bfloat16), pool_scale.reshape(1, -1))


def mlstm_pool_mixer(h, w_in, gate_b, head_g, pool_w, pool_scale, w_out, rope, B, n_lat, n_ctx):
    H = MLSTM_HEADS
    W4 = 4 * MLSTM_WIDTH
    w_in = w_in.astype(jnp.bfloat16)
    QK = qk_projection(h, w_in[:, :2 * MLSTM_WIDTH], rope, B, n_lat)
    P = matmul(h, jnp.concatenate([w_in[:, 2 * MLSTM_WIDTH:W4], w_in[:, W4 + 4 * H:]], axis=1))
    G = matmul(h, w_in[:, W4:W4 + 4 * H])
    hf, hb = mlstm_bidir(QK, P, gates_layout(G), gate_b, B, n_lat, n_ctx)
    mix = even_mix(hf, hb, P, head_g, pool_w, pool_scale, B, n_lat, n_ctx)
    return mix, w_out.astype(jnp.bfloat16)


def na_mixer(h, w_in, rpb, w_out, B, n_lat, n_ctx):
    qkv = matmul(h, w_in.astype(jnp.bfloat16), jnp.bfloat16)
    return na_attention(qkv, rpb, B, n_lat, n_ctx), w_out.astype(jnp.bfloat16)


def expert_choice_ffn(h, aff, w_gate, w_up, w_down, layer, B, n_lat, n_ctx):
    E = N_EXPERTS
    rows, D = h.shape
    picks = []
    for lo, n in ((0, n_lat), (B * n_lat, n_ctx)):
        a = aff[lo:lo + B * n].reshape(B, n, E)
        gate, idx = lax.top_k(jnp.swapaxes(a, 1, 2), CAPACITY_FACTOR * n // E)
        picks.append((gate, idx + lo + (jnp.arange(B) * n)[:, None, None]))
    gate = jnp.concatenate([p[0] for p in picks], axis=-1)
    idx = jnp.concatenate([p[1] for p in picks], axis=-1)
    xin = h[idx]
    y = expert_ffn(xin, w_gate, w_up, w_down, layer, gate[..., None])
    tok = idx.reshape(-1)
    order = jnp.argsort(tok)
    return combine_routed(y.reshape(-1, D)[order], tok[order], rows)


def kernel(x, c, ctx, c_ctx, ada_w, ada_b, norm_g, final_g, ev_w_in, ev_gate_b, ev_head_g, ev_pool_w,
           ev_pool_scale, ev_w_out, na_w_in, na_rpb, na_w_out, moe_w_router, moe_w_gate, moe_w_up, moe_w_down):
    B, n_lat, D = x.shape
    n_ctx = ctx.shape[1]
    G = B + 1
    X = jnp.concatenate([x.reshape(B * n_lat, D), ctx.reshape(B * n_ctx, D)], axis=0)
    rope = rope_tables(n_lat)
    cc = jax.nn.silu(jnp.concatenate([c, c_ctx[None]], axis=0)).astype(jnp.bfloat16)
    cc = jnp.pad(cc, ((0, 16 - G), (0, 0)))
    norm = functools.partial(res_norm, n_lat=n_lat, n_groups=G)
    ada = ada_modulation(cc, ada_w, ada_b)[:, :G]
    res = None
    for i in range(DEPTH):
        j = i // 2
        sh, sc, ga, shf, scf, gaf = (v[:, None, :] for v in jnp.split(ada[i], 6, axis=-1))
        if res is None:
            h = norm(X, norm_g[i, 0], sh, 1 + sc)[0]
        else:
            X, h = norm(X, norm_g[i, 0], sh, 1 + sc, res=res)
        if i % 2 == 0:
            o = mlstm_pool_mixer(h, ev_w_in[j], ev_gate_b[j], ev_head_g[j], ev_pool_w[j], ev_pool_scale[j],
                                 ev_w_out[j], rope, B, n_lat, n_ctx)
        else:
            o = na_mixer(h, na_w_in[j], na_rpb[j], na_w_out[j], B, n_lat, n_ctx)
        X, h, aff = norm(X, norm_g[i, 1], shf, 1 + scf, res=(o, ga), w_router=moe_w_router[i].astype(jnp.bfloat16))
        f = expert_choice_ffn(h, aff, moe_w_gate, moe_w_up, moe_w_down, i, B, n_lat, n_ctx)
        res = (f, gaf)
    zeros = jnp.zeros((G, 1, D), jnp.float32)
    out, = norm(X, final_g, zeros, zeros + 1, res=res, out_dtype=jnp.float32, rows_out=B * n_lat)
    return out.reshape(B, n_lat, D)
```

```python
import functools

import jax
import jax.numpy as jnp
import numpy as np
from jax import lax
from jax.experimental import pallas as pl
from jax.experimental.pallas import tpu as pltpu

D_MODEL = 2048
DEPTH = 4
GRID_W = 64
MLSTM_WIDTH = D_MODEL // 2
MLSTM_HEADS = 8
MLSTM_HEAD_DIM = MLSTM_WIDTH // MLSTM_HEADS
MLSTM_CHUNK = 256
POOL_WIDTH = D_MODEL // 2
POOL_WINDOWS = (2, 4, 8, 16)
POOL_GROUPS = len(POOL_WINDOWS)
POOL_GROUP_DIM = POOL_WIDTH // POOL_GROUPS
NA_HEADS = 16
NA_HEAD_DIM = D_MODEL // NA_HEADS
NA_WIDTH = NA_HEADS * NA_HEAD_DIM
NA_ROWS = 8
NA_COLS = 16
RPB_R = 2 * NA_ROWS - 1
RPB_C = 2 * NA_COLS - 1
N_EXPERTS = 16
EXPERT_FF = D_MODEL // 2
CAPACITY_FACTOR = 2
ROPE_BASE = 10000.0
EPS = 1e-6
MASKED = -1e30
VMEM_LIMIT = 48 * 1024 * 1024


def _mm_kernel(a_ref, w_ref, o_ref):
    o_ref[...] = jnp.dot(a_ref[...], w_ref[...], preferred_element_type=jnp.float32).astype(o_ref.dtype)


def matmul(a, w, out_dtype=jnp.float32, tm=512, tn=1024):
    M, K = a.shape
    N = w.shape[1]
    tm, tn = min(tm, M), min(tn, N)
    assert a.dtype == w.dtype == jnp.bfloat16 and M % tm == 0 and N % tn == 0
    return pl.pallas_call(
        _mm_kernel,
        grid=(N // tn, M // tm),
        in_specs=[pl.BlockSpec((tm, K), lambda j, i: (i, 0)),
                  pl.BlockSpec((K, tn), lambda j, i: (0, j))],
        out_specs=pl.BlockSpec((tm, tn), lambda j, i: (i, j)),
        out_shape=jax.ShapeDtypeStruct((M, N), out_dtype),
        compiler_params=pltpu.CompilerParams(vmem_limit_bytes=VMEM_LIMIT),
        name="matmul",
    )(a, w)


def _ada_kernel(c_ref, w_ref, b_ref, o_ref):
    o_ref[0] = jnp.dot(c_ref[...], w_ref[0].astype(jnp.bfloat16), preferred_element_type=jnp.float32) + b_ref[0]


def ada_modulation(cc, ada_w, ada_b, tn=1024):
    M, D = cc.shape
    depth, _, N = ada_w.shape
    return pl.pallas_call(
        _ada_kernel,
        grid=(depth, N // tn),
        in_specs=[pl.BlockSpec((M, D), lambda l, j: (0, 0)),
                  pl.BlockSpec((1, D, tn), lambda l, j: (l, 0, j)),
                  pl.BlockSpec((1, 1, tn), lambda l, j: (l, 0, j))],
        out_specs=pl.BlockSpec((1, M, tn), lambda l, j: (l, 0, j)),
        out_shape=jax.ShapeDtypeStruct((depth, M, N), jnp.float32),
        compiler_params=pltpu.CompilerParams(vmem_limit_bytes=VMEM_LIMIT),
        name="ada_modulation",
    )(cc, ada_w, ada_b.reshape(depth, 1, N))


NA_QROWS = 4
NA_KROWS = 12
NA_STEP_HEADS = 4
LOG2E = 1.4426950408889634


def _toeplitz(v):
    W = GRID_W
    lead = v.shape[:-1]
    y = jnp.concatenate([v, jnp.zeros(lead + (1,), v.dtype)], axis=-1)
    t = jnp.tile(y, (1,) * len(lead) + (W,))[..., : W * (2 * W - 1)].reshape(lead + (W, 2 * W - 1))
    return t[..., W - 1:]


def na_bias_tables(rpb, R):
    H = rpb.shape[0]
    W = GRID_W
    nblk = R // NA_QROWS
    r0 = np.arange(nblk) * NA_QROWS
    ws = np.clip(r0 - NA_ROWS // 2, 0, R - NA_KROWS)
    types, tmap = np.unique(ws - r0, return_inverse=True)
    dc = np.arange(-(W - 1), W)
    in_rpb = np.abs(dc + 0) <= NA_COLS - 1
    cols_idx = np.clip(dc + NA_COLS - 1, 0, RPB_C - 1)
    v = jnp.where(jnp.asarray(in_rpb), rpb[:, :, cols_idx], MASKED)
    toe = _toeplitz(v)
    c = np.arange(W)[:, None]
    kc = np.arange(W)[None, :]
    cs = np.clip(c - NA_COLS // 2, 0, W - NA_COLS)
    col_ok = (kc >= cs) & (kc < cs + NA_COLS)
    toe = jnp.where(jnp.asarray(col_ok), toe, MASKED)
    toe = jnp.concatenate([toe, jnp.full((H, 1, W, W), MASKED, toe.dtype)], axis=1)
    qi = np.arange(NA_QROWS)[:, None]
    kj = np.arange(NA_KROWS)[None, :]
    tabs = []
    for ti in range(len(types)):
        blk = int(np.nonzero(tmap == ti)[0][0])
        r = r0[blk] + qi
        kr = ws[blk] + kj
        rs = np.clip(r - NA_ROWS // 2, 0, R - NA_ROWS)
        row_ok = (kr >= rs) & (kr < rs + NA_ROWS)
        dr = np.where(row_ok, kr - r + NA_ROWS - 1, RPB_R)
        tab = toe[:, dr]
        tabs.append(tab.transpose(0, 1, 3, 2, 4).reshape(H, NA_QROWS * W, NA_KROWS * W))
    tabs.append(jnp.full((H, NA_QROWS * W, NA_KROWS * W), MASKED, jnp.float32))
    return jnp.stack(tabs, axis=1).astype(jnp.float32), tmap.astype(np.int32), ws.astype(np.int32)


def _na_kernel(tmap_ref, ws_ref, q_ref, k_ref, v_ref, kc_ref, vc_ref, bias_ref, o_ref, *, qscale):
    rb = pl.program_id(2)
    start = pl.multiple_of(ws_ref[rb] * GRID_W, GRID_W)
    dh = NA_HEAD_DIM
    tk = NA_KROWS * GRID_W
    dn = (((1,), (1,)), ((), ()))
    ones_w = jnp.ones((tk, dh), jnp.bfloat16)
    ones_c = jnp.ones((kc_ref.shape[0], dh), jnp.bfloat16)
    for hh in range(NA_STEP_HEADS):
        ls = slice(hh * dh, (hh + 1) * dh)
        q = (q_ref[:, ls].astype(jnp.float32) * qscale).astype(jnp.bfloat16)
        kw = k_ref[pl.ds(start, tk), ls]
        vw = jnp.concatenate([v_ref[pl.ds(start, tk), ls], ones_w], axis=1)
        vc = jnp.concatenate([vc_ref[:, ls], ones_c], axis=1)
        s_nb = lax.dot_general(q, kw, dn, preferred_element_type=jnp.float32) + bias_ref[hh, 0]
        s_cx = lax.dot_general(q, kc_ref[:, ls], dn, preferred_element_type=jnp.float32)
        m = jnp.maximum(jnp.max(s_nb, axis=-1, keepdims=True), jnp.max(s_cx, axis=-1, keepdims=True))
        p_nb = jnp.exp2(s_nb - m).astype(jnp.bfloat16)
        p_cx = jnp.exp2(s_cx - m).astype(jnp.bfloat16)
        ol = (jnp.dot(p_nb, vw, preferred_element_type=jnp.float32)
              + jnp.dot(p_cx, vc, preferred_element_type=jnp.float32))
        o_ref[:, ls] = (ol[:, :dh] / ol[:, dh:dh + 1]).astype(o_ref.dtype)


def na_attention(qkv, rpb, B, n_lat, n_ctx):
    rows = qkv.shape[0]
    H, dh = NA_HEADS, NA_HEAD_DIM
    HS = NA_STEP_HEADS
    HG = H // HS
    R = n_lat // GRID_W
    nblk = R // NA_QROWS
    tq = NA_QROWS * GRID_W
    tk = NA_KROWS * GRID_W
    assert n_ctx == tq and n_lat % tq == 0
    bias, tmap, ws = na_bias_tables(rpb, R)
    bias = jnp.maximum(bias * LOG2E, MASKED)
    tmap = np.concatenate([tmap, [bias.shape[1] - 1]]).astype(np.int32)
    ws = np.concatenate([ws, [0]]).astype(np.int32)

    def qblk(b, r):
        return jnp.where(r < nblk, b * nblk + r, B * nblk + b)

    grid_spec = pltpu.PrefetchScalarGridSpec(
        num_scalar_prefetch=2,
        grid=(B, HG, nblk + 1),
        in_specs=[
            pl.BlockSpec((tq, HS * dh), lambda b, h, r, tm, ws: (qblk(b, r), h)),
            pl.BlockSpec((n_lat, HS * dh), lambda b, h, r, tm, ws: (b, HG + h)),
            pl.BlockSpec((n_lat, HS * dh), lambda b, h, r, tm, ws: (b, 2 * HG + h)),
            pl.BlockSpec((n_ctx, HS * dh), lambda b, h, r, tm, ws: (B * nblk + b, HG + h)),
            pl.BlockSpec((n_ctx, HS * dh), lambda b, h, r, tm, ws: (B * nblk + b, 2 * HG + h)),
            pl.BlockSpec((HS, 1, tq, tk), lambda b, h, r, tm, ws: (h, tm[r], 0, 0)),
        ],
        out_specs=pl.BlockSpec((tq, HS * dh), lambda b, h, r, tm, ws: (qblk(b, r), h)),
    )
    return pl.pallas_call(
        functools.partial(_na_kernel, qscale=dh ** -0.5 * LOG2E),
        grid_spec=grid_spec,
        out_shape=jax.ShapeDtypeStruct((rows, H * dh), jnp.bfloat16),
        compiler_params=pltpu.CompilerParams(vmem_limit_bytes=VMEM_LIMIT),
        name="na_attention",
    )(jnp.asarray(tmap), jnp.asarray(ws), qkv, qkv, qkv, qkv, qkv, bias)


MLSTM_STEP_HEADS = 1
ROPE_TILE = 512


def rope_tables(n):
    dh = MLSTM_HEAD_DIM
    quarter = dh // 4
    t = np.arange(n)
    pos = np.stack([t // GRID_W, t % GRID_W], axis=1).astype(np.float32)
    lane = np.arange(dh)
    inv = jnp.asarray(ROPE_BASE, jnp.float32) ** (-jnp.arange(quarter, dtype=jnp.float32) / quarter)
    ang = jnp.asarray(pos)[:, lane // (dh // 2)] * inv[lane % quarter][None, :]
    cos, sin = jnp.cos(ang), jnp.sin(ang)
    first = jnp.asarray((lane % (dh // 2)) < quarter)[None, :]
    sin_up = jnp.where(first, -sin, 0.0)
    sin_dn = jnp.where(first, 0.0, sin)
    pad = lambda a, v: jnp.concatenate([a, jnp.full((ROPE_TILE, dh), v, jnp.float32)], axis=0)
    return pad(cos, 1.0), pad(sin_up, 0.0), pad(sin_dn, 0.0)


def _log_sigmoid(x):
    return jnp.minimum(x, 0.0) - jnp.log1p(jnp.exp(-jnp.abs(x)))


def _mlstm_chunk(qb, kb, v, gt, b_i, b_f, state, rev):
    L = MLSTM_CHUNK
    dh = MLSTM_HEAD_DIM
    f32, bf16 = jnp.float32, jnp.bfloat16
    Cn, m = state

    row = lax.broadcasted_iota(jnp.int32, gt.shape, 0)
    gl = jnp.where(row == 0, gt + b_i, jnp.where(row == 1, _log_sigmoid(gt + b_f), 0.0))
    glT = gl.T
    li_row, lf_row = gl[0:1], gl[1:2]
    li_col, lf_col = glT[:, 0:1], glT[:, 1:2]
    jj = lax.broadcasted_iota(jnp.int32, (L, L), 0)
    ss = lax.broadcasted_iota(jnp.int32, (L, L), 1)
    tri = (ss >= jj) if rev else (ss <= jj)
    tri_t = (ss <= jj) if rev else (ss >= jj)
    b_col = jnp.sum(jnp.where(tri, lf_row, 0.0), axis=1, keepdims=True)
    b_row = jnp.sum(jnp.where(tri_t, lf_col, 0.0), axis=0, keepdims=True)
    g = jnp.sum(lf_row, axis=1, keepdims=True)

    Dm = jnp.where(tri, b_col - b_row + li_row, MASKED)
    inter = b_col + m
    m_j = jnp.maximum(inter, jnp.max(Dm, axis=1, keepdims=True))
    nt = (((1,), (1,)), ((), ()))
    S = lax.dot_general(qb, kb, nt, preferred_element_type=f32) * jnp.exp(Dm - m_j)
    w_int = jnp.exp(inter - m_j)
    qCn = lax.dot_general(qb, Cn.astype(bf16), nt, preferred_element_type=f32)
    num = jnp.dot(S.astype(bf16), v.astype(bf16), preferred_element_type=f32) + w_int * qCn[:, :dh]
    den = jnp.sum(S, axis=1, keepdims=True) + w_int * qCn[:, dh:dh + 1]
    out = num / jnp.maximum(jnp.abs(den), jnp.exp(-m_j))

    a_row = g - b_row + li_row
    a_col = g - b_col + li_col
    m_loc = jnp.max(a_row, axis=1, keepdims=True)
    wa = jnp.exp(a_col - m_loc)
    C_loc = lax.dot_general((v * wa).astype(bf16), kb, (((0,), (0,)), ((), ())), preferred_element_type=f32)
    n_loc = jnp.sum(kb.astype(f32) * wa, axis=0, keepdims=True)
    upd = jnp.concatenate([C_loc, jnp.broadcast_to(n_loc, (8, dh))], axis=0)
    m_new = jnp.maximum(g + m, m_loc)
    sp = jnp.exp(g + m - m_new)
    sl = jnp.exp(m_loc - m_new)
    return out, (sp * Cn + sl * upd, m_new)


def _mlstm_direction(q_ref, k_ref, v_ref, g_ref, gb_ref, state_refs, o_ref, rev):
    dh = MLSTM_HEAD_DIM
    d = 2 if rev else 0
    for hh in range(MLSTM_STEP_HEADS):
        h = pl.program_id(1) * MLSTM_STEP_HEADS + hh
        ls = slice(hh * dh, (hh + 1) * dh)
        Cn_ref, m_ref = state_refs[2 * hh], state_refs[2 * hh + 1]
        out, (Cn, m) = _mlstm_chunk(q_ref[:, ls], k_ref[:, ls], v_ref[:, ls], g_ref[0, hh], gb_ref[d, h],
                                    gb_ref[d + 1, h], (Cn_ref[...], m_ref[...]), rev)
        o_ref[:, ls] = out
        Cn_ref[...] = Cn
        m_ref[...] = m


def _mlstm_kernel(gb_ref, qf, kf, vf, gf, qb, kb, vb, gbk, of, ob, *state_refs):
    @pl.when(pl.program_id(2) == 0)
    def _():
        for r in state_refs:
            r[...] = jnp.zeros_like(r)

    half = len(state_refs) // 2
    _mlstm_direction(qf, kf, vf, gf, gb_ref, state_refs[:half], of, False)
    _mlstm_direction(qb, kb, vb, gbk, gb_ref, state_refs[half:], ob, True)


def _qk_rope_kernel(a_ref, w_ref, cos_ref, su_ref, sd_ref, o_ref):
    dh = MLSTM_HEAD_DIM
    acc = jnp.dot(a_ref[...], w_ref[...], preferred_element_type=jnp.float32)
    scale = jnp.where(pl.program_id(0) == 0, 1.0, dh ** -0.5)
    cos, su, sd = cos_ref[...], su_ref[...], sd_ref[...]
    for hs in range(MLSTM_HEADS):
        ls = slice(hs * dh, (hs + 1) * dh)
        x = acc[:, ls] * scale
        y = x * cos + pltpu.roll(x, dh - dh // 4, 1) * su + pltpu.roll(x, dh // 4, 1) * sd
        o_ref[:, ls] = y.astype(o_ref.dtype)


def qk_projection(h, w_qk, rope, B, n_lat):
    rows, K = h.shape
    tm, tn = ROPE_TILE, MLSTM_WIDTH
    dh = MLSTM_HEAD_DIM
    per_sample, lat_tiles = n_lat // tm, B * n_lat // tm
    tab = pl.BlockSpec((tm, dh), lambda j, i: (jnp.where(i < lat_tiles, i % per_sample, per_sample), 0))
    return pl.pallas_call(
        _qk_rope_kernel,
        grid=(2, rows // tm),
        in_specs=[pl.BlockSpec((tm, K), lambda j, i: (i, 0)), pl.BlockSpec((K, tn), lambda j, i: (0, j)), tab, tab, tab],
        out_specs=pl.BlockSpec((tm, tn), lambda j, i: (i, j)),
        out_shape=jax.ShapeDtypeStruct((rows, 2 * tn), jnp.bfloat16),
        compiler_params=pltpu.CompilerParams(vmem_limit_bytes=VMEM_LIMIT),
        name="qk_projection",
    )(h, w_qk, *rope)


def mlstm_bidir(QK, P, GT, gate_b, B, n_lat, n_ctx):
    H, dh = MLSTM_HEADS, MLSTM_HEAD_DIM
    HS = MLSTM_STEP_HEADS
    HG = H // HS
    L = MLSTM_CHUNK
    rows = P.shape[0]
    NB, CB = n_lat // L, n_ctx // L

    def blk(b, s, rev):
        sc = (CB - 1 - s) if rev else s
        sl = (NB - 1 - (s - CB)) if rev else (s - CB)
        return jnp.where(s < CB, B * NB + b * CB + sc, b * NB + sl)

    def specs(rev):
        d = 1 if rev else 0
        return [
            pl.BlockSpec((L, HS * dh), lambda b, h, s, gb: (blk(b, s, rev), h)),
            pl.BlockSpec((L, HS * dh), lambda b, h, s, gb: (blk(b, s, rev), HG + h)),
            pl.BlockSpec((L, HS * dh), lambda b, h, s, gb: (blk(b, s, rev), h)),
            pl.BlockSpec((1, HS, 8, L), lambda b, h, s, gb: (d, h, 0, blk(b, s, rev))),
        ]

    state = [pltpu.VMEM((dh + 8, dh), jnp.float32), pltpu.VMEM((1, 1), jnp.float32)] * (2 * HS)
    grid_spec = pltpu.PrefetchScalarGridSpec(
        num_scalar_prefetch=1,
        grid=(B, HG, NB + CB),
        in_specs=specs(False) + specs(True),
        out_specs=[pl.BlockSpec((L, HS * dh), lambda b, h, s, gb: (blk(b, s, False), h)),
                   pl.BlockSpec((L, HS * dh), lambda b, h, s, gb: (blk(b, s, True), h))],
        scratch_shapes=state,
    )
    return pl.pallas_call(
        _mlstm_kernel,
        grid_spec=grid_spec,
        out_shape=[jax.ShapeDtypeStruct((rows, H * dh), jnp.float32)] * 2,
        name="mlstm_bidir",
    )(gate_b, QK, QK, P, GT, QK, QK, P, GT)


def gates_layout(G):
    rows = G.shape[0]
    g = G.T.reshape(2, 2, MLSTM_HEADS, rows).transpose(0, 2, 1, 3)
    return jnp.pad(g, ((0, 0), (0, 0), (0, 6), (0, 0)))


FFN_TF = 256
FFN_ROW_SPLIT = 2


def _ffn_kernel(x_ref, wg_ref, wu_ref, wd_ref, gate_ref, o_ref, acc_ref):
    f = pl.program_id(2)
    @pl.when(f == 0)
    def _():
        acc_ref[...] = jnp.zeros_like(acc_ref)

    wg, wu, wd = (r[0, 0].astype(jnp.bfloat16) for r in (wg_ref, wu_ref, wd_ref))
    C = acc_ref.shape[0]
    for r0 in range(0, C, C // FFN_ROW_SPLIT):
        rs = slice(r0, r0 + C // FFN_ROW_SPLIT)
        x = x_ref[0, 0, rs, :]
        a = jnp.dot(x, wg, preferred_element_type=jnp.float32)
        u = jnp.dot(x, wu, preferred_element_type=jnp.float32)
        act = (a * jax.nn.sigmoid(a) * u).astype(jnp.bfloat16)
        acc_ref[rs, :] += jnp.dot(act, wd, preferred_element_type=jnp.float32)

    @pl.when(f == pl.num_programs(2) - 1)
    def _():
        o_ref[0, 0] = (acc_ref[...] * gate_ref[0, 0]).astype(o_ref.dtype)


def expert_ffn(xin, w_gate, w_up, w_down, layer, gate):
    B, E, C, D = xin.shape
    F = w_gate.shape[-1]
    return pl.pallas_call(
        _ffn_kernel,
        grid=(E, B, F // FFN_TF),
        in_specs=[pl.BlockSpec((1, 1, C, D), lambda e, b, f: (b, e, 0, 0)),
                  pl.BlockSpec((1, 1, D, FFN_TF), lambda e, b, f: (layer, e, 0, f)),
                  pl.BlockSpec((1, 1, D, FFN_TF), lambda e, b, f: (layer, e, 0, f)),
                  pl.BlockSpec((1, 1, FFN_TF, D), lambda e, b, f: (layer, e, f, 0)),
                  pl.BlockSpec((1, 1, C, 1), lambda e, b, f: (b, e, 0, 0))],
        out_specs=pl.BlockSpec((1, 1, C, D), lambda e, b, f: (b, e, 0, 0)),
        out_shape=jax.ShapeDtypeStruct((B, E, C, D), jnp.bfloat16),
        scratch_shapes=[pltpu.VMEM((C, D), jnp.float32)],
        compiler_params=pltpu.CompilerParams(vmem_limit_bytes=VMEM_LIMIT),
        name="expert_ffn",
    )(xin, w_gate, w_up, w_down, gate)


COMBINE_TOKENS = 512
COMBINE_ROWS = 256


def _combine_kernel(tile_ref, blk_ref, first_ref, valid_ref, y_ref, tok_ref, o_ref):
    i = pl.program_id(0)

    @pl.when(first_ref[i] == 1)
    def _():
        o_ref[...] = jnp.zeros_like(o_ref)

    @pl.when(valid_ref[i] == 1)
    def _():
        base = tile_ref[i] * COMBINE_TOKENS
        t = base + lax.broadcasted_iota(jnp.int32, (COMBINE_TOKENS, COMBINE_ROWS), 0)
        onehot = jnp.where(tok_ref[0] == t, 1.0, 0.0).astype(jnp.bfloat16)
        o_ref[...] += jnp.dot(onehot, y_ref[...], preferred_element_type=jnp.float32)


def combine_routed(ys, tok_sorted, rows):
    R, D = ys.shape
    n_tiles, n_blk = rows // COMBINE_TOKENS, R // COMBINE_ROWS
    n_items = n_tiles + n_blk
    edges = jnp.arange(n_tiles + 1, dtype=jnp.int32) * COMBINE_TOKENS
    bounds = jnp.sum((tok_sorted[None, :] < edges[:, None]).astype(jnp.int32), axis=1)
    first_blk = jnp.minimum(bounds[:-1] // COMBINE_ROWS, n_blk - 1)
    last_blk = jnp.maximum((bounds[1:] - 1) // COMBINE_ROWS, first_blk)
    start = jnp.cumsum(last_blk - first_blk + 1) - (last_blk - first_blk + 1)
    item = jnp.arange(n_items, dtype=jnp.int32)
    tile = jnp.sum((start[None, :] <= item[:, None]).astype(jnp.int32), axis=1) - 1
    blk = first_blk[tile] + item - start[tile]
    valid = (blk <= last_blk[tile]).astype(jnp.int32)
    blk = jnp.minimum(blk, n_blk - 1).astype(jnp.int32)
    first = (item == start[tile]).astype(jnp.int32)
    grid_spec = pltpu.PrefetchScalarGridSpec(
        num_scalar_prefetch=4,
        grid=(n_items,),
        in_specs=[pl.BlockSpec((COMBINE_ROWS, D), lambda i, tile, blk, first, valid: (blk[i], 0)),
                  pl.BlockSpec((1, 1, COMBINE_ROWS), lambda i, tile, blk, first, valid: (blk[i], 0, 0))],
        out_specs=pl.BlockSpec((COMBINE_TOKENS, D), lambda i, tile, blk, first, valid: (tile[i], 0)),
    )
    return pl.pallas_call(
        _combine_kernel,
        grid_spec=grid_spec,
        out_shape=jax.ShapeDtypeStruct((rows, D), jnp.float32),
        compiler_params=pltpu.CompilerParams(vmem_limit_bytes=VMEM_LIMIT),
        name="combine_routed",
    )(tile, blk, first, valid, ys, tok_sorted.reshape(n_blk, 1, COMBINE_ROWS))


ROW_TILE = 256
NORM_TILE = 512


def _res_norm_kernel(*refs, has_res, has_proj, has_router, keep_x):
    it = iter(refs)
    x_ref = next(it)
    f_ref = next(it) if has_res else None
    w_ref = next(it) if has_proj else None
    gate_ref = next(it) if has_res else None
    g_ref, shift_ref, scale1_ref = next(it), next(it), next(it)
    wr_ref = next(it) if has_router else None
    xo_ref = next(it) if keep_x else None
    h_ref = next(it)
    aff_ref = next(it) if has_router else None

    tm = x_ref.shape[0]
    nsplit = 2 if has_proj else 1
    for r0 in range(0, tm, tm // nsplit):
        rs = slice(r0, r0 + tm // nsplit)
        x = x_ref[rs, :]
        if has_res:
            f = jnp.dot(f_ref[rs, :], w_ref[...], preferred_element_type=jnp.float32) if has_proj else f_ref[rs, :]
            x = x + gate_ref[0] * f
        if keep_x:
            xo_ref[rs, :] = x
        y = x * lax.rsqrt(jnp.mean(x * x, axis=-1, keepdims=True) + EPS) * g_ref[...]
        h = (y * scale1_ref[0] + shift_ref[0]).astype(h_ref.dtype)
        h_ref[rs, :] = h
        if has_router:
            logits = jnp.dot(h, wr_ref[...], preferred_element_type=jnp.float32)
            e = jnp.exp(logits - jnp.max(logits, axis=-1, keepdims=True))
            aff_ref[rs, :] = e / jnp.sum(e, axis=-1, keepdims=True)


def res_norm(X, g, shift, scale1, n_lat, n_groups, res=None, w_router=None, out_dtype=jnp.bfloat16, rows_out=None):
    D = X.shape[1]
    keep_x = res is not None and rows_out is None
    rows = X.shape[0] if rows_out is None else rows_out
    tm = NORM_TILE
    per_sample = n_lat // tm
    grp = lambda t: (jnp.minimum(t // per_sample, n_groups - 1), 0, 0)
    row_spec = pl.BlockSpec((tm, D), lambda t: (t, 0))
    vec_spec = pl.BlockSpec((1, 1, D), grp)
    ins, in_specs = [X], [row_spec]
    has_proj = res is not None and isinstance(res[0], tuple)
    if has_proj:
        (A, W), gate = res
        ins += [A, W, gate]
        in_specs += [pl.BlockSpec((tm, A.shape[1]), lambda t: (t, 0)), pl.BlockSpec(W.shape, lambda t: (0, 0)), vec_spec]
    elif res is not None:
        F, gate = res
        ins += [F, gate]
        in_specs += [row_spec, vec_spec]
    ins += [g.reshape(1, D), shift, scale1]
    in_specs += [pl.BlockSpec((1, D), lambda t: (0, 0)), vec_spec, vec_spec]
    outs, out_specs = [], []
    if keep_x:
        outs.append(jax.ShapeDtypeStruct((rows, D), X.dtype))
        out_specs.append(row_spec)
    outs.append(jax.ShapeDtypeStruct((rows, D), out_dtype))
    out_specs.append(row_spec)
    if w_router is not None:
        E = w_router.shape[1]
        ins.append(w_router)
        in_specs.append(pl.BlockSpec((D, E), lambda t: (0, 0)))
        outs.append(jax.ShapeDtypeStruct((rows, E), jnp.float32))
        out_specs.append(pl.BlockSpec((tm, E), lambda t: (t, 0)))
    return pl.pallas_call(
        functools.partial(_res_norm_kernel, has_res=res is not None, has_proj=has_proj,
                          has_router=w_router is not None, keep_x=keep_x),
        grid=(rows // tm,),
        in_specs=in_specs,
        out_specs=out_specs,
        out_shape=outs,
        input_output_aliases={0: 0} if keep_x else {},
        compiler_params=pltpu.CompilerParams(vmem_limit_bytes=VMEM_LIMIT),
        name="res_norm",
    )(*ins)


POOL_HALO = 8


def _even_mix_kernel(hf_ref, hb_ref, o_ref, u_ref, up_ref, un_ref, hg_ref, pw_ref, ps_ref, mix_ref, xe_ref,
                     *, n_lat, lat_tiles, n_ctx):
    tm = ROW_TILE
    H, dh = MLSTM_HEADS, MLSTM_HEAD_DIM
    t = pl.program_id(0)
    hh = hf_ref[...] + hb_ref[...]
    for h in range(H):
        sl = slice(h * dh, (h + 1) * dh)
        x = hh[:, sl]
        hn = x * lax.rsqrt(jnp.mean(x * x, axis=-1, keepdims=True) + EPS) * hg_ref[:, sl]
        mix_ref[:, sl] = (hn * jax.nn.sigmoid(o_ref[:, sl])).astype(mix_ref.dtype)

    is_lat = t < lat_tiles
    n_seq = jnp.where(is_lat, n_lat, n_ctx)
    p0 = jnp.where(is_lat, (t % (n_lat // tm)) * tm, ((t - lat_tiles) % (n_ctx // tm)) * tm)
    pos_prev = p0 - POOL_HALO + lax.broadcasted_iota(jnp.int32, (POOL_HALO, 1), 0)
    pos_next = p0 + tm + lax.broadcasted_iota(jnp.int32, (POOL_HALO, 1), 0)
    xe_ref[0:POOL_HALO, :] = jnp.where(pos_prev >= 0, up_ref[...], 0.0)
    xe_ref[POOL_HALO:POOL_HALO + tm, :] = u_ref[...]
    xe_ref[POOL_HALO + tm:, :] = jnp.where(pos_next < n_seq, un_ref[...], 0.0)
    pos = p0 + lax.broadcasted_iota(jnp.int32, (tm, 1), 0)
    for gi, w in enumerate(POOL_WINDOWS):
        cs = slice(gi * POOL_GROUP_DIM, (gi + 1) * POOL_GROUP_DIM)
        acc = xe_ref[POOL_HALO - w // 2:POOL_HALO - w // 2 + tm, cs]
        for d in range(1, w):
            acc = acc + xe_ref[POOL_HALO - w // 2 + d:POOL_HALO - w // 2 + d + tm, cs]
        cnt = (jnp.minimum(pos + w // 2, n_seq) - jnp.maximum(pos - w // 2, 0)).astype(jnp.float32)
        diff = (acc / cnt - u_ref[:, cs]).astype(jnp.bfloat16)
        y = jnp.dot(diff, pw_ref[gi], preferred_element_type=jnp.float32) * ps_ref[:, cs]
        mix_ref[:, MLSTM_WIDTH + gi * POOL_GROUP_DIM:MLSTM_WIDTH + (gi + 1) * POOL_GROUP_DIM] = y.astype(mix_ref.dtype)


def even_mix(hf, hb, P, head_g, pool_w, pool_scale, B, n_lat, n_ctx):
    rows = hf.shape[0]
    tm = ROW_TILE
    hpt = tm // POOL_HALO
    last_halo = rows // POOL_HALO - 1
    wide = pl.BlockSpec((tm, MLSTM_WIDTH), lambda t: (t, 0))
    return pl.pallas_call(
        functools.partial(_even_mix_kernel, n_lat=n_lat, lat_tiles=B * n_lat // tm, n_ctx=n_ctx),
        grid=(rows // tm,),
        in_specs=[wide, wide,
                  pl.BlockSpec((tm, MLSTM_WIDTH), lambda t: (t, 1)),
                  pl.BlockSpec((tm, POOL_WIDTH), lambda t: (t, 2)),
                  pl.BlockSpec((POOL_HALO, POOL_WIDTH), lambda t: (jnp.maximum(t * hpt - 1, 0), 2)),
                  pl.BlockSpec((POOL_HALO, POOL_WIDTH), lambda t: (jnp.minimum((t + 1) * hpt, last_halo), 2)),
                  pl.BlockSpec((1, MLSTM_WIDTH), lambda t: (0, 0)),
                  pl.BlockSpec((POOL_GROUPS, POOL_GROUP_DIM, POOL_GROUP_DIM), lambda t: (0, 0, 0)),
                  pl.BlockSpec((1, POOL_WIDTH), lambda t: (0, 0))],
        out_specs=pl.BlockSpec((tm, MLSTM_WIDTH + POOL_WIDTH), lambda t: (t, 0)),
        out_shape=jax.ShapeDtypeStruct((rows, MLSTM_WIDTH + POOL_WIDTH), jnp.bfloat16),
        scratch_shapes=[pltpu.VMEM((tm + 2 * POOL_HALO, POOL_WIDTH), jnp.float32)],
        compiler_params=pltpu.CompilerParams(vmem_limit_bytes=VMEM_LIMIT),
        name="even_mix",
    )(hf, hb, P, P, P, P, head_g.reshape(1, -1), pool_w.astype(jnp.bfloat16), pool_scale.reshape(1, -1))


def mlstm_pool_mixer(h, w_in, gate_b, head_g, pool_w, pool_scale, w_out, rope, B, n_lat, n_ctx):
    H = MLSTM_HEADS
    W4 = 4 * MLSTM_WIDTH
    w_in = w_in.astype(jnp.bfloat16)
    QK = qk_projection(h, w_in[:, :2 * MLSTM_WIDTH], rope, B, n_lat)
    P = matmul(h, jnp.concatenate([w_in[:, 2 * MLSTM_WIDTH:W4], w_in[:, W4 + 4 * H:]], axis=1))
    G = matmul(h, w_in[:, W4:W4 + 4 * H])
    hf, hb = mlstm_bidir(QK, P, gates_layout(G), gate_b, B, n_lat, n_ctx)
    mix = even_mix(hf, hb, P, head_g, pool_w, pool_scale, B, n_lat, n_ctx)
    return mix, w_out.astype(jnp.bfloat16)


def na_mixer(h, w_in, rpb, w_out, B, n_lat, n_ctx):
    qkv = matmul(h, w_in.astype(jnp.bfloat16), jnp.bfloat16)
    return na_attention(qkv, rpb, B, n_lat, n_ctx), w_out.astype(jnp.bfloat16)


def expert_choice_ffn(h, aff, w_gate, w_up, w_down, layer, B, n_lat, n_ctx):
    E = N_EXPERTS
    rows, D = h.shape
    picks = []
    for lo, n in ((0, n_lat), (B * n_lat, n_ctx)):
        a = aff[lo:lo + B * n].reshape(B, n, E)
        gate, idx = lax.top_k(jnp.swapaxes(a, 1, 2), CAPACITY_FACTOR * n // E)
        picks.append((gate, idx + lo + (jnp.arange(B) * n)[:, None, None]))
    gate = jnp.concatenate([p[0] for p in picks], axis=-1)
    idx = jnp.concatenate([p[1] for p in picks], axis=-1)
    xin = h[idx]
    y = expert_ffn(xin, w_gate, w_up, w_down, layer, gate[..., None])
    tok = idx.reshape(-1)
    order = jnp.argsort(tok)
    return combine_routed(y.reshape(-1, D)[order], tok[order], rows)


def kernel(x, c, ctx, c_ctx, ada_w, ada_b, norm_g, final_g, ev_w_in, ev_gate_b, ev_head_g, ev_pool_w,
           ev_pool_scale, ev_w_out, na_w_in, na_rpb, na_w_out, moe_w_router, moe_w_gate, moe_w_up, moe_w_down):
    B, n_lat, D = x.shape
    n_ctx = ctx.shape[1]
    G = B + 1
    X = jnp.concatenate([x.reshape(B * n_lat, D), ctx.reshape(B * n_ctx, D)], axis=0)
    rope = rope_tables(n_lat)
    cc = jax.nn.silu(jnp.concatenate([c, c_ctx[None]], axis=0)).astype(jnp.bfloat16)
    cc = jnp.pad(cc, ((0, 16 - G), (0, 0)))
    norm = functools.partial(res_norm, n_lat=n_lat, n_groups=G)
    ada = ada_modulation(cc, ada_w, ada_b)[:, :G]
    res = None
    for i in range(DEPTH):
        j = i // 2
        sh, sc, ga, shf, scf, gaf = (v[:, None, :] for v in jnp.split(ada[i], 6, axis=-1))
        if res is None:
            h = norm(X, norm_g[i, 0], sh, 1 + sc)[0]
        else:
            X, h = norm(X, norm_g[i, 0], sh, 1 + sc, res=res)
        if i % 2 == 0:
            o = mlstm_pool_mixer(h, ev_w_in[j], ev_gate_b[j], ev_head_g[j], ev_pool_w[j], ev_pool_scale[j],
                                 ev_w_out[j], rope, B, n_lat, n_ctx)
        else:
            o = na_mixer(h, na_w_in[j], na_rpb[j], na_w_out[j], B, n_lat, n_ctx)
        X, h, aff = norm(X, norm_g[i, 1], shf, 1 + scf, res=(o, ga), w_router=moe_w_router[i].astype(jnp.bfloat16))
        f = expert_choice_ffn(h, aff, moe_w_gate, moe_w_up, moe_w_down, i, B, n_lat, n_ctx)
        res = (f, gaf)
    zeros = jnp.zeros((G, 1, D), jnp.float32)
    out, = norm(X, final_g, zeros, zeros + 1, res=res, out_dtype=jnp.float32, rows_out=B * n_lat)
    return out.reshape(B, n_lat, D)
```

```python
import functools

import jax
import jax.numpy as jnp
import numpy as np
from jax import lax
from jax.experimental import pallas as pl
from jax.experimental.pallas import tpu as pltpu

D_MODEL = 2048
DEPTH = 4
GRID_W = 64
MLSTM_WIDTH = D_MODEL // 2
MLSTM_HEADS = 8
MLSTM_HEAD_DIM = MLSTM_WIDTH // MLSTM_HEADS
MLSTM_CHUNK = 256
POOL_WIDTH = D_MODEL // 2
POOL_WINDOWS = (2, 4, 8, 16)
POOL_GROUPS = len(POOL_WINDOWS)
POOL_GROUP_DIM = POOL_WIDTH // POOL_GROUPS
NA_HEADS = 16
NA_HEAD_DIM = D_MODEL // NA_HEADS
NA_ROWS = 8
NA_COLS = 16
RPB_R = 2 * NA_ROWS - 1
RPB_C = 2 * NA_COLS - 1
N_EXPERTS = 16
CAPACITY_FACTOR = 2
ROPE_BASE = 10000.0
EPS = 1e-6
MASKED = -1e30
VMEM_LIMIT = 48 * 1024 * 1024


def _mm_kernel(a_ref, w_ref, o_ref):
    o_ref[...] = jnp.dot(a_ref[...], w_ref[...], preferred_element_type=jnp.float32).astype(o_ref.dtype)


def matmul(a, w, out_dtype=jnp.float32, tm=512, tn=1024):
    M, K = a.shape
    N = w.shape[1]
    tm, tn = min(tm, M), min(tn, N)
    assert a.dtype == w.dtype == jnp.bfloat16 and M % tm == 0 and N % tn == 0
    return pl.pallas_call(
        _mm_kernel,
        grid=(N // tn, M // tm),
        in_specs=[pl.BlockSpec((tm, K), lambda j, i: (i, 0)),
                  pl.BlockSpec((K, tn), lambda j, i: (0, j))],
        out_specs=pl.BlockSpec((tm, tn), lambda j, i: (i, j)),
        out_shape=jax.ShapeDtypeStruct((M, N), out_dtype),
        compiler_params=pltpu.CompilerParams(vmem_limit_bytes=VMEM_LIMIT),
        name="matmul",
    )(a, w)


def _ada_kernel(c_ref, w_ref, b_ref, o_ref):
    o_ref[0] = jnp.dot(c_ref[...], w_ref[0].astype(jnp.bfloat16), preferred_element_type=jnp.float32) + b_ref[0]


def ada_modulation(cc, ada_w, ada_b, tn=1024):
    M, D = cc.shape
    depth, _, N = ada_w.shape
    return pl.pallas_call(
        _ada_kernel,
        grid=(depth, N // tn),
        in_specs=[pl.BlockSpec((M, D), lambda l, j: (0, 0)),
                  pl.BlockSpec((1, D, tn), lambda l, j: (l, 0, j)),
                  pl.BlockSpec((1, 1, tn), lambda l, j: (l, 0, j))],
        out_specs=pl.BlockSpec((1, M, tn), lambda l, j: (l, 0, j)),
        out_shape=jax.ShapeDtypeStruct((depth, M, N), jnp.float32),
        compiler_params=pltpu.CompilerParams(vmem_limit_bytes=VMEM_LIMIT),
        name="ada_modulation",
    )(cc, ada_w, ada_b.reshape(depth, 1, N))


NA_QROWS = 4
NA_KROWS = 12
NA_STEP_HEADS = 4
LOG2E = 1.4426950408889634


def _toeplitz(v):
    W = GRID_W
    lead = v.shape[:-1]
    y = jnp.concatenate([v, jnp.zeros(lead + (1,), v.dtype)], axis=-1)
    t = jnp.tile(y, (1,) * len(lead) + (W,))[..., : W * (2 * W - 1)].reshape(lead + (W, 2 * W - 1))
    return t[..., W - 1:]


def na_bias_tables(rpb, R):
    H = rpb.shape[0]
    W = GRID_W
    nblk = R // NA_QROWS
    r0 = np.arange(nblk) * NA_QROWS
    ws = np.clip(r0 - NA_ROWS // 2, 0, R - NA_KROWS)
    types, tmap = np.unique(ws - r0, return_inverse=True)
    dc = np.arange(-(W - 1), W)
    in_rpb = np.abs(dc + 0) <= NA_COLS - 1
    cols_idx = np.clip(dc + NA_COLS - 1, 0, RPB_C - 1)
    v = jnp.where(jnp.asarray(in_rpb), rpb[:, :, cols_idx], MASKED)
    toe = _toeplitz(v)
    c = np.arange(W)[:, None]
    kc = np.arange(W)[None, :]
    cs = np.clip(c - NA_COLS // 2, 0, W - NA_COLS)
    col_ok = (kc >= cs) & (kc < cs + NA_COLS)
    toe = jnp.where(jnp.asarray(col_ok), toe, MASKED)
    toe = jnp.concatenate([toe, jnp.full((H, 1, W, W), MASKED, toe.dtype)], axis=1)
    qi = np.arange(NA_QROWS)[:, None]
    kj = np.arange(NA_KROWS)[None, :]
    tabs = []
    for ti in range(len(types)):
        blk = int(np.nonzero(tmap == ti)[0][0])
        r = r0[blk] + qi
        kr = ws[blk] + kj
        rs = np.clip(r - NA_ROWS // 2, 0, R - NA_ROWS)
        row_ok = (kr >= rs) & (kr < rs + NA_ROWS)
        dr = np.where(row_ok, kr - r + NA_ROWS - 1, RPB_R)
        tab = toe[:, dr]
        tabs.append(tab.transpose(0, 1, 3, 2, 4).reshape(H, NA_QROWS * W, NA_KROWS * W))
    tabs.append(jnp.full((H, NA_QROWS * W, NA_KROWS * W), MASKED, jnp.float32))
    return jnp.stack(tabs, axis=1).astype(jnp.float32), tmap.astype(np.int32), ws.astype(np.int32)


def _na_kernel(tmap_ref, ws_ref, q_ref, k_ref, v_ref, kc_ref, vc_ref, bias_ref, o_ref, *, qscale):
    rb = pl.program_id(2)
    start = pl.multiple_of(ws_ref[rb] * GRID_W, GRID_W)
    dh = NA_HEAD_DIM
    tk = NA_KROWS * GRID_W
    dn = (((1,), (1,)), ((), ()))
    ones_w = jnp.ones((tk, dh), jnp.bfloat16)
    ones_c = jnp.ones((kc_ref.shape[0], dh), jnp.bfloat16)
    for hh in range(NA_STEP_HEADS):
        ls = slice(hh * dh, (hh + 1) * dh)
        q = (q_ref[:, ls].astype(jnp.float32) * qscale).astype(jnp.bfloat16)
        kw = k_ref[pl.ds(start, tk), ls]
        vw = jnp.concatenate([v_ref[pl.ds(start, tk), ls], ones_w], axis=1)
        vc = jnp.concatenate([vc_ref[:, ls], ones_c], axis=1)
        s_nb = lax.dot_general(q, kw, dn, preferred_element_type=jnp.float32) + bias_ref[hh, 0]
        s_cx = lax.dot_general(q, kc_ref[:, ls], dn, preferred_element_type=jnp.float32)
        m = jnp.maximum(jnp.max(s_nb, axis=-1, keepdims=True), jnp.max(s_cx, axis=-1, keepdims=True))
        p_nb = jnp.exp2(s_nb - m).astype(jnp.bfloat16)
        p_cx = jnp.exp2(s_cx - m).astype(jnp.bfloat16)
        ol = (jnp.dot(p_nb, vw, preferred_element_type=jnp.float32)
              + jnp.dot(p_cx, vc, preferred_element_type=jnp.float32))
        o_ref[:, ls] = (ol[:, :dh] / ol[:, dh:dh + 1]).astype(o_ref.dtype)


def na_attention(qkv, rpb, B, n_lat, n_ctx):
    rows = qkv.shape[0]
    H, dh = NA_HEADS, NA_HEAD_DIM
    HS = NA_STEP_HEADS
    HG = H // HS
    R = n_lat // GRID_W
    nblk = R // NA_QROWS
    tq = NA_QROWS * GRID_W
    tk = NA_KROWS * GRID_W
    assert n_ctx == tq and n_lat % tq == 0
    bias, tmap, ws = na_bias_tables(rpb, R)
    bias = jnp.maximum(bias * LOG2E, MASKED)
    tmap = np.concatenate([tmap, [bias.shape[1] - 1]]).astype(np.int32)
    ws = np.concatenate([ws, [0]]).astype(np.int32)

    def qblk(b, r):
        return jnp.where(r < nblk, b * nblk + r, B * nblk + b)

    grid_spec = pltpu.PrefetchScalarGridSpec(
        num_scalar_prefetch=2,
        grid=(B, HG, nblk + 1),
        in_specs=[
            pl.BlockSpec((tq, HS * dh), lambda b, h, r, tm, ws: (qblk(b, r), h)),
            pl.BlockSpec((n_lat, HS * dh), lambda b, h, r, tm, ws: (b, HG + h)),
            pl.BlockSpec((n_lat, HS * dh), lambda b, h, r, tm, ws: (b, 2 * HG + h)),
            pl.BlockSpec((n_ctx, HS * dh), lambda b, h, r, tm, ws: (B * nblk + b, HG + h)),
            pl.BlockSpec((n_ctx, HS * dh), lambda b, h, r, tm, ws: (B * nblk + b, 2 * HG + h)),
            pl.BlockSpec((HS, 1, tq, tk), lambda b, h, r, tm, ws: (h, tm[r], 0, 0)),
        ],
        out_specs=pl.BlockSpec((tq, HS * dh), lambda b, h, r, tm, ws: (qblk(b, r), h)),
    )
    return pl.pallas_call(
        functools.partial(_na_kernel, qscale=dh ** -0.5 * LOG2E),
        grid_spec=grid_spec,
        out_shape=jax.ShapeDtypeStruct((rows, H * dh), jnp.bfloat16),
        compiler_params=pltpu.CompilerParams(vmem_limit_bytes=VMEM_LIMIT),
        name="na_attention",
    )(jnp.asarray(tmap), jnp.asarray(ws), qkv, qkv, qkv, qkv, qkv, bias)


MLSTM_STEP_HEADS = 1
ROPE_TILE = 512


def rope_tables(n):
    dh = MLSTM_HEAD_DIM
    quarter = dh // 4
    t = np.arange(n)
    pos = np.stack([t // GRID_W, t % GRID_W], axis=1).astype(np.float32)
    lane = np.arange(dh)
    inv = jnp.asarray(ROPE_BASE, jnp.float32) ** (-jnp.arange(quarter, dtype=jnp.float32) / quarter)
    ang = jnp.asarray(pos)[:, lane // (dh // 2)] * inv[lane % quarter][None, :]
    cos, sin = jnp.cos(ang), jnp.sin(ang)
    first = jnp.asarray((lane % (dh // 2)) < quarter)[None, :]
    sin_up = jnp.where(first, -sin, 0.0)
    sin_dn = jnp.where(first, 0.0, sin)
    pad = lambda a, v: jnp.concatenate([a, jnp.full((ROPE_TILE, dh), v, jnp.float32)], axis=0)
    return pad(cos, 1.0), pad(sin_up, 0.0), pad(sin_dn, 0.0)


def _log_sigmoid(x):
    return jnp.minimum(x, 0.0) - jnp.log1p(jnp.exp(-jnp.abs(x)))


def _mlstm_chunk(qb, kb, v, gt, b_i, b_f, state, rev):
    L = MLSTM_CHUNK
    dh = MLSTM_HEAD_DIM
    f32, bf16 = jnp.float32, jnp.bfloat16
    Cn, m = state

    row = lax.broadcasted_iota(jnp.int32, gt.shape, 0)
    gl = jnp.where(row == 0, gt + b_i, jnp.where(row == 1, _log_sigmoid(gt + b_f), 0.0))
    glT = gl.T
    li_row, lf_row = gl[0:1], gl[1:2]
    li_col, lf_col = glT[:, 0:1], glT[:, 1:2]
    jj = lax.broadcasted_iota(jnp.int32, (L, L), 0)
    ss = lax.broadcasted_iota(jnp.int32, (L, L), 1)
    tri = (ss >= jj) if rev else (ss <= jj)
    tri_t = (ss <= jj) if rev else (ss >= jj)
    b_col = jnp.sum(jnp.where(tri, lf_row, 0.0), axis=1, keepdims=True)
    b_row = jnp.sum(jnp.where(tri_t, lf_col, 0.0), axis=0, keepdims=True)
    g = jnp.sum(lf_row, axis=1, keepdims=True)

    Dm = jnp.where(tri, b_col - b_row + li_row, MASKED)
    inter = b_col + m
    m_j = jnp.maximum(inter, jnp.max(Dm, axis=1, keepdims=True))
    nt = (((1,), (1,)), ((), ()))
    S = lax.dot_general(qb, kb, nt, preferred_element_type=f32) * jnp.exp(Dm - m_j)
    w_int = jnp.exp(inter - m_j)
    qCn = lax.dot_general(qb, Cn.astype(bf16), nt, preferred_element_type=f32)
    num = jnp.dot(S.astype(bf16), v.astype(bf16), preferred_element_type=f32) + w_int * qCn[:, :dh]
    den = jnp.sum(S, axis=1, keepdims=True) + w_int * qCn[:, dh:dh + 1]
    out = num / jnp.maximum(jnp.abs(den), jnp.exp(-m_j))

    a_row = g - b_row + li_row
    a_col = g - b_col + li_col
    m_loc = jnp.max(a_row, axis=1, keepdims=True)
    wa = jnp.exp(a_col - m_loc)
    C_loc = lax.dot_general((v * wa).astype(bf16), kb, (((0,), (0,)), ((), ())), preferred_element_type=f32)
    n_loc = jnp.sum(kb.astype(f32) * wa, axis=0, keepdims=True)
    upd = jnp.concatenate([C_loc, jnp.broadcast_to(n_loc, (8, dh))], axis=0)
    m_new = jnp.maximum(g + m, m_loc)
    sp = jnp.exp(g + m - m_new)
    sl = jnp.exp(m_loc - m_new)
    return out, (sp * Cn + sl * upd, m_new)


def _mlstm_direction(q_ref, k_ref, v_ref, g_ref, gb_ref, state_refs, o_ref, rev):
    dh = MLSTM_HEAD_DIM
    d = 2 if rev else 0
    for hh in range(MLSTM_STEP_HEADS):
        h = pl.program_id(1) * MLSTM_STEP_HEADS + hh
        ls = slice(hh * dh, (hh + 1) * dh)
        Cn_ref, m_ref = state_refs[2 * hh], state_refs[2 * hh + 1]
        out, (Cn, m) = _mlstm_chunk(q_ref[:, ls], k_ref[:, ls], v_ref[:, ls], g_ref[0, hh], gb_ref[d, h],
                                    gb_ref[d + 1, h], (Cn_ref[...], m_ref[...]), rev)
        o_ref[:, ls] = out
        Cn_ref[...] = Cn
        m_ref[...] = m


def _mlstm_kernel(gb_ref, qf, kf, vf, gf, qb, kb, vb, gbk, of, ob, *state_refs):
    @pl.when(pl.program_id(2) == 0)
    def _():
        for r in state_refs:
            r[...] = jnp.zeros_like(r)

    half = len(state_refs) // 2
    _mlstm_direction(qf, kf, vf, gf, gb_ref, state_refs[:half], of, False)
    _mlstm_direction(qb, kb, vb, gbk, gb_ref, state_refs[half:], ob, True)


def _qk_rope_kernel(a_ref, w_ref, cos_ref, su_ref, sd_ref, o_ref):
    dh = MLSTM_HEAD_DIM
    scale = jnp.where(pl.program_id(0) == 0, 1.0, dh ** -0.5)
    half = a_ref.shape[0] // 2
    for r0 in (0, half):
        rs = slice(r0, r0 + half)
        acc = jnp.dot(a_ref[rs, :], w_ref[...], preferred_element_type=jnp.float32)
        cos, su, sd = cos_ref[rs, :], su_ref[rs, :], sd_ref[rs, :]
        for hs in range(MLSTM_HEADS):
            ls = slice(hs * dh, (hs + 1) * dh)
            x = acc[:, ls] * scale
            y = x * cos + pltpu.roll(x, dh - dh // 4, 1) * su + pltpu.roll(x, dh // 4, 1) * sd
            o_ref[rs, ls] = y.astype(o_ref.dtype)


def qk_projection(h, w_qk, rope, B, n_lat):
    rows, K = h.shape
    tm, tn = ROPE_TILE, MLSTM_WIDTH
    dh = MLSTM_HEAD_DIM
    per_sample, lat_tiles = n_lat // tm, B * n_lat // tm
    tab = pl.BlockSpec((tm, dh), lambda j, i: (jnp.where(i < lat_tiles, i % per_sample, per_sample), 0))
    return pl.pallas_call(
        _qk_rope_kernel,
        grid=(2, rows // tm),
        in_specs=[pl.BlockSpec((tm, K), lambda j, i: (i, 0)), pl.BlockSpec((K, tn), lambda j, i: (0, j)), tab, tab, tab],
        out_specs=pl.BlockSpec((tm, tn), lambda j, i: (i, j)),
        out_shape=jax.ShapeDtypeStruct((rows, 2 * tn), jnp.bfloat16),
        compiler_params=pltpu.CompilerParams(vmem_limit_bytes=VMEM_LIMIT),
        name="qk_projection",
    )(h, w_qk, *rope)


def mlstm_bidir(QK, P, GT, gate_b, B, n_lat, n_ctx):
    H, dh = MLSTM_HEADS, MLSTM_HEAD_DIM
    HS = MLSTM_STEP_HEADS
    HG = H // HS
    L = MLSTM_CHUNK
    rows = P.shape[0]
    NB, CB = n_lat // L, n_ctx // L

    def blk(b, s, rev):
        sc = (CB - 1 - s) if rev else s
        sl = (NB - 1 - (s - CB)) if rev else (s - CB)
        return jnp.where(s < CB, B * NB + b * CB + sc, b * NB + sl)

    def specs(rev):
        d = 1 if rev else 0
        return [
            pl.BlockSpec((L, HS * dh), lambda b, h, s, gb: (blk(b, s, rev), h)),
            pl.BlockSpec((L, HS * dh), lambda b, h, s, gb: (blk(b, s, rev), HG + h)),
            pl.BlockSpec((L, HS * dh), lambda b, h, s, gb: (blk(b, s, rev), h)),
            pl.BlockSpec((1, HS, 8, L), lambda b, h, s, gb: (d, h, 0, blk(b, s, rev))),
        ]

    state = [pltpu.VMEM((dh + 8, dh), jnp.float32), pltpu.VMEM((1, 1), jnp.float32)] * (2 * HS)
    grid_spec = pltpu.PrefetchScalarGridSpec(
        num_scalar_prefetch=1,
        grid=(B, HG, NB + CB),
        in_specs=specs(False) + specs(True),
        out_specs=[pl.BlockSpec((L, HS * dh), lambda b, h, s, gb: (blk(b, s, False), h)),
                   pl.BlockSpec((L, HS * dh), lambda b, h, s, gb: (blk(b, s, True), h))],
        scratch_shapes=state,
    )
    return pl.pallas_call(
        _mlstm_kernel,
        grid_spec=grid_spec,
        out_shape=[jax.ShapeDtypeStruct((rows, H * dh), jnp.float32)] * 2,
        name="mlstm_bidir",
    )(gate_b, QK, QK, P, GT, QK, QK, P, GT)


def gates_layout(G):
    rows = G.shape[0]
    g = G.T.reshape(2, 2, MLSTM_HEADS, rows).transpose(0, 2, 1, 3)
    return jnp.pad(g, ((0, 0), (0, 0), (0, 6), (0, 0)))


FFN_TF = 256
FFN_ROW_SPLIT = 2


def _ffn_kernel(x_ref, wg_ref, wu_ref, wd_ref, gate_ref, o_ref, acc_ref):
    f = pl.program_id(2)
    @pl.when(f == 0)
    def _():
        acc_ref[...] = jnp.zeros_like(acc_ref)

    wg, wu, wd = (r[0, 0].astype(jnp.bfloat16) for r in (wg_ref, wu_ref, wd_ref))
    C = acc_ref.shape[0]
    for r0 in range(0, C, C // FFN_ROW_SPLIT):
        rs = slice(r0, r0 + C // FFN_ROW_SPLIT)
        x = x_ref[0, 0, rs, :]
        a = jnp.dot(x, wg, preferred_element_type=jnp.float32)
        u = jnp.dot(x, wu, preferred_element_type=jnp.float32)
        act = (a * jax.nn.sigmoid(a) * u).astype(jnp.bfloat16)
        acc_ref[rs, :] += jnp.dot(act, wd, preferred_element_type=jnp.float32)

    @pl.when(f == pl.num_programs(2) - 1)
    def _():
        o_ref[0, 0] = (acc_ref[...] * gate_ref[0, 0]).astype(o_ref.dtype)


def expert_ffn(xin, w_gate, w_up, w_down, layer, gate):
    B, E, C, D = xin.shape
    F = w_gate.shape[-1]
    return pl.pallas_call(
        _ffn_kernel,
        grid=(E, B, F // FFN_TF),
        in_specs=[pl.BlockSpec((1, 1, C, D), lambda e, b, f: (b, e, 0, 0)),
                  pl.BlockSpec((1, 1, D, FFN_TF), lambda e, b, f: (layer, e, 0, f)),
                  pl.BlockSpec((1, 1, D, FFN_TF), lambda e, b, f: (layer, e, 0, f)),
                  pl.BlockSpec((1, 1, FFN_TF, D), lambda e, b, f: (layer, e, f, 0)),
                  pl.BlockSpec((1, 1, C, 1), lambda e, b, f: (b, e, 0, 0))],
        out_specs=pl.BlockSpec((1, 1, C, D), lambda e, b, f: (b, e, 0, 0)),
        out_shape=jax.ShapeDtypeStruct((B, E, C, D), jnp.bfloat16),
        scratch_shapes=[pltpu.VMEM((C, D), jnp.float32)],
        compiler_params=pltpu.CompilerParams(vmem_limit_bytes=VMEM_LIMIT),
        name="expert_ffn",
    )(xin, w_gate, w_up, w_down, gate)


COMBINE_TOKENS = 512
COMBINE_ROWS = 256


def _combine_kernel(tile_ref, blk_ref, first_ref, valid_ref, y_ref, tok_ref, o_ref):
    i = pl.program_id(0)

    @pl.when(first_ref[i] == 1)
    def _():
        o_ref[...] = jnp.zeros_like(o_ref)

    @pl.when(valid_ref[i] == 1)
    def _():
        base = tile_ref[i] * COMBINE_TOKENS
        t = base + lax.broadcasted_iota(jnp.int32, (COMBINE_TOKENS, COMBINE_ROWS), 0)
        onehot = jnp.where(tok_ref[0] == t, 1.0, 0.0).astype(jnp.bfloat16)
        o_ref[...] += jnp.dot(onehot, y_ref[...], preferred_element_type=jnp.float32)


def combine_routed(ys, tok_sorted, rows):
    R, D = ys.shape
    n_tiles, n_blk = rows // COMBINE_TOKENS, R // COMBINE_ROWS
    n_items = n_tiles + n_blk
    edges = jnp.arange(n_tiles + 1, dtype=jnp.int32) * COMBINE_TOKENS
    bounds = jnp.sum((tok_sorted[None, :] < edges[:, None]).astype(jnp.int32), axis=1)
    first_blk = jnp.minimum(bounds[:-1] // COMBINE_ROWS, n_blk - 1)
    last_blk = jnp.maximum((bounds[1:] - 1) // COMBINE_ROWS, first_blk)
    start = jnp.cumsum(last_blk - first_blk + 1) - (last_blk - first_blk + 1)
    item = jnp.arange(n_items, dtype=jnp.int32)
    tile = jnp.sum((start[None, :] <= item[:, None]).astype(jnp.int32), axis=1) - 1
    blk = first_blk[tile] + item - start[tile]
    valid = (blk <= last_blk[tile]).astype(jnp.int32)
    blk = jnp.minimum(blk, n_blk - 1).astype(jnp.int32)
    first = (item == start[tile]).astype(jnp.int32)
    grid_spec = pltpu.PrefetchScalarGridSpec(
        num_scalar_prefetch=4,
        grid=(n_items,),
        in_specs=[pl.BlockSpec((COMBINE_ROWS, D), lambda i, tile, blk, first, valid: (blk[i], 0)),
                  pl.BlockSpec((1, 1, COMBINE_ROWS), lambda i, tile, blk, first, valid: (blk[i], 0, 0))],
        out_specs=pl.BlockSpec((COMBINE_TOKENS, D), lambda i, tile, blk, first, valid: (tile[i], 0)),
    )
    return pl.pallas_call(
        _combine_kernel,
        grid_spec=grid_spec,
        out_shape=jax.ShapeDtypeStruct((rows, D), jnp.float32),
        compiler_params=pltpu.CompilerParams(vmem_limit_bytes=VMEM_LIMIT),
        name="combine_routed",
    )(tile, blk, first, valid, ys, tok_sorted.reshape(n_blk, 1, COMBINE_ROWS))


ROW_TILE = 256
NORM_TILE = 512


def _res_norm_kernel(*refs, has_res, has_proj, has_router, keep_x):
    it = iter(refs)
    x_ref = next(it)
    f_ref = next(it) if has_res else None
    w_ref = next(it) if has_proj else None
    gate_ref = next(it) if has_res else None
    g_ref, shift_ref, scale1_ref = next(it), next(it), next(it)
    wr_ref = next(it) if has_router else None
    xo_ref = next(it) if keep_x else None
    h_ref = next(it)
    aff_ref = next(it) if has_router else None

    tm = x_ref.shape[0]
    nsplit = 2 if has_proj else 1
    for r0 in range(0, tm, tm // nsplit):
        rs = slice(r0, r0 + tm // nsplit)
        x = x_ref[rs, :]
        if has_res:
            f = jnp.dot(f_ref[rs, :], w_ref[...], preferred_element_type=jnp.float32) if has_proj else f_ref[rs, :]
            x = x + gate_ref[0] * f
        if keep_x:
            xo_ref[rs, :] = x
        y = x * lax.rsqrt(jnp.mean(x * x, axis=-1, keepdims=True) + EPS) * g_ref[...]
        h = (y * scale1_ref[0] + shift_ref[0]).astype(h_ref.dtype)
        h_ref[rs, :] = h
        if has_router:
            logits = jnp.dot(h, wr_ref[...], preferred_element_type=jnp.float32)
            e = jnp.exp(logits - jnp.max(logits, axis=-1, keepdims=True))
            aff_ref[rs, :] = e / jnp.sum(e, axis=-1, keepdims=True)


def res_norm(X, g, shift, scale1, n_lat, n_groups, res=None, w_router=None, out_dtype=jnp.bfloat16, rows_out=None):
    D = X.shape[1]
    keep_x = res is not None and rows_out is None
    rows = X.shape[0] if rows_out is None else rows_out
    tm = NORM_TILE
    per_sample = n_lat // tm
    grp = lambda t: (jnp.minimum(t // per_sample, n_groups - 1), 0, 0)
    row_spec = pl.BlockSpec((tm, D), lambda t: (t, 0))
    vec_spec = pl.BlockSpec((1, 1, D), grp)
    ins, in_specs = [X], [row_spec]
    has_proj = res is not None and isinstance(res[0], tuple)
    if has_proj:
        (A, W), gate = res
        ins += [A, W, gate]
        in_specs += [pl.BlockSpec((tm, A.shape[1]), lambda t: (t, 0)), pl.BlockSpec(W.shape, lambda t: (0, 0)), vec_spec]
    elif res is not None:
        F, gate = res
        ins += [F, gate]
        in_specs += [row_spec, vec_spec]
    ins += [g.reshape(1, D), shift, scale1]
    in_specs += [pl.BlockSpec((1, D), lambda t: (0, 0)), vec_spec, vec_spec]
    outs, out_specs = [], []
    if keep_x:
        outs.append(jax.ShapeDtypeStruct((rows, D), X.dtype))
        out_specs.append(row_spec)
    outs.append(jax.ShapeDtypeStruct((rows, D), out_dtype))
    out_specs.append(row_spec)
    if w_router is not None:
        E = w_router.shape[1]
        ins.append(w_router)
        in_specs.append(pl.BlockSpec((D, E), lambda t: (0, 0)))
        outs.append(jax.ShapeDtypeStruct((rows, E), jnp.float32))
        out_specs.append(pl.BlockSpec((tm, E), lambda t: (t, 0)))
    return pl.pallas_call(
        functools.partial(_res_norm_kernel, has_res=res is not None, has_proj=has_proj,
                          has_router=w_router is not None, keep_x=keep_x),
        grid=(rows // tm,),
        in_specs=in_specs,
        out_specs=out_specs,
        out_shape=outs,
        input_output_aliases={0: 0} if keep_x else {},
        compiler_params=pltpu.CompilerParams(vmem_limit_bytes=VMEM_LIMIT),
        name="res_norm",
    )(*ins)


POOL_HALO = 8


def _even_mix_kernel(hf_ref, hb_ref, o_ref, u_ref, up_ref, un_ref, hg_ref, pw_ref, ps_ref, mix_ref, xe_ref,
                     *, n_lat, lat_tiles, n_ctx):
    tm = ROW_TILE
    H, dh = MLSTM_HEADS, MLSTM_HEAD_DIM
    t = pl.program_id(0)
    hh = hf_ref[...] + hb_ref[...]
    for h in range(H):
        sl = slice(h * dh, (h + 1) * dh)
        x = hh[:, sl]
        hn = x * lax.rsqrt(jnp.mean(x * x, axis=-1, keepdims=True) + EPS) * hg_ref[:, sl]
        mix_ref[:, sl] = (hn * jax.nn.sigmoid(o_ref[:, sl])).astype(mix_ref.dtype)

    is_lat = t < lat_tiles
    n_seq = jnp.where(is_lat, n_lat, n_ctx)
    p0 = jnp.where(is_lat, (t % (n_lat // tm)) * tm, ((t - lat_tiles) % (n_ctx // tm)) * tm)
    pos_prev = p0 - POOL_HALO + lax.broadcasted_iota(jnp.int32, (POOL_HALO, 1), 0)
    pos_next = p0 + tm + lax.broadcasted_iota(jnp.int32, (POOL_HALO, 1), 0)
    xe_ref[0:POOL_HALO, :] = jnp.where(pos_prev >= 0, up_ref[...], 0.0)
    xe_ref[POOL_HALO:POOL_HALO + tm, :] = u_ref[...]
    xe_ref[POOL_HALO + tm:, :] = jnp.where(pos_next < n_seq, un_ref[...], 0.0)
    pos = p0 + lax.broadcasted_iota(jnp.int32, (tm, 1), 0)
    for gi, w in enumerate(POOL_WINDOWS):
        cs = slice(gi * POOL_GROUP_DIM, (gi + 1) * POOL_GROUP_DIM)
        acc = xe_ref[POOL_HALO - w // 2:POOL_HALO - w // 2 + tm, cs]
        for d in range(1, w):
            acc = acc + xe_ref[POOL_HALO - w // 2 + d:POOL_HALO - w // 2 + d + tm, cs]
        cnt = (jnp.minimum(pos + w // 2, n_seq) - jnp.maximum(pos - w // 2, 0)).astype(jnp.float32)
        diff = (acc / cnt - u_ref[:, cs]).astype(jnp.bfloat16)
        y = jnp.dot(diff, pw_ref[gi], preferred_element_type=jnp.float32) * ps_ref[:, cs]
        mix_ref[:, MLSTM_WIDTH + gi * POOL_GROUP_DIM:MLSTM_WIDTH + (gi + 1) * POOL_GROUP_DIM] = y.astype(mix_ref.dtype)


def even_mix(hf, hb, P, head_g, pool_w, pool_scale, B, n_lat, n_ctx):
    rows = hf.shape[0]
    tm = ROW_TILE
    hpt = tm // POOL_HALO
    last_halo = rows // POOL_HALO - 1
    wide = pl.BlockSpec((tm, MLSTM_WIDTH), lambda t: (t, 0))
    return pl.pallas_call(
        functools.partial(_even_mix_kernel, n_lat=n_lat, lat_tiles=B * n_lat // tm, n_ctx=n_ctx),
        grid=(rows // tm,),
        in_specs=[wide, wide,
                  pl.BlockSpec((tm, MLSTM_WIDTH), lambda t: (t, 1)),
                  pl.BlockSpec((tm, POOL_WIDTH), lambda t: (t, 2)),
                  pl.BlockSpec((POOL_HALO, POOL_WIDTH), lambda t: (jnp.maximum(t * hpt - 1, 0), 2)),
                  pl.BlockSpec((POOL_HALO, POOL_WIDTH), lambda t: (jnp.minimum((t + 1) * hpt, last_halo), 2)),
                  pl.BlockSpec((1, MLSTM_WIDTH), lambda t: (0, 0)),
                  pl.BlockSpec((POOL_GROUPS, POOL_GROUP_DIM, POOL_GROUP_DIM), lambda t: (0, 0, 0)),
                  pl.BlockSpec((1, POOL_WIDTH), lambda t: (0, 0))],
        out_specs=pl.BlockSpec((tm, MLSTM_WIDTH + POOL_WIDTH), lambda t: (t, 0)),
        out_shape=jax.ShapeDtypeStruct((rows, MLSTM_WIDTH + POOL_WIDTH), jnp.bfloat16),
        scratch_shapes=[pltpu.VMEM((tm + 2 * POOL_HALO, POOL_WIDTH), jnp.float32)],
        compiler_params=pltpu.CompilerParams(vmem_limit_bytes=VMEM_LIMIT),
        name="even_mix",
    )(hf, hb, P, P, P, P, head_g.reshape(1, -1), pool_w.astype(jnp.bfloat16), pool_scale.reshape(1, -1))


def mlstm_pool_mixer(h, w_in, gate_b, head_g, pool_w, pool_scale, w_out, rope, B, n_lat, n_ctx):
    H = MLSTM_HEADS
    W4 = 4 * MLSTM_WIDTH
    w_in = w_in.astype(jnp.bfloat16)
    QK = qk_projection(h, w_in[:, :2 * MLSTM_WIDTH], rope, B, n_lat)
    P = matmul(h, jnp.concatenate([w_in[:, 2 * MLSTM_WIDTH:W4], w_in[:, W4 + 4 * H:]], axis=1))
    G = matmul(h, w_in[:, W4:W4 + 4 * H])
    hf, hb = mlstm_bidir(QK, P, gates_layout(G), gate_b, B, n_lat, n_ctx)
    mix = even_mix(hf, hb, P, head_g, pool_w, pool_scale, B, n_lat, n_ctx)
    return mix, w_out.astype(jnp.bfloat16)


def na_mixer(h, w_in, rpb, w_out, B, n_lat, n_ctx):
    qkv = matmul(h, w_in.astype(jnp.bfloat16), jnp.bfloat16)
    return na_attention(qkv, rpb, B, n_lat, n_ctx), w_out.astype(jnp.bfloat16)


def expert_choice_ffn(h, aff, w_gate, w_up, w_down, layer, B, n_lat, n_ctx):
    E = N_EXPERTS
    rows, D = h.shape
    picks = []
    for lo, n in ((0, n_lat), (B * n_lat, n_ctx)):
        a = aff[lo:lo + B * n].reshape(B, n, E)
        gate, idx = lax.top_k(jnp.swapaxes(a, 1, 2), CAPACITY_FACTOR * n // E)
        picks.append((gate, idx + lo + (jnp.arange(B) * n)[:, None, None]))
    gate = jnp.concatenate([p[0] for p in picks], axis=-1)
    idx = jnp.concatenate([p[1] for p in picks], axis=-1)
    xin = h[idx]
    y = expert_ffn(xin, w_gate, w_up, w_down, layer, gate[..., None])
    tok = idx.reshape(-1)
    order = jnp.argsort(tok)
    return combine_routed(y.reshape(-1, D)[order], tok[order], rows)


def kernel(x, c, ctx, c_ctx, ada_w, ada_b, norm_g, final_g, ev_w_in, ev_gate_b, ev_head_g, ev_pool_w,
           ev_pool_scale, ev_w_out, na_w_in, na_rpb, na_w_out, moe_w_router, moe_w_gate, moe_w_up, moe_w_down):
    B, n_lat, D = x.shape
    n_ctx = ctx.shape[1]
    G = B + 1
    X = jnp.concatenate([x.reshape(B * n_lat, D), ctx.reshape(B * n_ctx, D)], axis=0)
    rope = rope_tables(n_lat)
    cc = jax.nn.silu(jnp.concatenate([c, c_ctx[None]], axis=0)).astype(jnp.bfloat16)
    cc = jnp.pad(cc, ((0, 16 - G), (0, 0)))
    norm = functools.partial(res_norm, n_lat=n_lat, n_groups=G)
    ada = ada_modulation(cc, ada_w, ada_b)[:, :G]
    res = None
    for i in range(DEPTH):
        j = i // 2
        sh, sc, ga, shf, scf, gaf = (v[:, None, :] for v in jnp.split(ada[i], 6, axis=-1))
        if res is None:
            h = norm(X, norm_g[i, 0], sh, 1 + sc)[0]
        else:
            X, h = norm(X, norm_g[i, 0], sh, 1 + sc, res=res)
        if i % 2 == 0:
            o = mlstm_pool_mixer(h, ev_w_in[j], ev_gate_b[j], ev_head_g[j], ev_pool_w[j], ev_pool_scale[j],
                                 ev_w_out[j], rope, B, n_lat, n_ctx)
        else:
            o = na_mixer(h, na_w_in[j], na_rpb[j], na_w_out[j], B, n_lat, n_ctx)
        X, h, aff = norm(X, norm_g[i, 1], shf, 1 + scf, res=(o, ga), w_router=moe_w_router[i].astype(jnp.bfloat16))
        f = expert_choice_ffn(h, aff, moe_w_gate, moe_w_up, moe_w_down, i, B, n_lat, n_ctx)
        res = (f, gaf)
    zeros = jnp.zeros((G, 1, D), jnp.float32)
    out, = norm(X, final_g, zeros, zeros + 1, res=res, out_dtype=jnp.float32, rows_out=B * n_lat)
    return out.reshape(B, n_lat, D)
```

```python
import functools

import jax
import jax.numpy as jnp
import numpy as np
from jax import lax
from jax.experimental import pallas as pl
from jax.experimental.pallas import tpu as pltpu

D_MODEL = 2048
DEPTH = 4
GRID_W = 64
MLSTM_WIDTH = D_MODEL // 2
MLSTM_HEADS = 8
MLSTM_HEAD_DIM = MLSTM_WIDTH // MLSTM_HEADS
MLSTM_CHUNK = 256
POOL_WIDTH = D_MODEL // 2
POOL_WINDOWS = (2, 4, 8, 16)
POOL_GROUPS = len(POOL_WINDOWS)
POOL_GROUP_DIM = POOL_WIDTH // POOL_GROUPS
NA_HEADS = 16
NA_HEAD_DIM = D_MODEL // NA_HEADS
NA_ROWS = 8
NA_COLS = 16
RPB_R = 2 * NA_ROWS - 1
RPB_C = 2 * NA_COLS - 1
N_EXPERTS = 16
CAPACITY_FACTOR = 2
ROPE_BASE = 10000.0
EPS = 1e-6
MASKED = -1e30
VMEM_LIMIT = 48 * 1024 * 1024


def _mm_kernel(a_ref, w_ref, o_ref):
    o_ref[...] = jnp.dot(a_ref[...], w_ref[...], preferred_element_type=jnp.float32).astype(o_ref.dtype)


def matmul(a, w, out_dtype=jnp.float32, tm=512, tn=1024):
    M, K = a.shape
    N = w.shape[1]
    tm, tn = min(tm, M), min(tn, N)
    assert a.dtype == w.dtype == jnp.bfloat16 and M % tm == 0 and N % tn == 0
    return pl.pallas_call(
        _mm_kernel,
        grid=(N // tn, M // tm),
        in_specs=[pl.BlockSpec((tm, K), lambda j, i: (i, 0)),
                  pl.BlockSpec((K, tn), lambda j, i: (0, j))],
        out_specs=pl.BlockSpec((tm, tn), lambda j, i: (i, j)),
        out_shape=jax.ShapeDtypeStruct((M, N), out_dtype),
        compiler_params=pltpu.CompilerParams(vmem_limit_bytes=VMEM_LIMIT),
        name="matmul",
    )(a, w)


def _ada_kernel(c_ref, w_ref, b_ref, o_ref):
    o_ref[0] = jnp.dot(c_ref[...], w_ref[0].astype(jnp.bfloat16), preferred_element_type=jnp.float32) + b_ref[0]


def ada_modulation(cc, ada_w, ada_b, tn=1024):
    M, D = cc.shape
    depth, _, N = ada_w.shape
    return pl.pallas_call(
        _ada_kernel,
        grid=(depth, N // tn),
        in_specs=[pl.BlockSpec((M, D), lambda l, j: (0, 0)),
                  pl.BlockSpec((1, D, tn), lambda l, j: (l, 0, j)),
                  pl.BlockSpec((1, 1, tn), lambda l, j: (l, 0, j))],
        out_specs=pl.BlockSpec((1, M, tn), lambda l, j: (l, 0, j)),
        out_shape=jax.ShapeDtypeStruct((depth, M, N), jnp.float32),
        compiler_params=pltpu.CompilerParams(vmem_limit_bytes=VMEM_LIMIT),
        name="ada_modulation",
    )(cc, ada_w, ada_b.reshape(depth, 1, N))


NA_QROWS = 4
NA_KROWS = 12
NA_STEP_HEADS = 4
NA_KEY_CHUNK = 256
LOG2E = 1.4426950408889634


def _toeplitz(v):
    W = GRID_W
    lead = v.shape[:-1]
    y = jnp.concatenate([v, jnp.zeros(lead + (1,), v.dtype)], axis=-1)
    t = jnp.tile(y, (1,) * len(lead) + (W,))[..., : W * (2 * W - 1)].reshape(lead + (W, 2 * W - 1))
    return t[..., W - 1:]


def na_bias_tables(rpb, R):
    H = rpb.shape[0]
    W = GRID_W
    nblk = R // NA_QROWS
    r0 = np.arange(nblk) * NA_QROWS
    ws = np.clip(r0 - NA_ROWS // 2, 0, R - NA_KROWS)
    types, tmap = np.unique(ws - r0, return_inverse=True)
    dc = np.arange(-(W - 1), W)
    in_rpb = np.abs(dc + 0) <= NA_COLS - 1
    cols_idx = np.clip(dc + NA_COLS - 1, 0, RPB_C - 1)
    v = jnp.where(jnp.asarray(in_rpb), rpb[:, :, cols_idx], MASKED)
    toe = _toeplitz(v)
    c = np.arange(W)[:, None]
    kc = np.arange(W)[None, :]
    cs = np.clip(c - NA_COLS // 2, 0, W - NA_COLS)
    col_ok = (kc >= cs) & (kc < cs + NA_COLS)
    toe = jnp.where(jnp.asarray(col_ok), toe, MASKED)
    toe = jnp.concatenate([toe, jnp.full((H, 1, W, W), MASKED, toe.dtype)], axis=1)
    qi = np.arange(NA_QROWS)[:, None]
    kj = np.arange(NA_KROWS)[None, :]
    tabs = []
    for ti in range(len(types)):
        blk = int(np.nonzero(tmap == ti)[0][0])
        r = r0[blk] + qi
        kr = ws[blk] + kj
        rs = np.clip(r - NA_ROWS // 2, 0, R - NA_ROWS)
        row_ok = (kr >= rs) & (kr < rs + NA_ROWS)
        dr = np.where(row_ok, kr - r + NA_ROWS - 1, RPB_R)
        tab = toe[:, dr]
        tabs.append(tab.transpose(0, 1, 3, 2, 4).reshape(H, NA_QROWS * W, NA_KROWS * W))
    tabs.append(jnp.full((H, NA_QROWS * W, NA_KROWS * W), MASKED, jnp.float32))
    return jnp.stack(tabs, axis=1).astype(jnp.float32), tmap.astype(np.int32), ws.astype(np.int32)


def _na_kernel(tmap_ref, ws_ref, q_ref, k_ref, v_ref, kc_ref, vc_ref, bias_ref, o_ref, *, qscale):
    rb = pl.program_id(2)
    start = pl.multiple_of(ws_ref[rb] * GRID_W, GRID_W)
    dh = NA_HEAD_DIM
    tk = NA_KROWS * GRID_W
    dn = (((1,), (1,)), ((), ()))
    ones_w = jnp.ones((NA_KEY_CHUNK, dh), jnp.bfloat16)
    ones_c = jnp.ones((kc_ref.shape[0], dh), jnp.bfloat16)
    for hh in range(NA_STEP_HEADS):
        ls = slice(hh * dh, (hh + 1) * dh)
        q = (q_ref[:, ls].astype(jnp.float32) * qscale).astype(jnp.bfloat16)
        vc = jnp.concatenate([vc_ref[:, ls], ones_c], axis=1)
        s = lax.dot_general(q, kc_ref[:, ls], dn, preferred_element_type=jnp.float32)
        m = jnp.max(s, axis=-1, keepdims=True)
        ol = jnp.dot(jnp.exp2(s - m).astype(jnp.bfloat16), vc, preferred_element_type=jnp.float32)
        for c0 in range(0, tk, NA_KEY_CHUNK):
            rows = pl.ds(pl.multiple_of(start + c0, GRID_W), NA_KEY_CHUNK)
            kw = k_ref[rows, ls]
            vw = jnp.concatenate([v_ref[rows, ls], ones_w], axis=1)
            s = lax.dot_general(q, kw, dn, preferred_element_type=jnp.float32) + bias_ref[hh, 0, :, c0:c0 + NA_KEY_CHUNK]
            m_new = jnp.maximum(m, jnp.max(s, axis=-1, keepdims=True))
            p = jnp.exp2(s - m_new).astype(jnp.bfloat16)
            ol = ol * jnp.exp2(m - m_new) + jnp.dot(p, vw, preferred_element_type=jnp.float32)
            m = m_new
        o_ref[:, ls] = (ol[:, :dh] / ol[:, dh:dh + 1]).astype(o_ref.dtype)


def na_attention(qkv, rpb, B, n_lat, n_ctx):
    rows = qkv.shape[0]
    H, dh = NA_HEADS, NA_HEAD_DIM
    HS = NA_STEP_HEADS
    HG = H // HS
    R = n_lat // GRID_W
    nblk = R // NA_QROWS
    tq = NA_QROWS * GRID_W
    tk = NA_KROWS * GRID_W
    assert n_ctx == tq and n_lat % tq == 0
    bias, tmap, ws = na_bias_tables(rpb, R)
    bias = jnp.maximum(bias * LOG2E, MASKED)
    tmap = np.concatenate([tmap, [bias.shape[1] - 1]]).astype(np.int32)
    ws = np.concatenate([ws, [0]]).astype(np.int32)

    def qblk(b, r):
        return jnp.where(r < nblk, b * nblk + r, B * nblk + b)

    grid_spec = pltpu.PrefetchScalarGridSpec(
        num_scalar_prefetch=2,
        grid=(B, HG, nblk + 1),
        in_specs=[
            pl.BlockSpec((tq, HS * dh), lambda b, h, r, tm, ws: (qblk(b, r), h)),
            pl.BlockSpec((n_lat, HS * dh), lambda b, h, r, tm, ws: (b, HG + h)),
            pl.BlockSpec((n_lat, HS * dh), lambda b, h, r, tm, ws: (b, 2 * HG + h)),
            pl.BlockSpec((n_ctx, HS * dh), lambda b, h, r, tm, ws: (B * nblk + b, HG + h)),
            pl.BlockSpec((n_ctx, HS * dh), lambda b, h, r, tm, ws: (B * nblk + b, 2 * HG + h)),
            pl.BlockSpec((HS, 1, tq, tk), lambda b, h, r, tm, ws: (h, tm[r], 0, 0)),
        ],
        out_specs=pl.BlockSpec((tq, HS * dh), lambda b, h, r, tm, ws: (qblk(b, r), h)),
    )
    return pl.pallas_call(
        functools.partial(_na_kernel, qscale=dh ** -0.5 * LOG2E),
        grid_spec=grid_spec,
        out_shape=jax.ShapeDtypeStruct((rows, H * dh), jnp.bfloat16),
        compiler_params=pltpu.CompilerParams(vmem_limit_bytes=VMEM_LIMIT),
        name="na_attention",
    )(jnp.asarray(tmap), jnp.asarray(ws), qkv, qkv, qkv, qkv, qkv, bias)


MLSTM_STEP_HEADS = 1
ROPE_TILE = 512


def rope_tables(n):
    dh = MLSTM_HEAD_DIM
    quarter = dh // 4
    t = np.arange(n)
    pos = np.stack([t // GRID_W, t % GRID_W], axis=1).astype(np.float32)
    lane = np.arange(dh)
    inv = jnp.asarray(ROPE_BASE, jnp.float32) ** (-jnp.arange(quarter, dtype=jnp.float32) / quarter)
    ang = jnp.asarray(pos)[:, lane // (dh // 2)] * inv[lane % quarter][None, :]
    cos, sin = jnp.cos(ang), jnp.sin(ang)
    first = jnp.asarray((lane % (dh // 2)) < quarter)[None, :]
    sin_up = jnp.where(first, -sin, 0.0)
    sin_dn = jnp.where(first, 0.0, sin)
    pad = lambda a, v: jnp.concatenate([a, jnp.full((ROPE_TILE, dh), v, jnp.float32)], axis=0)
    return pad(cos, 1.0), pad(sin_up, 0.0), pad(sin_dn, 0.0)


def _log_sigmoid(x):
    return jnp.minimum(x, 0.0) - jnp.log1p(jnp.exp(-jnp.abs(x)))


def _mlstm_chunk(qb, kb, v, gt, b_i, b_f, state, rev):
    L = MLSTM_CHUNK
    dh = MLSTM_HEAD_DIM
    f32, bf16 = jnp.float32, jnp.bfloat16
    Cn, m = state

    row = lax.broadcasted_iota(jnp.int32, gt.shape, 0)
    gl = jnp.where(row == 0, gt + b_i, jnp.where(row == 1, _log_sigmoid(gt + b_f), 0.0))
    glT = gl.T
    li_row, lf_row = gl[0:1], gl[1:2]
    li_col, lf_col = glT[:, 0:1], glT[:, 1:2]
    jj = lax.broadcasted_iota(jnp.int32, (L, L), 0)
    ss = lax.broadcasted_iota(jnp.int32, (L, L), 1)
    tri = (ss >= jj) if rev else (ss <= jj)
    tri_t = (ss <= jj) if rev else (ss >= jj)
    b_col = jnp.sum(jnp.where(tri, lf_row, 0.0), axis=1, keepdims=True)
    b_row = jnp.sum(jnp.where(tri_t, lf_col, 0.0), axis=0, keepdims=True)
    g = jnp.sum(lf_row, axis=1, keepdims=True)

    Dm = jnp.where(tri, b_col - b_row + li_row, MASKED)
    inter = b_col + m
    m_j = jnp.maximum(inter, jnp.max(Dm, axis=1, keepdims=True))
    nt = (((1,), (1,)), ((), ()))
    S = lax.dot_general(qb, kb, nt, preferred_element_type=f32) * jnp.exp(Dm - m_j)
    w_int = jnp.exp(inter - m_j)
    qCn = lax.dot_general(qb, Cn.astype(bf16), nt, preferred_element_type=f32)
    num = jnp.dot(S.astype(bf16), v.astype(bf16), preferred_element_type=f32) + w_int * qCn[:, :dh]
    den = jnp.sum(S, axis=1, keepdims=True) + w_int * qCn[:, dh:dh + 1]
    out = num / jnp.maximum(jnp.abs(den), jnp.exp(-m_j))

    a_row = g - b_row + li_row
    a_col = g - b_col + li_col
    m_loc = jnp.max(a_row, axis=1, keepdims=True)
    wa = jnp.exp(a_col - m_loc)
    C_loc = lax.dot_general((v * wa).astype(bf16), kb, (((0,), (0,)), ((), ())), preferred_element_type=f32)
    n_loc = jnp.sum(kb.astype(f32) * wa, axis=0, keepdims=True)
    upd = jnp.concatenate([C_loc, jnp.broadcast_to(n_loc, (8, dh))], axis=0)
    m_new = jnp.maximum(g + m, m_loc)
    sp = jnp.exp(g + m - m_new)
    sl = jnp.exp(m_loc - m_new)
    return out, (sp * Cn + sl * upd, m_new)


def _mlstm_direction(q_ref, k_ref, v_ref, g_ref, gb_ref, state_refs, o_ref, rev):
    dh = MLSTM_HEAD_DIM
    d = 2 if rev else 0
    for hh in range(MLSTM_STEP_HEADS):
        h = pl.program_id(1) * MLSTM_STEP_HEADS + hh
        ls = slice(hh * dh, (hh + 1) * dh)
        Cn_ref, m_ref = state_refs[2 * hh], state_refs[2 * hh + 1]
        out, (Cn, m) = _mlstm_chunk(q_ref[:, ls], k_ref[:, ls], v_ref[:, ls], g_ref[0, hh], gb_ref[d, h],
                                    gb_ref[d + 1, h], (Cn_ref[...], m_ref[...]), rev)
        o_ref[:, ls] = out
        Cn_ref[...] = Cn
        m_ref[...] = m


def _mlstm_kernel(gb_ref, qf, kf, vf, gf, qb, kb, vb, gbk, of, ob, *state_refs):
    @pl.when(pl.program_id(2) == 0)
    def _():
        for r in state_refs:
            r[...] = jnp.zeros_like(r)

    half = len(state_refs) // 2
    _mlstm_direction(qf, kf, vf, gf, gb_ref, state_refs[:half], of, False)
    _mlstm_direction(qb, kb, vb, gbk, gb_ref, state_refs[half:], ob, True)


def _qk_rope_kernel(a_ref, w_ref, cos_ref, su_ref, sd_ref, o_ref):
    dh = MLSTM_HEAD_DIM
    scale = jnp.where(pl.program_id(0) == 0, 1.0, dh ** -0.5)
    half = a_ref.shape[0] // 2
    for r0 in (0, half):
        rs = slice(r0, r0 + half)
        acc = jnp.dot(a_ref[rs, :], w_ref[...], preferred_element_type=jnp.float32)
        cos, su, sd = cos_ref[rs, :], su_ref[rs, :], sd_ref[rs, :]
        for hs in range(MLSTM_HEADS):
            ls = slice(hs * dh, (hs + 1) * dh)
            x = acc[:, ls] * scale
            y = x * cos + pltpu.roll(x, dh - dh // 4, 1) * su + pltpu.roll(x, dh // 4, 1) * sd
            o_ref[rs, ls] = y.astype(o_ref.dtype)


def qk_projection(h, w_qk, rope, B, n_lat):
    rows, K = h.shape
    tm, tn = ROPE_TILE, MLSTM_WIDTH
    dh = MLSTM_HEAD_DIM
    per_sample, lat_tiles = n_lat // tm, B * n_lat // tm
    tab = pl.BlockSpec((tm, dh), lambda j, i: (jnp.where(i < lat_tiles, i % per_sample, per_sample), 0))
    return pl.pallas_call(
        _qk_rope_kernel,
        grid=(2, rows // tm),
        in_specs=[pl.BlockSpec((tm, K), lambda j, i: (i, 0)), pl.BlockSpec((K, tn), lambda j, i: (0, j)), tab, tab, tab],
        out_specs=pl.BlockSpec((tm, tn), lambda j, i: (i, j)),
        out_shape=jax.ShapeDtypeStruct((rows, 2 * tn), jnp.bfloat16),
        compiler_params=pltpu.CompilerParams(vmem_limit_bytes=VMEM_LIMIT),
        name="qk_projection",
    )(h, w_qk, *rope)


def mlstm_bidir(QK, P, GT, gate_b, B, n_lat, n_ctx):
    H, dh = MLSTM_HEADS, MLSTM_HEAD_DIM
    HS = MLSTM_STEP_HEADS
    HG = H // HS
    L = MLSTM_CHUNK
    rows = P.shape[0]
    NB, CB = n_lat // L, n_ctx // L

    def blk(b, s, rev):
        sc = (CB - 1 - s) if rev else s
        sl = (NB - 1 - (s - CB)) if rev else (s - CB)
        return jnp.where(s < CB, B * NB + b * CB + sc, b * NB + sl)

    def specs(rev):
        d = 1 if rev else 0
        return [
            pl.BlockSpec((L, HS * dh), lambda b, h, s, gb: (blk(b, s, rev), h)),
            pl.BlockSpec((L, HS * dh), lambda b, h, s, gb: (blk(b, s, rev), HG + h)),
            pl.BlockSpec((L, HS * dh), lambda b, h, s, gb: (blk(b, s, rev), h)),
            pl.BlockSpec((1, HS, 8, L), lambda b, h, s, gb: (d, h, 0, blk(b, s, rev))),
        ]

    state = [pltpu.VMEM((dh + 8, dh), jnp.float32), pltpu.VMEM((1, 1), jnp.float32)] * (2 * HS)
    grid_spec = pltpu.PrefetchScalarGridSpec(
        num_scalar_prefetch=1,
        grid=(B, HG, NB + CB),
        in_specs=specs(False) + specs(True),
        out_specs=[pl.BlockSpec((L, HS * dh), lambda b, h, s, gb: (blk(b, s, False), h)),
                   pl.BlockSpec((L, HS * dh), lambda b, h, s, gb: (blk(b, s, True), h))],
        scratch_shapes=state,
    )
    return pl.pallas_call(
        _mlstm_kernel,
        grid_spec=grid_spec,
        out_shape=[jax.ShapeDtypeStruct((rows, H * dh), jnp.float32)] * 2,
        name="mlstm_bidir",
    )(gate_b, QK, QK, P, GT, QK, QK, P, GT)


def gates_layout(G):
    rows = G.shape[0]
    g = G.T.reshape(2, 2, MLSTM_HEADS, rows).transpose(0, 2, 1, 3)
    return jnp.pad(g, ((0, 0), (0, 0), (0, 6), (0, 0)))


FFN_TF = 256
FFN_ROW_SPLIT = 2


def _ffn_kernel(x_ref, wg_ref, wu_ref, wd_ref, gate_ref, o_ref, acc_ref):
    f = pl.program_id(2)
    @pl.when(f == 0)
    def _():
        acc_ref[...] = jnp.zeros_like(acc_ref)

    wg, wu, wd = (r[0, 0].astype(jnp.bfloat16) for r in (wg_ref, wu_ref, wd_ref))
    C = acc_ref.shape[0]
    for r0 in range(0, C, C // FFN_ROW_SPLIT):
        rs = slice(r0, r0 + C // FFN_ROW_SPLIT)
        x = x_ref[0, 0, rs, :]
        a = jnp.dot(x, wg, preferred_element_type=jnp.float32)
        u = jnp.dot(x, wu, preferred_element_type=jnp.float32)
        act = (a * jax.nn.sigmoid(a) * u).astype(jnp.bfloat16)
        acc_ref[rs, :] += jnp.dot(act, wd, preferred_element_type=jnp.float32)

    @pl.when(f == pl.num_programs(2) - 1)
    def _():
        o_ref[0, 0] = (acc_ref[...] * gate_ref[0, 0]).astype(o_ref.dtype)


def expert_ffn(xin, w_gate, w_up, w_down, layer, gate):
    B, E, C, D = xin.shape
    F = w_gate.shape[-1]
    return pl.pallas_call(
        _ffn_kernel,
        grid=(E, B, F // FFN_TF),
        in_specs=[pl.BlockSpec((1, 1, C, D), lambda e, b, f: (b, e, 0, 0)),
                  pl.BlockSpec((1, 1, D, FFN_TF), lambda e, b, f: (layer, e, 0, f)),
                  pl.BlockSpec((1, 1, D, FFN_TF), lambda e, b, f: (layer, e, 0, f)),
                  pl.BlockSpec((1, 1, FFN_TF, D), lambda e, b, f: (layer, e, f, 0)),
                  pl.BlockSpec((1, 1, C, 1), lambda e, b, f: (b, e, 0, 0))],
        out_specs=pl.BlockSpec((1, 1, C, D), lambda e, b, f: (b, e, 0, 0)),
        out_shape=jax.ShapeDtypeStruct((B, E, C, D), jnp.bfloat16),
        scratch_shapes=[pltpu.VMEM((C, D), jnp.float32)],
        compiler_params=pltpu.CompilerParams(vmem_limit_bytes=VMEM_LIMIT),
        name="expert_ffn",
    )(xin, w_gate, w_up, w_down, gate)


COMBINE_TOKENS = 512
COMBINE_ROWS = 256


def _combine_kernel(tile_ref, blk_ref, first_ref, valid_ref, y_ref, tok_ref, o_ref):
    i = pl.program_id(0)

    @pl.when(first_ref[i] == 1)
    def _():
        o_ref[...] = jnp.zeros_like(o_ref)

    @pl.when(valid_ref[i] == 1)
    def _():
        base = tile_ref[i] * COMBINE_TOKENS
        t = base + lax.broadcasted_iota(jnp.int32, (COMBINE_TOKENS, COMBINE_ROWS), 0)
        onehot = jnp.where(tok_ref[0] == t, 1.0, 0.0).astype(jnp.bfloat16)
        o_ref[...] += jnp.dot(onehot, y_ref[...], preferred_element_type=jnp.float32)


def combine_routed(ys, tok_sorted, rows):
    R, D = ys.shape
    n_tiles, n_blk = rows // COMBINE_TOKENS, R // COMBINE_ROWS
    n_items = n_tiles + n_blk
    edges = jnp.arange(n_tiles + 1, dtype=jnp.int32) * COMBINE_TOKENS
    bounds = jnp.sum((tok_sorted[None, :] < edges[:, None]).astype(jnp.int32), axis=1)
    first_blk = jnp.minimum(bounds[:-1] // COMBINE_ROWS, n_blk - 1)
    last_blk = jnp.maximum((bounds[1:] - 1) // COMBINE_ROWS, first_blk)
    start = jnp.cumsum(last_blk - first_blk + 1) - (last_blk - first_blk + 1)
    item = jnp.arange(n_items, dtype=jnp.int32)
    tile = jnp.sum((start[None, :] <= item[:, None]).astype(jnp.int32), axis=1) - 1
    blk = first_blk[tile] + item - start[tile]
    valid = (blk <= last_blk[tile]).astype(jnp.int32)
    blk = jnp.minimum(blk, n_blk - 1).astype(jnp.int32)
    first = (item == start[tile]).astype(jnp.int32)
    grid_spec = pltpu.PrefetchScalarGridSpec(
        num_scalar_prefetch=4,
        grid=(n_items,),
        in_specs=[pl.BlockSpec((COMBINE_ROWS, D), lambda i, tile, blk, first, valid: (blk[i], 0)),
                  pl.BlockSpec((1, 1, COMBINE_ROWS), lambda i, tile, blk, first, valid: (blk[i], 0, 0))],
        out_specs=pl.BlockSpec((COMBINE_TOKENS, D), lambda i, tile, blk, first, valid: (tile[i], 0)),
    )
    return pl.pallas_call(
        _combine_kernel,
        grid_spec=grid_spec,
        out_shape=jax.ShapeDtypeStruct((rows, D), jnp.float32),
        compiler_params=pltpu.CompilerParams(vmem_limit_bytes=VMEM_LIMIT),
        name="combine_routed",
    )(tile, blk, first, valid, ys, tok_sorted.reshape(n_blk, 1, COMBINE_ROWS))


ROW_TILE = 256
NORM_TILE = 512


def _res_norm_kernel(*refs, has_res, has_proj, has_router, keep_x):
    it = iter(refs)
    x_ref = next(it)
    f_ref = next(it) if has_res else None
    w_ref = next(it) if has_proj else None
    gate_ref = next(it) if has_res else None
    g_ref, shift_ref, scale1_ref = next(it), next(it), next(it)
    wr_ref = next(it) if has_router else None
    xo_ref = next(it) if keep_x else None
    h_ref = next(it)
    aff_ref = next(it) if has_router else None

    tm = x_ref.shape[0]
    nsplit = 2 if has_proj else 1
    for r0 in range(0, tm, tm // nsplit):
        rs = slice(r0, r0 + tm // nsplit)
        x = x_ref[rs, :]
        if has_res:
            f = jnp.dot(f_ref[rs, :], w_ref[...], preferred_element_type=jnp.float32) if has_proj else f_ref[rs, :]
            x = x + gate_ref[0] * f
        if keep_x:
            xo_ref[rs, :] = x
        y = x * lax.rsqrt(jnp.mean(x * x, axis=-1, keepdims=True) + EPS) * g_ref[...]
        h = (y * scale1_ref[0] + shift_ref[0]).astype(h_ref.dtype)
        h_ref[rs, :] = h
        if has_router:
            logits = jnp.dot(h, wr_ref[...], preferred_element_type=jnp.float32)
            e = jnp.exp(logits - jnp.max(logits, axis=-1, keepdims=True))
            aff_ref[rs, :] = e / jnp.sum(e, axis=-1, keepdims=True)


def res_norm(X, g, shift, scale1, n_lat, n_groups, res=None, w_router=None, out_dtype=jnp.bfloat16, rows_out=None):
    D = X.shape[1]
    keep_x = res is not None and rows_out is None
    rows = X.shape[0] if rows_out is None else rows_out
    tm = NORM_TILE
    per_sample = n_lat // tm
    grp = lambda t: (jnp.minimum(t // per_sample, n_groups - 1), 0, 0)
    row_spec = pl.BlockSpec((tm, D), lambda t: (t, 0))
    vec_spec = pl.BlockSpec((1, 1, D), grp)
    ins, in_specs = [X], [row_spec]
    has_proj = res is not None and isinstance(res[0], tuple)
    if has_proj:
        (A, W), gate = res
        ins += [A, W, gate]
        in_specs += [pl.BlockSpec((tm, A.shape[1]), lambda t: (t, 0)), pl.BlockSpec(W.shape, lambda t: (0, 0)), vec_spec]
    elif res is not None:
        F, gate = res
        ins += [F, gate]
        in_specs += [row_spec, vec_spec]
    ins += [g.reshape(1, D), shift, scale1]
    in_specs += [pl.BlockSpec((1, D), lambda t: (0, 0)), vec_spec, vec_spec]
    outs, out_specs = [], []
    if keep_x:
        outs.append(jax.ShapeDtypeStruct((rows, D), X.dtype))
        out_specs.append(row_spec)
    outs.append(jax.ShapeDtypeStruct((rows, D), out_dtype))
    out_specs.append(row_spec)
    if w_router is not None:
        E = w_router.shape[1]
        ins.append(w_router)
        in_specs.append(pl.BlockSpec((D, E), lambda t: (0, 0)))
        outs.append(jax.ShapeDtypeStruct((rows, E), jnp.float32))
        out_specs.append(pl.BlockSpec((tm, E), lambda t: (t, 0)))
    return pl.pallas_call(
        functools.partial(_res_norm_kernel, has_res=res is not None, has_proj=has_proj,
                          has_router=w_router is not None, keep_x=keep_x),
        grid=(rows // tm,),
        in_specs=in_specs,
        out_specs=out_specs,
        out_shape=outs,
        input_output_aliases={0: 0} if keep_x else {},
        compiler_params=pltpu.CompilerParams(vmem_limit_bytes=VMEM_LIMIT),
        name="res_norm",
    )(*ins)


POOL_HALO = 8


def _even_mix_kernel(hf_ref, hb_ref, o_ref, u_ref, up_ref, un_ref, hg_ref, pw_ref, ps_ref, mix_ref, xe_ref,
                     *, n_lat, lat_tiles, n_ctx):
    tm = ROW_TILE
    H, dh = MLSTM_HEADS, MLSTM_HEAD_DIM
    t = pl.program_id(0)
    hh = hf_ref[...] + hb_ref[...]
    for h in range(H):
        sl = slice(h * dh, (h + 1) * dh)
        x = hh[:, sl]
        hn = x * lax.rsqrt(jnp.mean(x * x, axis=-1, keepdims=True) + EPS) * hg_ref[:, sl]
        mix_ref[:, sl] = (hn * jax.nn.sigmoid(o_ref[:, sl])).astype(mix_ref.dtype)

    is_lat = t < lat_tiles
    n_seq = jnp.where(is_lat, n_lat, n_ctx)
    p0 = jnp.where(is_lat, (t % (n_lat // tm)) * tm, ((t - lat_tiles) % (n_ctx // tm)) * tm)
    pos_prev = p0 - POOL_HALO + lax.broadcasted_iota(jnp.int32, (POOL_HALO, 1), 0)
    pos_next = p0 + tm + lax.broadcasted_iota(jnp.int32, (POOL_HALO, 1), 0)
    xe_ref[0:POOL_HALO, :] = jnp.where(pos_prev >= 0, up_ref[...], 0.0)
    xe_ref[POOL_HALO:POOL_HALO + tm, :] = u_ref[...]
    xe_ref[POOL_HALO + tm:, :] = jnp.where(pos_next < n_seq, un_ref[...], 0.0)
    pos = p0 + lax.broadcasted_iota(jnp.int32, (tm, 1), 0)
    for gi, w in enumerate(POOL_WINDOWS):
        cs = slice(gi * POOL_GROUP_DIM, (gi + 1) * POOL_GROUP_DIM)
        acc = xe_ref[POOL_HALO - w // 2:POOL_HALO - w // 2 + tm, cs]
        for d in range(1, w):
            acc = acc + xe_ref[POOL_HALO - w // 2 + d:POOL_HALO - w // 2 + d + tm, cs]
        cnt = (jnp.minimum(pos + w // 2, n_seq) - jnp.maximum(pos - w // 2, 0)).astype(jnp.float32)
        diff = (acc / cnt - u_ref[:, cs]).astype(jnp.bfloat16)
        y = jnp.dot(diff, pw_ref[gi], preferred_element_type=jnp.float32) * ps_ref[:, cs]
        mix_ref[:, MLSTM_WIDTH + gi * POOL_GROUP_DIM:MLSTM_WIDTH + (gi + 1) * POOL_GROUP_DIM] = y.astype(mix_ref.dtype)


def even_mix(hf, hb, P, head_g, pool_w, pool_scale, B, n_lat, n_ctx):
    rows = hf.shape[0]
    tm = ROW_TILE
    hpt = tm // POOL_HALO
    last_halo = rows // POOL_HALO - 1
    wide = pl.BlockSpec((tm, MLSTM_WIDTH), lambda t: (t, 0))
    return pl.pallas_call(
        functools.partial(_even_mix_kernel, n_lat=n_lat, lat_tiles=B * n_lat // tm, n_ctx=n_ctx),
        grid=(rows // tm,),
        in_specs=[wide, wide,
                  pl.BlockSpec((tm, MLSTM_WIDTH), lambda t: (t, 1)),
                  pl.BlockSpec((tm, POOL_WIDTH), lambda t: (t, 2)),
                  pl.BlockSpec((POOL_HALO, POOL_WIDTH), lambda t: (jnp.maximum(t * hpt - 1, 0), 2)),
                  pl.BlockSpec((POOL_HALO, POOL_WIDTH), lambda t: (jnp.minimum((t + 1) * hpt, last_halo), 2)),
                  pl.BlockSpec((1, MLSTM_WIDTH), lambda t: (0, 0)),
                  pl.BlockSpec((POOL_GROUPS, POOL_GROUP_DIM, POOL_GROUP_DIM), lambda t: (0, 0, 0)),
                  pl.BlockSpec((1, POOL_WIDTH), lambda t: (0, 0))],
        out_specs=pl.BlockSpec((tm, MLSTM_WIDTH + POOL_WIDTH), lambda t: (t, 0)),
        out_shape=jax.ShapeDtypeStruct((rows, MLSTM_WIDTH + POOL_WIDTH), jnp.bfloat16),
        scratch_shapes=[pltpu.VMEM((tm + 2 * POOL_HALO, POOL_WIDTH), jnp.float32)],
        compiler_params=pltpu.CompilerParams(vmem_limit_bytes=VMEM_LIMIT),
        name="even_mix",
    )(hf, hb, P, P, P, P, head_g.reshape(1, -1), pool_w.astype(jnp.bfloat16), pool_scale.reshape(1, -1))


def mlstm_pool_mixer(h, w_in, gate_b, head_g, pool_w, pool_scale, w_out, rope, B, n_lat, n_ctx):
    H = MLSTM_HEADS
    W4 = 4 * MLSTM_WIDTH
    w_in = w_in.astype(jnp.bfloat16)
    QK = qk_projection(h, w_in[:, :2 * MLSTM_WIDTH], rope, B, n_lat)
    P = matmul(h, jnp.concatenate([w_in[:, 2 * MLSTM_WIDTH:W4], w_in[:, W4 + 4 * H:]], axis=1))
    G = matmul(h, w_in[:, W4:W4 + 4 * H])
    hf, hb = mlstm_bidir(QK, P, gates_layout(G), gate_b, B, n_lat, n_ctx)
    mix = even_mix(hf, hb, P, head_g, pool_w, pool_scale, B, n_lat, n_ctx)
    return mix, w_out.astype(jnp.bfloat16)


def na_mixer(h, w_in, rpb, w_out, B, n_lat, n_ctx):
    qkv = matmul(h, w_in.astype(jnp.bfloat16), jnp.bfloat16)
    return na_attention(qkv, rpb, B, n_lat, n_ctx), w_out.astype(jnp.bfloat16)


def expert_choice_ffn(h, aff, w_gate, w_up, w_down, layer, B, n_lat, n_ctx):
    E = N_EXPERTS
    rows, D = h.shape
    picks = []
    for lo, n in ((0, n_lat), (B * n_lat, n_ctx)):
        a = aff[lo:lo + B * n].reshape(B, n, E)
        gate, idx = lax.top_k(jnp.swapaxes(a, 1, 2), CAPACITY_FACTOR * n // E)
        picks.append((gate, idx + lo + (jnp.arange(B) * n)[:, None, None]))
    gate = jnp.concatenate([p[0] for p in picks], axis=-1)
    idx = jnp.concatenate([p[1] for p in picks], axis=-1)
    xin = h[idx]
    y = expert_ffn(xin, w_gate, w_up, w_down, layer, gate[..., None])
    tok = idx.reshape(-1)
    order = jnp.argsort(tok)
    return combine_routed(y.reshape(-1, D)[order], tok[order], rows)


def kernel(x, c, ctx, c_ctx, ada_w, ada_b, norm_g, final_g, ev_w_in, ev_gate_b, ev_head_g, ev_pool_w,
           ev_pool_scale, ev_w_out, na_w_in, na_rpb, na_w_out, moe_w_router, moe_w_gate, moe_w_up, moe_w_down):
    B, n_lat, D = x.shape
    n_ctx = ctx.shape[1]
    G = B + 1
    X = jnp.concatenate([x.reshape(B * n_lat, D), ctx.reshape(B * n_ctx, D)], axis=0)
    rope = rope_tables(n_lat)
    cc = jax.nn.silu(jnp.concatenate([c, c_ctx[None]], axis=0)).astype(jnp.bfloat16)
    cc = jnp.pad(cc, ((0, 16 - G), (0, 0)))
    norm = functools.partial(res_norm, n_lat=n_lat, n_groups=G)
    ada = ada_modulation(cc, ada_w, ada_b)[:, :G]
    res = None
    for i in range(DEPTH):
        j = i // 2
        sh, sc, ga, shf, scf, gaf = (v[:, None, :] for v in jnp.split(ada[i], 6, axis=-1))
        if res is None:
            h = norm(X, norm_g[i, 0], sh, 1 + sc)[0]
        else:
            X, h = norm(X, norm_g[i, 0], sh, 1 + sc, res=res)
        if i % 2 == 0:
            o = mlstm_pool_mixer(h, ev_w_in[j], ev_gate_b[j], ev_head_g[j], ev_pool_w[j], ev_pool_scale[j],
                                 ev_w_out[j], rope, B, n_lat, n_ctx)
        else:
            o = na_mixer(h, na_w_in[j], na_rpb[j], na_w_out[j], B, n_lat, n_ctx)
        X, h, aff = norm(X, norm_g[i, 1], shf, 1 + scf, res=(o, ga), w_router=moe_w_router[i].astype(jnp.bfloat16))
        f = expert_choice_ffn(h, aff, moe_w_gate, moe_w_up, moe_w_down, i, B, n_lat, n_ctx)
        res = (f, gaf)
    zeros = jnp.zeros((G, 1, D), jnp.float32)
    out, = norm(X, final_g, zeros, zeros + 1, res=res, out_dtype=jnp.float32, rows_out=B * n_lat)
    return out.reshape(B, n_lat, D)
```
